```python
import jax, jax.numpy as jnp
from jax import lax
import numpy as np

D_MODEL = 1024
BATCH = 8
SEQ = 4096
DEPTH = 4

N_HEADS = 8
HEAD_DIM = 64
ATTN_WIDTH = N_HEADS * HEAD_DIM
KV_LATENT = 128
IDX_HEADS = 4
IDX_DIM = 32
TOPK_MAX = 256
Q_BLOCK = 128
ATTN_SCALE = HEAD_DIM ** -0.5
IDX_W_SCALE = (IDX_HEADS ** -0.5) * (IDX_DIM ** -0.5)
POOL_WINDOWS = (2, 4, 8, 16)
POOL_GROUPS = 4
POOL_GROUP_DIM = 128
POOL_WIDTH = POOL_GROUPS * POOL_GROUP_DIM
N_EXPERTS = 32
TOP_K = 4
D_FF = D_MODEL // 2
MOE_BLOCK = 512
SWIGLU_LIMIT = 7.0
SWIGLU_ALPHA = 1.702
RMS_EPS = 1e-6
COL_WIDTHS = (ATTN_WIDTH, KV_LATENT, IDX_HEADS * IDX_DIM, IDX_DIM, IDX_HEADS, POOL_WIDTH, D_MODEL, D_MODEL)
IN_WIDTH = sum(COL_WIDTHS)
SPLIT_OFFSETS = tuple(int(v) for v in np.cumsum(COL_WIDTHS)[:-1])

kernel_name = 'hybrid_dsa_pool_moe_adaln'


def rms_norm(x, g):
    xf = x.astype(jnp.float32)
    y = xf * lax.rsqrt(jnp.mean(xf * xf, axis=-1, keepdims=True) + RMS_EPS)
    return (y * g.astype(jnp.float32)).astype(x.dtype)


def dsa_attention(q, ckv, qi, ki, wi, w_uk, w_uv):
    b_, L = q.shape[0], q.shape[1]
    k_sel = min(TOPK_MAX, L // 4)
    nb = L // Q_BLOCK
    key_pos = jnp.arange(L, dtype=jnp.int32)

    def to_blocks(t):
        return jnp.moveaxis(t.reshape((b_, nb, Q_BLOCK) + t.shape[2:]), 1, 0)

    def block(args):
        q_b, qi_b, wi_b, start = args
        qpos = start + jnp.arange(Q_BLOCK, dtype=jnp.int32)
        causal = key_pos[None, :] <= qpos[:, None]
        raw = jnp.einsum('bqhd,bsd->bqhs', qi_b, ki, preferred_element_type=jnp.float32)
        score = jnp.einsum('bqhs,bqh->bqs', jax.nn.relu(raw), wi_b.astype(jnp.float32) * IDX_W_SCALE)
        score = jnp.where(causal[None], score, -jnp.inf)
        _, idx = lax.top_k(score, k_sel)
        valid = idx <= qpos[None, :, None]
        kv_sel = jax.vmap(lambda kv, i: kv[i])(ckv, idx)
        q_lat = jnp.einsum('bqhd,lhd->bqhl', q_b, w_uk)
        logits = jnp.einsum('bqhl,bqkl->bqhk', q_lat, kv_sel, preferred_element_type=jnp.float32) * ATTN_SCALE
        logits = jnp.where(valid[:, :, None, :], logits, -jnp.inf)
        p = jax.nn.softmax(logits, axis=-1).astype(kv_sel.dtype)
        o_lat = jnp.einsum('bqhk,bqkl->bqhl', p, kv_sel)
        return jnp.einsum('bqhl,lhd->bqhd', o_lat, w_uv)

    starts = jnp.arange(nb, dtype=jnp.int32) * Q_BLOCK
    out = lax.map(block, (to_blocks(q), to_blocks(qi), to_blocks(wi), starts))
    return jnp.moveaxis(out, 0, 1).reshape(b_, L, ATTN_WIDTH)


def multiscale_pool(a, w_pool, pool_scale):
    b_, L, _ = a.shape
    af = a.astype(jnp.float32)
    cs = jnp.cumsum(af, axis=1)
    pos = jnp.arange(1, L + 1, dtype=jnp.float32)
    outs = []
    for g, w in enumerate(POOL_WINDOWS):
        sl = slice(g * POOL_GROUP_DIM, (g + 1) * POOL_GROUP_DIM)
        cs_g = cs[..., sl]
        prev = jnp.pad(cs_g, ((0, 0), (w, 0), (0, 0)))[:, :L]
        mean = (cs_g - prev) / jnp.minimum(pos, float(w))[:, None]
        outs.append(mean - af[..., sl])
    d = jnp.stack(outs, axis=2).astype(a.dtype)
    z = jnp.einsum('blgc,gcd->blgd', d, w_pool).reshape(b_, L, POOL_WIDTH)
    return z * pool_scale


def moe(h, w_router, b_router, w_gu, b_gu, w_down, b_down):
    b_, L, D = h.shape
    hf = h.reshape(b_ * L, D)
    n_tok = b_ * L
    n_asg = n_tok * TOP_K
    logits = (hf @ w_router + b_router).astype(jnp.float32)
    top_val, top_idx = lax.top_k(logits, TOP_K)
    wts = jax.nn.softmax(top_val, axis=-1)
    flat_e = top_idx.reshape(-1)
    order = jnp.argsort(flat_e)
    e_sorted = flat_e[order]
    tok = order // TOP_K
    w_sorted = wts.reshape(-1)[order]
    sizes = jnp.bincount(flat_e, length=N_EXPERTS).astype(jnp.int32)
    padded = ((sizes + MOE_BLOCK - 1) // MOE_BLOCK) * MOE_BLOCK
    start_unpad = jnp.cumsum(sizes) - sizes
    end_pad = jnp.cumsum(padded)
    start_pad = end_pad - padded
    rank = jnp.arange(n_asg, dtype=jnp.int32) - start_unpad[e_sorted]
    dest = start_pad[e_sorted] + rank
    n_blocks = -(-n_asg // MOE_BLOCK) + N_EXPERTS
    n_rows = n_blocks * MOE_BLOCK
    xs = jnp.zeros((n_rows, D), h.dtype).at[dest].set(hf[tok])
    blk_start = jnp.arange(n_blocks, dtype=jnp.int32) * MOE_BLOCK
    blk_e = jnp.minimum(jnp.searchsorted(end_pad, blk_start, side='right'), N_EXPERTS - 1)

    def expert_block(args):
        xb, e = args
        gu = xb @ w_gu[e] + b_gu[e]
        gt, up = gu[:, :D_FF], gu[:, D_FF:]
        gt = jnp.minimum(gt, SWIGLU_LIMIT)
        up = jnp.clip(up, -SWIGLU_LIMIT, SWIGLU_LIMIT)
        act = gt * jax.nn.sigmoid(SWIGLU_ALPHA * gt) * (up + 1.0)
        return act @ w_down[e] + b_down[e]

    ys = lax.map(expert_block, (xs.reshape(n_blocks, MOE_BLOCK, D), blk_e)).reshape(n_rows, D)
    y = ys[dest] * w_sorted[:, None].astype(ys.dtype)
    return jax.ops.segment_sum(y, tok, num_segments=n_tok).reshape(b_, L, D)


def setup_inputs(seed: int = 0) -> dict:
    key = jax.random.key(seed)
    ks = jax.random.split(key, 24)
    f32 = jnp.float32
    nrm = lambda k, shp, s: jax.random.normal(k, shp, f32) * s
    D = D_MODEL
    return {
        'x': nrm(ks[0], (BATCH, SEQ, D), 1.0),
        'c': nrm(ks[1], (BATCH, D), 1.0),
        'norm1_g': 1.0 + nrm(ks[2], (DEPTH, D), 0.05),
        'norm2_g': 1.0 + nrm(ks[3], (DEPTH, D), 0.05),
        'w_ada': nrm(ks[4], (DEPTH, D, 6 * D), 0.5 * D ** -0.5),
        'b_ada': nrm(ks[5], (DEPTH, 6 * D), 0.02),
        'w_in': nrm(ks[6], (DEPTH, D, IN_WIDTH), D ** -0.5),
        'kv_norm_g': 1.0 + nrm(ks[7], (DEPTH, KV_LATENT), 0.05),
        'w_uk': nrm(ks[8], (DEPTH, KV_LATENT, N_HEADS, HEAD_DIM), HEAD_DIM ** -0.5),
        'w_uv': nrm(ks[9], (DEPTH, KV_LATENT, N_HEADS, HEAD_DIM), KV_LATENT ** -0.5),
        'w_pool': nrm(ks[10], (DEPTH, POOL_GROUPS, POOL_GROUP_DIM, POOL_GROUP_DIM), POOL_GROUP_DIM ** -0.5),
        'pool_scale': 1.0 + nrm(ks[11], (DEPTH, POOL_WIDTH), 0.1),
        'w_a_up': nrm(ks[12], (DEPTH, ATTN_WIDTH, D), ATTN_WIDTH ** -0.5),
        'w_b_up': nrm(ks[13], (DEPTH, POOL_WIDTH, D), POOL_WIDTH ** -0.5),
        'w_out': nrm(ks[14], (DEPTH, D, D), D ** -0.5),
        'w_router': nrm(ks[15], (DEPTH, D, N_EXPERTS), D ** -0.5),
        'b_router': nrm(ks[16], (DEPTH, N_EXPERTS), 0.01),
        'w_gu': nrm(ks[17], (DEPTH, N_EXPERTS, D, 2 * D_FF), D ** -0.5),
        'b_gu': nrm(ks[18], (DEPTH, N_EXPERTS, 2 * D_FF), 0.02),
        'w_down': nrm(ks[19], (DEPTH, N_EXPERTS, D_FF, D), D_FF ** -0.5),
        'b_down': nrm(ks[20], (DEPTH, N_EXPERTS, D), 0.02),
        'final_g': 1.0 + nrm(ks[21], (D,), 0.05),
    }


def reference(x, c, norm1_g, norm2_g, w_ada, b_ada, w_in, kv_norm_g, w_uk, w_uv, w_pool, pool_scale,
              w_a_up, w_b_up, w_out, w_router, b_router, w_gu, b_gu, w_down, b_down, final_g):
    b_, L, _ = x.shape
    cond = jax.nn.silu(c)
    for l in range(DEPTH):
        ada = (cond @ w_ada[l] + b_ada[l])[:, None, :]
        sh1, sc1, g1, sh2, sc2, g2 = jnp.split(ada, 6, axis=-1)
        h = rms_norm(x, norm1_g[l]) * (1.0 + sc1) + sh1
        p = h @ w_in[l]
        q, ckv, qi, ki, wi, a_pool, gate_a, gate_b = jnp.split(p, SPLIT_OFFSETS, axis=-1)
        q = q.reshape(b_, L, N_HEADS, HEAD_DIM)
        qi = qi.reshape(b_, L, IDX_HEADS, IDX_DIM)
        ckv = rms_norm(ckv, kv_norm_g[l])
        o_a = dsa_attention(q, ckv, qi, ki, wi, w_uk[l], w_uv[l])
        o_b = multiscale_pool(a_pool, w_pool[l], pool_scale[l])
        merged = jax.nn.sigmoid(gate_a) * (o_a @ w_a_up[l]) + jax.nn.sigmoid(gate_b) * (o_b @ w_b_up[l])
        x = x + g1 * (merged @ w_out[l])
        h2 = rms_norm(x, norm2_g[l]) * (1.0 + sc2) + sh2
        x = x + g2 * moe(h2, w_router[l], b_router[l], w_gu[l], b_gu[l], w_down[l], b_down[l])
    return rms_norm(x, final_g)
```

```python
import functools

import jax
import jax.numpy as jnp
from jax import lax
from jax.experimental import pallas as pl
from jax.experimental.pallas import tpu as pltpu

N_HEADS = 8
HEAD_DIM = 64
ATTN_WIDTH = N_HEADS * HEAD_DIM
KV_LATENT = 128
IDX_HEADS = 4
IDX_DIM = 32
TOPK_MAX = 256
ATTN_SCALE = HEAD_DIM ** -0.5
IDX_W_SCALE = (IDX_HEADS ** -0.5) * (IDX_DIM ** -0.5)
POOL_WINDOWS = (2, 4, 8, 16)
POOL_GROUP_DIM = 128
POOL_WIDTH = len(POOL_WINDOWS) * POOL_GROUP_DIM
N_EXPERTS = 32
TOP_K = 4
MOE_BLOCK = 512
SWIGLU_LIMIT = 7.0
SWIGLU_ALPHA = 1.702
RMS_EPS = 1e-6

LANES = 128
POOL_HALO = 16
NEG_BIG = -1e30
INT_MIN = -2 ** 31

TM_PROJ = 512
TQ = 128
TK = 512
TM_MIX = 256
TM_ROUTE = 512
TM_SCATTER = 256
TM_COMBINE = 256
VMEM_LIMIT = 56 * 1024 * 1024


def _cparams(sem):
    return pltpu.CompilerParams(dimension_semantics=sem, vmem_limit_bytes=VMEM_LIMIT)


def _rms(xf, g):
    return xf * lax.rsqrt(jnp.mean(xf * xf, axis=-1, keepdims=True) + RMS_EPS) * g


def _bf(v):
    return v.astype(jnp.bfloat16)


def _dot(a, b):
    return jnp.dot(a, b, preferred_element_type=jnp.float32)


def _dot_nt(a, b):
    return lax.dot_general(a, b, (((1,), (1,)), ((), ())), preferred_element_type=jnp.float32)


def _ada_kernel(c_ref, w_ref, b_ref, o_ref):
    cf = c_ref[...]
    cond = cf * jax.nn.sigmoid(cf)
    o_ref[0] = _dot(_bf(cond), _bf(w_ref[0])) + b_ref[0]


def _ada_call(c, w_ada, b_ada):
    depth, d, n6 = w_ada.shape
    bsz = c.shape[0]
    tn = 1024
    return pl.pallas_call(
        _ada_kernel,
        grid=(depth, n6 // tn),
        in_specs=[
            pl.BlockSpec((bsz, d), lambda l, j: (0, 0)),
            pl.BlockSpec((1, d, tn), lambda l, j: (l, 0, j)),
            pl.BlockSpec((1, 1, tn), lambda l, j: (l, 0, j)),
        ],
        out_specs=pl.BlockSpec((1, bsz, tn), lambda l, j: (l, 0, j)),
        out_shape=jax.ShapeDtypeStruct((depth, bsz, n6), jnp.float32),
        compiler_params=_cparams(("arbitrary", "arbitrary")),
        name="ada",
    )(c, w_ada, b_ada.reshape(depth, 1, n6))


PROJ_COLS = ATTN_WIDTH + 4 * LANES + POOL_WIDTH


def _proj_kernel(x_ref, sh_ref, sc_ref, g_ref, w_ref, kvg_ref,
                 q_ref, ckv_ref, qi_ref, ki_ref, wi_ref, ap_ref):
    h = _rms(x_ref[...], g_ref[...]) * (1.0 + sc_ref[0]) + sh_ref[0]
    p = _dot(_bf(h), w_ref[...])
    o = 0
    q_ref[...] = _bf(p[:, o:o + ATTN_WIDTH]); o += ATTN_WIDTH
    ckv_ref[...] = _bf(_rms(p[:, o:o + KV_LATENT], kvg_ref[...])); o += KV_LATENT
    qi_ref[...] = _bf(p[:, o:o + LANES]); o += LANES
    ki_ref[...] = _bf(p[:, o:o + LANES]); o += LANES
    wi_ref[...] = p[:, o:o + LANES]; o += LANES
    ap_ref[...] = p[:, o:o + POOL_WIDTH]


def _proj_call(xf, sh1, sc1, g1n, w_r, kvg, seq):
    n, d = xf.shape
    tpb = seq // TM_PROJ
    row = lambda i: (i, 0)
    per_b = lambda i: (i // tpb, 0, 0)
    fixed = lambda i: (0, 0)
    outs = [(ATTN_WIDTH, jnp.bfloat16), (KV_LATENT, jnp.bfloat16), (LANES, jnp.bfloat16),
            (LANES, jnp.bfloat16), (LANES, jnp.float32), (POOL_WIDTH, jnp.float32)]
    return pl.pallas_call(
        _proj_kernel,
        grid=(n // TM_PROJ,),
        in_specs=[
            pl.BlockSpec((TM_PROJ, d), row),
            pl.BlockSpec((1, 1, d), per_b),
            pl.BlockSpec((1, 1, d), per_b),
            pl.BlockSpec((1, d), fixed),
            pl.BlockSpec((d, PROJ_COLS), fixed),
            pl.BlockSpec((1, KV_LATENT), fixed),
        ],
        out_specs=[pl.BlockSpec((TM_PROJ, w), row) for w, _ in outs],
        out_shape=[jax.ShapeDtypeStruct((n, w), dt) for w, dt in outs],
        compiler_params=_cparams(("arbitrary",)),
        name="in_proj",
    )(xf, sh1, sc1, g1n, w_r, kvg)


def _attn_kernel(q_ref, qi_ref, wi_ref, ckv_ref, ki_ref, wuk_ref, wuv_ref, o_ref,
                 key_buf, m_buf, l_buf, acc_buf, *, k_sel):
    qb = pl.program_id(1)
    q_start = qb * TQ
    n_kc = (q_start + TQ + TK - 1) // TK

    lane = lax.broadcasted_iota(jnp.int32, (TQ, LANES), 1)
    qpos = q_start + lax.broadcasted_iota(jnp.int32, (TQ, TK), 0)
    kcol = lax.broadcasted_iota(jnp.int32, (TQ, TK), 1)

    qi = qi_ref[...]
    zero_bf = jnp.zeros_like(qi)
    q4 = jnp.concatenate(
        [jnp.where((lane >= h * IDX_DIM) & (lane < (h + 1) * IDX_DIM), qi, zero_bf)
         for h in range(IDX_HEADS)], axis=0)
    wi = wi_ref[...] * IDX_W_SCALE
    w_cols = [wi[:, h:h + 1] for h in range(IDX_HEADS)]

    def score_chunk(kc, carry):
        k0 = pl.multiple_of(kc * TK, TK)
        raw = _dot_nt(q4, ki_ref[pl.ds(k0, TK), :])
        score = w_cols[0] * jnp.maximum(raw[0:TQ], 0.0)
        for h in range(1, IDX_HEADS):
            score = score + w_cols[h] * jnp.maximum(raw[h * TQ:(h + 1) * TQ], 0.0)
        score = score + 0.0
        bits = pltpu.bitcast(score, jnp.int32)
        key = jnp.where(bits < 0, bits ^ jnp.int32(0x7FFFFFFF), bits)
        key = jnp.where(kcol + k0 <= qpos, key, jnp.int32(INT_MIN))
        key_buf[kc] = key
        return carry

    lax.fori_loop(0, n_kc, score_chunk, 0)

    def count_ge(cand, strict):
        cand_b = jnp.broadcast_to(cand, (TQ, LANES))

        def body(kc, acc):
            key = key_buf[kc]
            for j in range(TK // LANES):
                kj = key[:, j * LANES:(j + 1) * LANES]
                hit = (kj > cand_b) if strict else (kj >= cand_b)
                acc = acc + jnp.where(hit, 1.0, 0.0)
            return acc

        acc = lax.fori_loop(0, n_kc, body, jnp.zeros((TQ, LANES), jnp.float32))
        return jnp.sum(acc, axis=1, keepdims=True)

    def bit_step(i, t_u):
        bit = 31 - i
        cand_u = t_u | lax.shift_left(jnp.int32(1), bit)
        cnt = count_ge(cand_u ^ jnp.int32(INT_MIN), False)
        return jnp.where(cnt >= float(k_sel), cand_u, t_u)

    t_u = lax.fori_loop(0, 32, bit_step, jnp.zeros((TQ, 1), jnp.int32))
    thr = t_u ^ jnp.int32(INT_MIN)
    n_tie_take = float(k_sel) - count_ge(thr, True)

    qlat = _dot(q_ref[...], wuk_ref[...]) * ATTN_SCALE
    qs = _bf(jnp.concatenate(
        [qlat[:, h * KV_LATENT:(h + 1) * KV_LATENT] for h in range(N_HEADS)], axis=0))
    upper = _bf(jnp.where(lax.broadcasted_iota(jnp.int32, (TK, TK), 0)
                          < lax.broadcasted_iota(jnp.int32, (TK, TK), 1), 1.0, 0.0))
    m_buf[...] = jnp.full(m_buf.shape, -3e38, jnp.float32)
    l_buf[...] = jnp.zeros(l_buf.shape, jnp.float32)
    acc_buf[...] = jnp.zeros(acc_buf.shape, jnp.float32)

    def attend_chunk(kc, tie_seen):
        k0 = pl.multiple_of(kc * TK, TK)
        key = key_buf[kc]
        kv = ckv_ref[pl.ds(k0, TK), :]
        eq = key == thr
        eq_f = jnp.where(eq, 1.0, 0.0)
        tie_rank = _dot(_bf(eq_f), upper) + tie_seen
        sel = ((key > thr) | (eq & (tie_rank < n_tie_take))) & (kcol + k0 <= qpos)
        bias = jnp.where(sel, 0.0, NEG_BIG)
        logits = _dot_nt(qs, kv)
        for h in range(N_HEADS):
            lg = logits[h * TQ:(h + 1) * TQ] + bias
            m_old = m_buf[h]
            m_new = jnp.maximum(m_old, jnp.max(lg, axis=1, keepdims=True))
            alpha = jnp.exp(m_old - m_new)
            p = jnp.exp(lg - m_new)
            l_buf[h] = alpha * l_buf[h] + jnp.sum(p, axis=1, keepdims=True)
            acc_buf[h] = alpha * acc_buf[h] + _dot(_bf(p), kv)
            m_buf[h] = m_new
        return tie_seen + jnp.sum(eq_f, axis=1, keepdims=True)

    lax.fori_loop(0, n_kc, attend_chunk, jnp.zeros((TQ, 1), jnp.float32))

    o_lat = jnp.concatenate([acc_buf[h] / l_buf[h] for h in range(N_HEADS)], axis=1)
    o_ref[...] = _bf(_dot(_bf(o_lat), wuv_ref[...]))


def _attn_call(q, qi, wi, ckv, ki4, wuk_bd, wuv_bd, bsz, seq):
    n = q.shape[0]
    nq = seq // TQ
    k_sel = min(TOPK_MAX, seq // 4)
    qrow = lambda b, i: (b * nq + i, 0)
    per_b = lambda b, i: (b, 0)
    fixed = lambda b, i: (0, 0)
    return pl.pallas_call(
        functools.partial(_attn_kernel, k_sel=k_sel),
        grid=(bsz, nq),
        in_specs=[
            pl.BlockSpec((TQ, ATTN_WIDTH), qrow),
            pl.BlockSpec((TQ, LANES), qrow),
            pl.BlockSpec((TQ, LANES), qrow),
            pl.BlockSpec((seq, KV_LATENT), per_b),
            pl.BlockSpec((seq, LANES), per_b),
            pl.BlockSpec((ATTN_WIDTH, N_HEADS * KV_LATENT), fixed),
            pl.BlockSpec((N_HEADS * KV_LATENT, ATTN_WIDTH), fixed),
        ],
        out_specs=pl.BlockSpec((TQ, ATTN_WIDTH), qrow),
        out_shape=jax.ShapeDtypeStruct((n, ATTN_WIDTH), jnp.bfloat16),
        scratch_shapes=[
            pltpu.VMEM((seq // TK, TQ, TK), jnp.int32),
            pltpu.VMEM((N_HEADS, TQ, 1), jnp.float32),
            pltpu.VMEM((N_HEADS, TQ, 1), jnp.float32),
            pltpu.VMEM((N_HEADS, TQ, KV_LATENT), jnp.float32),
        ],
        compiler_params=_cparams(("arbitrary", "arbitrary")),
        name="dsa_attention",
    )(q, qi, wi, ckv, ki4, wuk_bd, wuv_bd)


def _mix_kernel(x_ref, oa_ref, ap_ref, halo_ref, sh1_ref, sc1_ref, g1_ref, n1g_ref,
                wgate_ref, wpool_ref, pscale_ref, waup_ref, wbup_ref, wout_ref,
                sh2_ref, sc2_ref, n2g_ref, wr_ref, br_ref,
                x1_ref, h2_ref, lg_ref, *, tiles_per_batch):
    i = pl.program_id(0)
    t_in_b = i % tiles_per_batch
    x = x_ref[...]
    d = x.shape[1]
    h = _rms(x, n1g_ref[...]) * (1.0 + sc1_ref[0]) + sh1_ref[0]
    gates = _dot(_bf(h), wgate_ref[...])

    pos1 = (t_in_b * TM_MIX + 1 + lax.broadcasted_iota(jnp.int32, (TM_MIX, 1), 0)).astype(jnp.float32)
    halo = jnp.where(t_in_b == 0, 0.0, halo_ref[...])
    zs = []
    for g, w in enumerate(POOL_WINDOWS):
        sl = slice(g * POOL_GROUP_DIM, (g + 1) * POOL_GROUP_DIM)
        a = ap_ref[:, sl]
        ext = jnp.concatenate([halo[:, sl], a], axis=0)
        span = 1
        while span < w:
            ext = ext + pltpu.roll(ext, span, 0)
            span *= 2
        mean = ext[POOL_HALO:] / jnp.minimum(pos1, float(w))
        zs.append(_dot(_bf(mean - a), wpool_ref[g]))
    o_b = jnp.concatenate(zs, axis=1) * pscale_ref[...]

    merged = (jax.nn.sigmoid(gates[:, :d]) * _dot(oa_ref[...], waup_ref[...])
              + jax.nn.sigmoid(gates[:, d:]) * _dot(_bf(o_b), wbup_ref[...]))
    x1 = x + g1_ref[0] * _dot(_bf(merged), wout_ref[...])
    x1_ref[...] = x1
    h2 = _rms(x1, n2g_ref[...]) * (1.0 + sc2_ref[0]) + sh2_ref[0]
    h2_ref[...] = h2
    lg_ref[...] = _dot(_bf(h2), wr_ref[...]) + br_ref[...]


def _mix_call(xf, o_a, apool, sh1, sc1, g1, n1g, wgate, wpool, pscale, waup, wbup, wout,
              sh2, sc2, n2g, wr, br, seq):
    n, d = xf.shape
    tpb = seq // TM_MIX
    row = lambda i: (i, 0)
    per_b = lambda i: (i // tpb, 0, 0)
    fixed = lambda i: (0, 0)
    fixed3 = lambda i: (0, 0, 0)
    halo_idx = lambda i: (jnp.maximum(i * (TM_MIX // POOL_HALO) - 1, 0), 0)
    mod = pl.BlockSpec((1, 1, d), per_b)
    return pl.pallas_call(
        functools.partial(_mix_kernel, tiles_per_batch=tpb),
        grid=(n // TM_MIX,),
        in_specs=[
            pl.BlockSpec((TM_MIX, d), row),
            pl.BlockSpec((TM_MIX, ATTN_WIDTH), row),
            pl.BlockSpec((TM_MIX, POOL_WIDTH), row),
            pl.BlockSpec((POOL_HALO, POOL_WIDTH), halo_idx),
            mod, mod, mod,
            pl.BlockSpec((1, d), fixed),
            pl.BlockSpec((d, 2 * d), fixed),
            pl.BlockSpec((len(POOL_WINDOWS), POOL_GROUP_DIM, POOL_GROUP_DIM), fixed3),
            pl.BlockSpec((1, POOL_WIDTH), fixed),
            pl.BlockSpec((ATTN_WIDTH, d), fixed),
            pl.BlockSpec((POOL_WIDTH, d), fixed),
            pl.BlockSpec((d, d), fixed),
            mod, mod,
            pl.BlockSpec((1, d), fixed),
            pl.BlockSpec((d, LANES), fixed),
            pl.BlockSpec((1, LANES), fixed),
        ],
        out_specs=[pl.BlockSpec((TM_MIX, d), row), pl.BlockSpec((TM_MIX, d), row),
                   pl.BlockSpec((TM_MIX, LANES), row)],
        out_shape=[jax.ShapeDtypeStruct((n, d), jnp.float32), jax.ShapeDtypeStruct((n, d), jnp.float32),
                   jax.ShapeDtypeStruct((n, LANES), jnp.float32)],
        compiler_params=_cparams(("arbitrary",)),
        name="mix_merge",
    )(xf, o_a, apool, apool, sh1, sc1, g1, n1g, wgate, wpool, pscale, waup, wbup, wout,
      sh2, sc2, n2g, wr, br)


def _route_kernel(lg_ref, idx_ref, wt_ref, rank_ref, cnt_ref, carry):
    @pl.when(pl.program_id(0) == 0)
    def _():
        carry[...] = jnp.zeros(carry.shape, jnp.float32)

    work = lg_ref[...]
    lane = lax.broadcasted_iota(jnp.int32, work.shape, 1).astype(jnp.float32)
    vals, idxs = [], []
    for _ in range(TOP_K):
        m = jnp.max(work, axis=1, keepdims=True)
        idx = jnp.min(jnp.where(work == m, lane, float(LANES)), axis=1, keepdims=True)
        vals.append(m)
        idxs.append(idx)
        work = jnp.where(lane == idx, -jnp.inf, work)
    exps = [jnp.exp(v - vals[0]) for v in vals]
    denom = exps[0] + exps[1] + exps[2] + exps[3]

    onehot = jnp.zeros(work.shape, jnp.float32)
    for idx in idxs:
        onehot = onehot + jnp.where(lane == idx, 1.0, 0.0)
    tm = work.shape[0]
    lower = _bf(jnp.where(lax.broadcasted_iota(jnp.int32, (tm, tm), 1)
                          < lax.broadcasted_iota(jnp.int32, (tm, tm), 0), 1.0, 0.0))
    before = _dot(lower, _bf(onehot)) + carry[...]

    idx_out = jnp.zeros(work.shape, jnp.int32)
    wt_out = jnp.zeros(work.shape, jnp.float32)
    rank_out = jnp.zeros(work.shape, jnp.int32)
    for k in range(TOP_K):
        rank_k = jnp.sum(jnp.where(lane == idxs[k], before, 0.0), axis=1, keepdims=True)
        idx_out = jnp.where(lane == k, idxs[k].astype(jnp.int32), idx_out)
        wt_out = jnp.where(lane == k, exps[k] / denom, wt_out)
        rank_out = jnp.where(lane == k, rank_k.astype(jnp.int32), rank_out)
    idx_ref[...] = idx_out
    wt_ref[...] = wt_out
    rank_ref[...] = rank_out
    carry[...] = carry[...] + jnp.sum(onehot, axis=0, keepdims=True)
    cnt_ref[...] = carry[...]


def _route_call(logits):
    n = logits.shape[0]
    row = lambda i: (i, 0)
    return pl.pallas_call(
        _route_kernel,
        grid=(n // TM_ROUTE,),
        in_specs=[pl.BlockSpec((TM_ROUTE, LANES), row)],
        out_specs=[pl.BlockSpec((TM_ROUTE, LANES), row)] * 3 + [pl.BlockSpec((1, LANES), lambda i: (0, 0))],
        out_shape=[jax.ShapeDtypeStruct((n, LANES), jnp.int32), jax.ShapeDtypeStruct((n, LANES), jnp.float32),
                   jax.ShapeDtypeStruct((n, LANES), jnp.int32), jax.ShapeDtypeStruct((1, LANES), jnp.float32)],
        scratch_shapes=[pltpu.VMEM((1, LANES), jnp.float32)],
        compiler_params=_cparams(("arbitrary",)),
        name="route",
    )(logits)


def _scatter_kernel(endpad_ref, dest_ref, h2_ref, xs_ref, zero_buf, sem_zero, sem_rows):
    @pl.when(pl.program_id(0) == 0)
    def _():
        zero_buf[...] = jnp.zeros(zero_buf.shape, zero_buf.dtype)

        def zero_block(start):
            return pltpu.make_async_copy(
                zero_buf, xs_ref.at[pl.ds(pl.multiple_of(start, MOE_BLOCK), MOE_BLOCK), :], sem_zero)

        def last_block_of(e):
            return jnp.maximum(endpad_ref[e] - MOE_BLOCK, 0)

        def start_zero(e, c):
            zero_block(last_block_of(e)).start()
            return c

        def wait_zero(e, c):
            zero_block(last_block_of(e)).wait()
            return c

        def start_tail(b, c):
            zero_block(b * MOE_BLOCK).start()
            return c

        def wait_tail(b, c):
            zero_block(b * MOE_BLOCK).wait()
            return c

        n_used = endpad_ref[N_EXPERTS - 1] // MOE_BLOCK
        n_blocks = xs_ref.shape[0] // MOE_BLOCK
        lax.fori_loop(0, N_EXPERTS, start_zero, 0)
        lax.fori_loop(n_used, n_blocks, start_tail, 0)
        lax.fori_loop(0, N_EXPERTS, wait_zero, 0)
        lax.fori_loop(n_used, n_blocks, wait_tail, 0)

    def row_copy(r, k):
        return pltpu.make_async_copy(h2_ref.at[pl.ds(r, 1), :],
                                     xs_ref.at[pl.ds(dest_ref[r * TOP_K + k], 1), :], sem_rows)

    def start_row(r, c):
        for k in range(TOP_K):
            row_copy(r, k).start()
        return c

    def wait_row(r, c):
        for k in range(TOP_K):
            row_copy(r, k).wait()
        return c

    lax.fori_loop(0, TM_SCATTER, start_row, 0)
    lax.fori_loop(0, TM_SCATTER, wait_row, 0)


def _scatter_call(h2, dest_flat, end_pad, n_rows):
    n, d = h2.shape
    grid_spec = pltpu.PrefetchScalarGridSpec(
        num_scalar_prefetch=1,
        grid=(n // TM_SCATTER,),
        in_specs=[
            pl.BlockSpec((TM_SCATTER * TOP_K,), lambda i, ep: (i,), memory_space=pltpu.SMEM),
            pl.BlockSpec((TM_SCATTER, d), lambda i, ep: (i, 0)),
        ],
        out_specs=pl.BlockSpec(memory_space=pl.ANY),
        scratch_shapes=[pltpu.VMEM((MOE_BLOCK, d), h2.dtype),
                        pltpu.SemaphoreType.DMA(()), pltpu.SemaphoreType.DMA(())],
    )
    return pl.pallas_call(
        _scatter_kernel,
        grid_spec=grid_spec,
        out_shape=jax.ShapeDtypeStruct((n_rows, d), h2.dtype),
        compiler_params=_cparams(("arbitrary",)),
        name="moe_scatter",
    )(end_pad, dest_flat, h2)


def _expert_kernel(blk_e_ref, blk_src_ref, blk_on_ref, xs_ref, wgu_ref, bgu_ref, wdn_ref, bdn_ref, ys_ref):
    on = blk_on_ref[pl.program_id(0)] == 1

    @pl.when(jnp.logical_not(on))
    def _():
        ys_ref[...] = jnp.zeros(ys_ref.shape, ys_ref.dtype)

    @pl.when(on)
    def _():
        f = wdn_ref.shape[1]
        gu = _dot(_bf(xs_ref[...]), wgu_ref[0]) + bgu_ref[0]
        gt = jnp.minimum(gu[:, :f], SWIGLU_LIMIT)
        up = jnp.clip(gu[:, f:], -SWIGLU_LIMIT, SWIGLU_LIMIT)
        act = gt * jax.nn.sigmoid(SWIGLU_ALPHA * gt) * (up + 1.0)
        ys_ref[...] = _dot(_bf(act), wdn_ref[0]) + bdn_ref[0]


def _expert_call(xs, blk_e, blk_src, blk_on, wgu, bgu, wdn, bdn):
    n_rows, d = xs.shape
    n_blocks = n_rows // MOE_BLOCK
    f2 = wgu.shape[2]
    f = wdn.shape[1]
    src = lambda i, be, bs, bo: (bs[i], 0)
    by_e = lambda i, be, bs, bo: (be[i], 0, 0)
    grid_spec = pltpu.PrefetchScalarGridSpec(
        num_scalar_prefetch=3,
        grid=(n_blocks,),
        in_specs=[
            pl.BlockSpec((MOE_BLOCK, d), src),
            pl.BlockSpec((1, d, f2), by_e),
            pl.BlockSpec((1, 1, f2), by_e),
            pl.BlockSpec((1, f, d), by_e),
            pl.BlockSpec((1, 1, d), by_e),
        ],
        out_specs=pl.BlockSpec((MOE_BLOCK, d), lambda i, be, bs, bo: (i, 0)),
    )
    return pl.pallas_call(
        _expert_kernel,
        grid_spec=grid_spec,
        out_shape=jax.ShapeDtypeStruct((n_rows, d), jnp.float32),
        compiler_params=_cparams(("arbitrary",)),
        name="moe_experts",
    )(blk_e, blk_src, blk_on, xs, wgu, bgu, wdn, bdn)


def _combine_kernel(dest_ref, x1_ref, wt_ref, g2_ref, fg_ref, ys_ref, o_ref, buf, sem, *, final_norm):
    def row_copy(r, k):
        return pltpu.make_async_copy(ys_ref.at[pl.ds(dest_ref[r * TOP_K + k], 1), :],
                                     buf.at[k, pl.ds(r, 1), :], sem)

    def start_row(r, c):
        for k in range(TOP_K):
            row_copy(r, k).start()
        return c

    def wait_row(r, c):
        for k in range(TOP_K):
            row_copy(r, k).wait()
        return c

    lax.fori_loop(0, TM_COMBINE, start_row, 0)
    lax.fori_loop(0, TM_COMBINE, wait_row, 0)

    wt = wt_ref[...]
    y = wt[:, 0:1] * buf[0]
    for k in range(1, TOP_K):
        y = y + wt[:, k:k + 1] * buf[k]
    out = x1_ref[...] + g2_ref[0] * y
    if final_norm:
        out = _rms(out, fg_ref[...])
    o_ref[...] = out


def _combine_call(x1, wts, g2, final_g, ys, dest_flat, seq, final_norm):
    n, d = x1.shape
    tpb = seq // TM_COMBINE
    row = lambda i: (i, 0)
    return pl.pallas_call(
        functools.partial(_combine_kernel, final_norm=final_norm),
        grid=(n // TM_COMBINE,),
        in_specs=[
            pl.BlockSpec((TM_COMBINE * TOP_K,), lambda i: (i,), memory_space=pltpu.SMEM),
            pl.BlockSpec((TM_COMBINE, d), row),
            pl.BlockSpec((TM_COMBINE, LANES), row),
            pl.BlockSpec((1, 1, d), lambda i: (i // tpb, 0, 0)),
            pl.BlockSpec((1, d), lambda i: (0, 0)),
            pl.BlockSpec(memory_space=pl.ANY),
        ],
        out_specs=pl.BlockSpec((TM_COMBINE, d), row),
        out_shape=jax.ShapeDtypeStruct((n, d), jnp.float32),
        scratch_shapes=[pltpu.VMEM((TOP_K, TM_COMBINE, d), jnp.float32), pltpu.SemaphoreType.DMA(())],
        compiler_params=_cparams(("arbitrary",)),
        name="moe_combine",
    )(dest_flat, x1, wts, g2, final_g, ys)


def _regroup_w_in(w_in):
    o_q, o_kv = 0, ATTN_WIDTH
    o_qi = o_kv + KV_LATENT
    o_ki = o_qi + IDX_HEADS * IDX_DIM
    o_wi = o_ki + IDX_DIM
    o_pool = o_wi + IDX_HEADS
    o_ga = o_pool + POOL_WIDTH
    ki = w_in[..., o_ki:o_wi]
    wi = w_in[..., o_wi:o_pool]
    wi_pad = jnp.pad(wi, ((0, 0), (0, 0), (0, LANES - IDX_HEADS)))
    w_r = jnp.concatenate([w_in[..., o_q:o_ki], ki, ki, ki, ki, wi_pad, w_in[..., o_pool:o_ga]], axis=-1)
    return _bf(w_r), _bf(w_in[..., o_ga:])


def _block_diag_uk(w_uk):
    depth = w_uk.shape[0]
    eye = jnp.eye(N_HEADS, dtype=w_uk.dtype)
    t = jnp.einsum('zlhd,hg->zhdgl', w_uk, eye)
    return _bf(t.reshape(depth, N_HEADS * HEAD_DIM, N_HEADS * KV_LATENT))


def _block_diag_uv(w_uv):
    depth = w_uv.shape[0]
    eye = jnp.eye(N_HEADS, dtype=w_uv.dtype)
    t = jnp.einsum('zlhd,hg->zhlgd', w_uv, eye)
    return _bf(t.reshape(depth, N_HEADS * KV_LATENT, N_HEADS * HEAD_DIM))


def kernel(x, c, norm1_g, norm2_g, w_ada, b_ada, w_in, kv_norm_g, w_uk, w_uv, w_pool, pool_scale, w_a_up, w_b_up, w_out, w_router, b_router, w_gu, b_gu, w_down, b_down, final_g):
    bsz, seq, d = x.shape
    depth = w_in.shape[0]
    n = bsz * seq
    assert seq % TK == 0 and seq % TM_PROJ == 0 and n % MOE_BLOCK == 0

    ada = _ada_call(c, w_ada, b_ada).reshape(depth, bsz, 6, 1, d)
    w_r, w_gate = _regroup_w_in(w_in)
    wuk_bd = _block_diag_uk(w_uk)
    wuv_bd = _block_diag_uv(w_uv)
    w_pool_b, w_a_up_b, w_b_up_b, w_out_b = _bf(w_pool), _bf(w_a_up), _bf(w_b_up), _bf(w_out)
    w_router_p = _bf(jnp.pad(w_router, ((0, 0), (0, 0), (0, LANES - N_EXPERTS))))
    b_router_p = jnp.pad(b_router, ((0, 0), (0, LANES - N_EXPERTS)), constant_values=NEG_BIG)
    w_gu_b, w_down_b = _bf(w_gu), _bf(w_down)

    n_asg = n * TOP_K
    n_blocks = -(-n_asg // MOE_BLOCK) + N_EXPERTS
    n_rows = n_blocks * MOE_BLOCK

    xf = x.reshape(n, d)
    for l in range(depth):
        sh1, sc1, g1, sh2, sc2, g2 = [ada[l, :, j] for j in range(6)]
        q, ckv, qi, ki4, wi, apool = _proj_call(xf, sh1, sc1, norm1_g[l][None], w_r[l], kv_norm_g[l][None], seq)
        o_a = _attn_call(q, qi, wi, ckv, ki4, wuk_bd[l], wuv_bd[l], bsz, seq)
        x1, h2, logits = _mix_call(xf, o_a, apool, sh1, sc1, g1, norm1_g[l][None], w_gate[l], w_pool_b[l],
                                   pool_scale[l][None], w_a_up_b[l], w_b_up_b[l], w_out_b[l],
                                   sh2, sc2, norm2_g[l][None], w_router_p[l], b_router_p[l][None], seq)
        idx_l, wts, rank_l, counts = _route_call(logits)

        sizes = counts[0, :N_EXPERTS].astype(jnp.int32)
        padded = ((sizes + MOE_BLOCK - 1) // MOE_BLOCK) * MOE_BLOCK
        end_pad = jnp.cumsum(padded)
        start_pad = end_pad - padded
        dest = start_pad[idx_l[:, :TOP_K]] + rank_l[:, :TOP_K]
        dest_flat = dest.reshape(-1).astype(jnp.int32)
        blk_start = jnp.arange(n_blocks, dtype=jnp.int32) * MOE_BLOCK
        blk_e = jnp.minimum(jnp.searchsorted(end_pad, blk_start, side='right'), N_EXPERTS - 1).astype(jnp.int32)
        n_used = end_pad[-1] // MOE_BLOCK
        blk_idx = jnp.arange(n_blocks, dtype=jnp.int32)
        blk_on = (blk_idx < n_used).astype(jnp.int32)
        blk_src = jnp.minimum(blk_idx, n_used - 1).astype(jnp.int32)
        blk_e = jnp.where(blk_on == 1, blk_e, blk_e[n_used - 1])

        xs = _scatter_call(h2, dest_flat, end_pad.astype(jnp.int32), n_rows)
        ys = _expert_call(xs, blk_e, blk_src, blk_on, w_gu_b[l], b_gu[l][:, None, :], w_down_b[l], b_down[l][:, None, :])
        xf = _combine_call(x1, wts, g2, final_g[None], ys, dest_flat, seq, final_norm=(l == depth - 1))
    return xf.reshape(bsz, seq, d)
```

```python
import functools

import jax
import jax.numpy as jnp
from jax import lax
from jax.experimental import pallas as pl
from jax.experimental.pallas import tpu as pltpu

N_HEADS = 8
HEAD_DIM = 64
ATTN_WIDTH = N_HEADS * HEAD_DIM
KV_LATENT = 128
IDX_HEADS = 4
IDX_DIM = 32
TOPK_MAX = 256
ATTN_SCALE = HEAD_DIM ** -0.5
IDX_W_SCALE = (IDX_HEADS ** -0.5) * (IDX_DIM ** -0.5)
POOL_WINDOWS = (2, 4, 8, 16)
POOL_GROUP_DIM = 128
POOL_WIDTH = len(POOL_WINDOWS) * POOL_GROUP_DIM
N_EXPERTS = 32
TOP_K = 4
MOE_BLOCK = 512
SWIGLU_LIMIT = 7.0
SWIGLU_ALPHA = 1.702
RMS_EPS = 1e-6

LANES = 128
SUBLANES = 8
POOL_HALO = 16
NEG_BIG = -1e30
INT_MIN = -2 ** 31

TM_PROJ = 512
TQ = 128
TK = 512
CNT_ROWS = 64
TM_MIX = 256
TM_ROUTE = 512
TM_SCATTER = 256
TM_COMBINE = 256
VMEM_LIMIT = 56 * 1024 * 1024


def _cparams(sem):
    return pltpu.CompilerParams(dimension_semantics=sem, vmem_limit_bytes=VMEM_LIMIT)


def _rms(xf, g):
    return xf * lax.rsqrt(jnp.mean(xf * xf, axis=-1, keepdims=True) + RMS_EPS) * g


def _bf(v):
    return v.astype(jnp.bfloat16)


def _dot(a, b):
    return jnp.dot(a, b, preferred_element_type=jnp.float32)


def _dot_nt(a, b):
    return lax.dot_general(a, b, (((1,), (1,)), ((), ())), preferred_element_type=jnp.float32)


def _strict_lower(n):
    return _bf(jnp.where(lax.broadcasted_iota(jnp.int32, (n, n), 1)
                         < lax.broadcasted_iota(jnp.int32, (n, n), 0), 1.0, 0.0))


def _ada_kernel(c_ref, w_ref, b_ref, o_ref):
    cf = c_ref[...]
    cond = cf * jax.nn.sigmoid(cf)
    o_ref[0] = _dot(_bf(cond), _bf(w_ref[0])) + b_ref[0]


def _ada_call(c, w_ada, b_ada):
    depth, d, n6 = w_ada.shape
    bsz = c.shape[0]
    tn = 1024
    return pl.pallas_call(
        _ada_kernel,
        grid=(depth, n6 // tn),
        in_specs=[
            pl.BlockSpec((bsz, d), lambda l, j: (0, 0)),
            pl.BlockSpec((1, d, tn), lambda l, j: (l, 0, j)),
            pl.BlockSpec((1, 1, tn), lambda l, j: (l, 0, j)),
        ],
        out_specs=pl.BlockSpec((1, bsz, tn), lambda l, j: (l, 0, j)),
        out_shape=jax.ShapeDtypeStruct((depth, bsz, n6), jnp.float32),
        compiler_params=_cparams(("arbitrary", "arbitrary")),
        name="ada",
    )(c, w_ada, b_ada.reshape(depth, 1, n6))


PROJ_COLS = ATTN_WIDTH + 3 * LANES + POOL_WIDTH


def _proj_kernel(x_ref, sh_ref, sc_ref, g_ref, w_ref, wwi_ref, kvg_ref,
                 q_ref, ckv_ref, ckvt_ref, qi_ref, ki_ref, wit_ref, ap_ref):
    h = _bf(_rms(x_ref[...], g_ref[...]) * (1.0 + sc_ref[0]) + sh_ref[0])
    p = _dot(h, w_ref[...])
    o = 0
    q_ref[...] = _bf(p[:, o:o + ATTN_WIDTH]); o += ATTN_WIDTH
    ckv = _bf(_rms(p[:, o:o + KV_LATENT], kvg_ref[...])); o += KV_LATENT
    ckv_ref[...] = ckv
    eye = _bf(jnp.where(lax.broadcasted_iota(jnp.int32, (KV_LATENT, KV_LATENT), 0)
                        == lax.broadcasted_iota(jnp.int32, (KV_LATENT, KV_LATENT), 1), 1.0, 0.0))
    ckvt_ref[0] = _bf(_dot_nt(eye, ckv))
    qi_ref[...] = _bf(p[:, o:o + LANES]); o += LANES
    ki_ref[...] = _bf(p[:, o:o + LANES]); o += LANES
    ap_ref[...] = p[:, o:o + POOL_WIDTH]
    wit_ref[...] = _dot_nt(wwi_ref[...], h)


def _proj_call(xf, sh1, sc1, g1n, w_r, w_wi_t, kvg, seq):
    n, d = xf.shape
    tpb = seq // TM_PROJ
    row = lambda i: (i, 0)
    per_b = lambda i: (i // tpb, 0, 0)
    fixed = lambda i: (0, 0)
    return pl.pallas_call(
        _proj_kernel,
        grid=(n // TM_PROJ,),
        in_specs=[
            pl.BlockSpec((TM_PROJ, d), row),
            pl.BlockSpec((1, 1, d), per_b),
            pl.BlockSpec((1, 1, d), per_b),
            pl.BlockSpec((1, d), fixed),
            pl.BlockSpec((d, PROJ_COLS), fixed),
            pl.BlockSpec((SUBLANES, d), fixed),
            pl.BlockSpec((1, KV_LATENT), fixed),
        ],
        out_specs=[
            pl.BlockSpec((TM_PROJ, ATTN_WIDTH), row),
            pl.BlockSpec((TM_PROJ, KV_LATENT), row),
            pl.BlockSpec((1, KV_LATENT, TM_PROJ), lambda i: (i, 0, 0)),
            pl.BlockSpec((TM_PROJ, LANES), row),
            pl.BlockSpec((TM_PROJ, LANES), row),
            pl.BlockSpec((SUBLANES, TM_PROJ), lambda i: (0, i)),
            pl.BlockSpec((TM_PROJ, POOL_WIDTH), row),
        ],
        out_shape=[
            jax.ShapeDtypeStruct((n, ATTN_WIDTH), jnp.bfloat16),
            jax.ShapeDtypeStruct((n, KV_LATENT), jnp.bfloat16),
            jax.ShapeDtypeStruct((n // TM_PROJ, KV_LATENT, TM_PROJ), jnp.bfloat16),
            jax.ShapeDtypeStruct((n, LANES), jnp.bfloat16),
            jax.ShapeDtypeStruct((n, LANES), jnp.bfloat16),
            jax.ShapeDtypeStruct((SUBLANES, n), jnp.float32),
            jax.ShapeDtypeStruct((n, POOL_WIDTH), jnp.float32),
        ],
        compiler_params=_cparams(("arbitrary",)),
        name="in_proj",
    )(xf, sh1, sc1, g1n, w_r, w_wi_t, kvg)


def _attn_kernel(q_ref, qi_ref, wit_ref, ckv_ref, ckvt_ref, ki_ref, wukt_ref, wuv_ref, o_ref,
                 key_buf, p_buf, acc_buf, m_buf, l_buf, a_buf, *, k_sel):
    qb = pl.program_id(1)
    q_start = qb * TQ
    n_kc = (q_start + TQ + TK - 1) // TK

    krow = lax.broadcasted_iota(jnp.int32, (TK, TQ), 0)
    qcol = q_start + lax.broadcasted_iota(jnp.int32, (TK, TQ), 1)

    qi = qi_ref[...]
    lane = lax.broadcasted_iota(jnp.int32, (TQ, LANES), 1)
    q4 = jnp.concatenate(
        [jnp.where((lane >= h * IDX_DIM) & (lane < (h + 1) * IDX_DIM), qi, jnp.zeros_like(qi))
         for h in range(IDX_HEADS)], axis=0)
    wit = wit_ref[...] * IDX_W_SCALE
    w_rows = [wit[h:h + 1, :] for h in range(IDX_HEADS)]

    def score_chunk(kc, carry):
        k0 = pl.multiple_of(kc * TK, TK)
        raw = _dot_nt(ki_ref[pl.ds(k0, TK), :], q4)
        score = w_rows[0] * jnp.maximum(raw[:, 0:TQ], 0.0)
        for h in range(1, IDX_HEADS):
            score = score + w_rows[h] * jnp.maximum(raw[:, h * TQ:(h + 1) * TQ], 0.0)
        score = score + 0.0
        bits = pltpu.bitcast(score, jnp.int32)
        key = jnp.where(bits < 0, bits ^ jnp.int32(0x7FFFFFFF), bits)
        key_buf[kc] = jnp.where(krow + k0 <= qcol, key, jnp.int32(INT_MIN))
        return carry

    lax.fori_loop(0, n_kc, score_chunk, 0)

    def count_keys(cand, strict):
        def body(kc, acc):
            for j in range(TK // CNT_ROWS):
                kj = key_buf[kc, j * CNT_ROWS:(j + 1) * CNT_ROWS, :]
                hit = (kj > cand) if strict else (kj >= cand)
                acc = acc + jnp.where(hit, 1.0, 0.0)
            return acc

        acc = lax.fori_loop(0, n_kc, body, jnp.zeros((CNT_ROWS, TQ), jnp.float32))
        return jnp.sum(acc, axis=0, keepdims=True)

    def bit_step(i, t_u):
        cand_u = t_u | lax.shift_left(jnp.int32(1), 31 - i)
        cnt = count_keys(cand_u ^ jnp.int32(INT_MIN), False)
        return jnp.where(cnt >= float(k_sel), cand_u, t_u)

    t_u = lax.fori_loop(0, 32, bit_step, jnp.zeros((1, TQ), jnp.int32))
    thr = t_u ^ jnp.int32(INT_MIN)
    n_tie_take = float(k_sel) - count_keys(thr, True)

    qlat_t = _dot_nt(wukt_ref[...], q_ref[...]) * ATTN_SCALE
    qs_t = _bf(jnp.concatenate(
        [qlat_t[h * KV_LATENT:(h + 1) * KV_LATENT, :] for h in range(N_HEADS)], axis=1))
    lower = _strict_lower(LANES)
    m_buf[...] = jnp.full(m_buf.shape, -3e38, jnp.float32)
    l_buf[...] = jnp.zeros(l_buf.shape, jnp.float32)
    acc_buf[...] = jnp.zeros(acc_buf.shape, jnp.float32)

    def attend_chunk(kc, tie_seen):
        k0 = pl.multiple_of(kc * TK, TK)
        key = key_buf[kc]
        eq = key == thr
        eq_f = jnp.where(eq, 1.0, 0.0)
        eq_b = _bf(eq_f)
        ranks = []
        for j in range(TK // LANES):
            rows = slice(j * LANES, (j + 1) * LANES)
            ranks.append(_dot(lower, eq_b[rows]) + tie_seen)
            tie_seen = tie_seen + jnp.sum(eq_f[rows], axis=0, keepdims=True)
        tie_rank = jnp.concatenate(ranks, axis=0)
        sel = ((key > thr) | (eq & (tie_rank < n_tie_take))) & (krow + k0 <= qcol)
        bias = jnp.where(sel, 0.0, NEG_BIG)
        logits = _dot(ckv_ref[pl.ds(k0, TK), :], qs_t)
        for h in range(N_HEADS):
            cols = slice(h * TQ, (h + 1) * TQ)
            lg = logits[:, cols] + bias
            m_old = m_buf[:, cols]
            m_new = jnp.maximum(m_old, jnp.max(lg, axis=0, keepdims=True))
            alpha = jnp.exp(m_old - m_new)
            p = jnp.exp(lg - m_new)
            l_buf[:, cols] = alpha * l_buf[:, cols] + jnp.sum(p, axis=0, keepdims=True)
            m_buf[:, cols] = m_new
            a_buf[:, cols] = alpha
            p_buf[:, cols] = _bf(p)
        acc_buf[...] = acc_buf[...] * a_buf[...] + _dot(ckvt_ref[kc], p_buf[...])
        return tie_seen

    lax.fori_loop(0, n_kc, attend_chunk, jnp.zeros((1, TQ), jnp.float32))

    o_lat_t = acc_buf[...] / l_buf[...]
    stacked = jnp.concatenate(
        [o_lat_t[:, h * TQ:(h + 1) * TQ] for h in range(N_HEADS)], axis=0)
    o_ref[...] = _bf(_dot(_bf(stacked.T), wuv_ref[...]))


def _attn_call(q, qi, wi_t, ckv, ckv_t, ki4, wukt_bd, wuv_bd, bsz, seq):
    n = q.shape[0]
    nq = seq // TQ
    nkc = seq // TK
    k_sel = min(TOPK_MAX, seq // 4)
    qrow = lambda b, i: (b * nq + i, 0)
    per_b = lambda b, i: (b, 0)
    fixed = lambda b, i: (0, 0)
    return pl.pallas_call(
        functools.partial(_attn_kernel, k_sel=k_sel),
        grid=(bsz, nq),
        in_specs=[
            pl.BlockSpec((TQ, ATTN_WIDTH), qrow),
            pl.BlockSpec((TQ, LANES), qrow),
            pl.BlockSpec((SUBLANES, TQ), lambda b, i: (0, b * nq + i)),
            pl.BlockSpec((seq, KV_LATENT), per_b),
            pl.BlockSpec((nkc, KV_LATENT, TK), lambda b, i: (b, 0, 0)),
            pl.BlockSpec((seq, LANES), per_b),
            pl.BlockSpec((N_HEADS * KV_LATENT, ATTN_WIDTH), fixed),
            pl.BlockSpec((N_HEADS * KV_LATENT, ATTN_WIDTH), fixed),
        ],
        out_specs=pl.BlockSpec((TQ, ATTN_WIDTH), qrow),
        out_shape=jax.ShapeDtypeStruct((n, ATTN_WIDTH), jnp.bfloat16),
        scratch_shapes=[
            pltpu.VMEM((nkc, TK, TQ), jnp.int32),
            pltpu.VMEM((TK, N_HEADS * TQ), jnp.bfloat16),
            pltpu.VMEM((KV_LATENT, N_HEADS * TQ), jnp.float32),
            pltpu.VMEM((1, N_HEADS * TQ), jnp.float32),
            pltpu.VMEM((1, N_HEADS * TQ), jnp.float32),
            pltpu.VMEM((1, N_HEADS * TQ), jnp.float32),
        ],
        compiler_params=_cparams(("arbitrary", "arbitrary")),
        name="dsa_attention",
    )(q, qi, wi_t, ckv, ckv_t, ki4, wukt_bd, wuv_bd)


def _mix_kernel(x_ref, oa_ref, ap_ref, halo_ref, sh1_ref, sc1_ref, g1_ref, n1g_ref,
                wgate_ref, wpool_ref, pscale_ref, waup_ref, wbup_ref, wout_ref,
                sh2_ref, sc2_ref, n2g_ref, wr_ref, br_ref,
                x1_ref, h2_ref, lg_ref, *, tiles_per_batch):
    i = pl.program_id(0)
    t_in_b = i % tiles_per_batch
    x = x_ref[...]
    d = x.shape[1]
    h = _rms(x, n1g_ref[...]) * (1.0 + sc1_ref[0]) + sh1_ref[0]
    gates = _dot(_bf(h), wgate_ref[...])

    pos1 = (t_in_b * TM_MIX + 1 + lax.broadcasted_iota(jnp.int32, (TM_MIX, 1), 0)).astype(jnp.float32)
    halo = jnp.where(t_in_b == 0, 0.0, halo_ref[...])
    zs = []
    for g, w in enumerate(POOL_WINDOWS):
        sl = slice(g * POOL_GROUP_DIM, (g + 1) * POOL_GROUP_DIM)
        a = ap_ref[:, sl]
        ext = jnp.concatenate([halo[:, sl], a], axis=0)
        span = 1
        while span < w:
            ext = ext + pltpu.roll(ext, span, 0)
            span *= 2
        mean = ext[POOL_HALO:] / jnp.minimum(pos1, float(w))
        zs.append(_dot(_bf(mean - a), wpool_ref[g]))
    o_b = jnp.concatenate(zs, axis=1) * pscale_ref[...]

    merged = (jax.nn.sigmoid(gates[:, :d]) * _dot(oa_ref[...], waup_ref[...])
              + jax.nn.sigmoid(gates[:, d:]) * _dot(_bf(o_b), wbup_ref[...]))
    x1 = x + g1_ref[0] * _dot(_bf(merged), wout_ref[...])
    x1_ref[...] = x1
    h2 = _rms(x1, n2g_ref[...]) * (1.0 + sc2_ref[0]) + sh2_ref[0]
    h2_ref[...] = h2
    lg_ref[...] = _dot(_bf(h2), wr_ref[...]) + br_ref[...]


def _mix_call(xf, o_a, apool, sh1, sc1, g1, n1g, wgate, wpool, pscale, waup, wbup, wout,
              sh2, sc2, n2g, wr, br, seq):
    n, d = xf.shape
    tpb = seq // TM_MIX
    row = lambda i: (i, 0)
    per_b = lambda i: (i // tpb, 0, 0)
    fixed = lambda i: (0, 0)
    fixed3 = lambda i: (0, 0, 0)
    halo_idx = lambda i: (jnp.maximum(i * (TM_MIX // POOL_HALO) - 1, 0), 0)
    mod = pl.BlockSpec((1, 1, d), per_b)
    return pl.pallas_call(
        functools.partial(_mix_kernel, tiles_per_batch=tpb),
        grid=(n // TM_MIX,),
        in_specs=[
            pl.BlockSpec((TM_MIX, d), row),
            pl.BlockSpec((TM_MIX, ATTN_WIDTH), row),
            pl.BlockSpec((TM_MIX, POOL_WIDTH), row),
            pl.BlockSpec((POOL_HALO, POOL_WIDTH), halo_idx),
            mod, mod, mod,
            pl.BlockSpec((1, d), fixed),
            pl.BlockSpec((d, 2 * d), fixed),
            pl.BlockSpec((len(POOL_WINDOWS), POOL_GROUP_DIM, POOL_GROUP_DIM), fixed3),
            pl.BlockSpec((1, POOL_WIDTH), fixed),
            pl.BlockSpec((ATTN_WIDTH, d), fixed),
            pl.BlockSpec((POOL_WIDTH, d), fixed),
            pl.BlockSpec((d, d), fixed),
            mod, mod,
            pl.BlockSpec((1, d), fixed),
            pl.BlockSpec((d, LANES), fixed),
            pl.BlockSpec((1, LANES), fixed),
        ],
        out_specs=[pl.BlockSpec((TM_MIX, d), row), pl.BlockSpec((TM_MIX, d), row),
                   pl.BlockSpec((TM_MIX, LANES), row)],
        out_shape=[jax.ShapeDtypeStruct((n, d), jnp.float32), jax.ShapeDtypeStruct((n, d), jnp.float32),
                   jax.ShapeDtypeStruct((n, LANES), jnp.float32)],
        compiler_params=_cparams(("arbitrary",)),
        name="mix_merge",
    )(xf, o_a, apool, apool, sh1, sc1, g1, n1g, wgate, wpool, pscale, waup, wbup, wout,
      sh2, sc2, n2g, wr, br)


def _route_kernel(lg_ref, idx_ref, wt_ref, rank_ref, cnt_ref, carry):
    @pl.when(pl.program_id(0) == 0)
    def _():
        carry[...] = jnp.zeros(carry.shape, jnp.float32)

    work = lg_ref[...]
    lane = lax.broadcasted_iota(jnp.int32, work.shape, 1).astype(jnp.float32)
    vals, idxs = [], []
    for _ in range(TOP_K):
        m = jnp.max(work, axis=1, keepdims=True)
        idx = jnp.min(jnp.where(work == m, lane, float(LANES)), axis=1, keepdims=True)
        vals.append(m)
        idxs.append(idx)
        work = jnp.where(lane == idx, -jnp.inf, work)
    exps = [jnp.exp(v - vals[0]) for v in vals]
    denom = exps[0] + exps[1] + exps[2] + exps[3]

    onehot = jnp.zeros(work.shape, jnp.float32)
    for idx in idxs:
        onehot = onehot + jnp.where(lane == idx, 1.0, 0.0)
    before = _dot(_strict_lower(work.shape[0]), _bf(onehot)) + carry[...]

    idx_out = jnp.zeros(work.shape, jnp.int32)
    wt_out = jnp.zeros(work.shape, jnp.float32)
    rank_out = jnp.zeros(work.shape, jnp.int32)
    for k in range(TOP_K):
        rank_k = jnp.sum(jnp.where(lane == idxs[k], before, 0.0), axis=1, keepdims=True)
        idx_out = jnp.where(lane == k, idxs[k].astype(jnp.int32), idx_out)
        wt_out = jnp.where(lane == k, exps[k] / denom, wt_out)
        rank_out = jnp.where(lane == k, rank_k.astype(jnp.int32), rank_out)
    idx_ref[...] = idx_out
    wt_ref[...] = wt_out
    rank_ref[...] = rank_out
    carry[...] = carry[...] + jnp.sum(onehot, axis=0, keepdims=True)
    cnt_ref[...] = carry[...]


def _route_call(logits):
    n = logits.shape[0]
    row = lambda i: (i, 0)
    return pl.pallas_call(
        _route_kernel,
        grid=(n // TM_ROUTE,),
        in_specs=[pl.BlockSpec((TM_ROUTE, LANES), row)],
        out_specs=[pl.BlockSpec((TM_ROUTE, LANES), row)] * 3 + [pl.BlockSpec((1, LANES), lambda i: (0, 0))],
        out_shape=[jax.ShapeDtypeStruct((n, LANES), jnp.int32), jax.ShapeDtypeStruct((n, LANES), jnp.float32),
                   jax.ShapeDtypeStruct((n, LANES), jnp.int32), jax.ShapeDtypeStruct((1, LANES), jnp.float32)],
        scratch_shapes=[pltpu.VMEM((1, LANES), jnp.float32)],
        compiler_params=_cparams(("arbitrary",)),
        name="route",
    )(logits)


def _scatter_kernel(endpad_ref, dest_ref, h2_ref, xs_ref, zero_buf, sem_zero, sem_rows):
    @pl.when(pl.program_id(0) == 0)
    def _():
        zero_buf[...] = jnp.zeros(zero_buf.shape, zero_buf.dtype)

        def zero_block(start):
            return pltpu.make_async_copy(
                zero_buf, xs_ref.at[pl.ds(pl.multiple_of(start, MOE_BLOCK), MOE_BLOCK), :], sem_zero)

        def last_block_of(e):
            return jnp.maximum(endpad_ref[e] - MOE_BLOCK, 0)

        def start_zero(e, c):
            zero_block(last_block_of(e)).start()
            return c

        def wait_zero(e, c):
            zero_block(last_block_of(e)).wait()
            return c

        def start_tail(b, c):
            zero_block(b * MOE_BLOCK).start()
            return c

        def wait_tail(b, c):
            zero_block(b * MOE_BLOCK).wait()
            return c

        n_used = endpad_ref[N_EXPERTS - 1] // MOE_BLOCK
        n_blocks = xs_ref.shape[0] // MOE_BLOCK
        lax.fori_loop(0, N_EXPERTS, start_zero, 0)
        lax.fori_loop(n_used, n_blocks, start_tail, 0)
        lax.fori_loop(0, N_EXPERTS, wait_zero, 0)
        lax.fori_loop(n_used, n_blocks, wait_tail, 0)

    def row_copy(r, k):
        return pltpu.make_async_copy(h2_ref.at[pl.ds(r, 1), :],
                                     xs_ref.at[pl.ds(dest_ref[r * TOP_K + k], 1), :], sem_rows)

    def start_row(r, c):
        for k in range(TOP_K):
            row_copy(r, k).start()
        return c

    def wait_row(r, c):
        for k in range(TOP_K):
            row_copy(r, k).wait()
        return c

    lax.fori_loop(0, TM_SCATTER, start_row, 0)
    lax.fori_loop(0, TM_SCATTER, wait_row, 0)


def _scatter_call(h2, dest_flat, end_pad, n_rows):
    n, d = h2.shape
    grid_spec = pltpu.PrefetchScalarGridSpec(
        num_scalar_prefetch=1,
        grid=(n // TM_SCATTER,),
        in_specs=[
            pl.BlockSpec((TM_SCATTER * TOP_K,), lambda i, ep: (i,), memory_space=pltpu.SMEM),
            pl.BlockSpec((TM_SCATTER, d), lambda i, ep: (i, 0)),
        ],
        out_specs=pl.BlockSpec(memory_space=pl.ANY),
        scratch_shapes=[pltpu.VMEM((MOE_BLOCK, d), h2.dtype),
                        pltpu.SemaphoreType.DMA(()), pltpu.SemaphoreType.DMA(())],
    )
    return pl.pallas_call(
        _scatter_kernel,
        grid_spec=grid_spec,
        out_shape=jax.ShapeDtypeStruct((n_rows, d), h2.dtype),
        compiler_params=_cparams(("arbitrary",)),
        name="moe_scatter",
    )(end_pad, dest_flat, h2)


def _expert_kernel(blk_e_ref, blk_src_ref, blk_on_ref, xs_ref, wgu_ref, bgu_ref, wdn_ref, bdn_ref, ys_ref):
    on = blk_on_ref[pl.program_id(0)] == 1

    @pl.when(jnp.logical_not(on))
    def _():
        ys_ref[...] = jnp.zeros(ys_ref.shape, ys_ref.dtype)

    @pl.when(on)
    def _():
        f = wdn_ref.shape[1]
        gu = _dot(_bf(xs_ref[...]), wgu_ref[0]) + bgu_ref[0]
        gt = jnp.minimum(gu[:, :f], SWIGLU_LIMIT)
        up = jnp.clip(gu[:, f:], -SWIGLU_LIMIT, SWIGLU_LIMIT)
        act = gt * jax.nn.sigmoid(SWIGLU_ALPHA * gt) * (up + 1.0)
        ys_ref[...] = _dot(_bf(act), wdn_ref[0]) + bdn_ref[0]


def _expert_call(xs, blk_e, blk_src, blk_on, wgu, bgu, wdn, bdn):
    n_rows, d = xs.shape
    n_blocks = n_rows // MOE_BLOCK
    f2 = wgu.shape[2]
    f = wdn.shape[1]
    src = lambda i, be, bs, bo: (bs[i], 0)
    by_e = lambda i, be, bs, bo: (be[i], 0, 0)
    grid_spec = pltpu.PrefetchScalarGridSpec(
        num_scalar_prefetch=3,
        grid=(n_blocks,),
        in_specs=[
            pl.BlockSpec((MOE_BLOCK, d), src),
            pl.BlockSpec((1, d, f2), by_e),
            pl.BlockSpec((1, 1, f2), by_e),
            pl.BlockSpec((1, f, d), by_e),
            pl.BlockSpec((1, 1, d), by_e),
        ],
        out_specs=pl.BlockSpec((MOE_BLOCK, d), lambda i, be, bs, bo: (i, 0)),
    )
    return pl.pallas_call(
        _expert_kernel,
        grid_spec=grid_spec,
        out_shape=jax.ShapeDtypeStruct((n_rows, d), jnp.float32),
        compiler_params=_cparams(("arbitrary",)),
        name="moe_experts",
    )(blk_e, blk_src, blk_on, xs, wgu, bgu, wdn, bdn)


def _combine_kernel(dest_ref, x1_ref, wt_ref, g2_ref, fg_ref, ys_ref, o_ref, buf, sem, *, final_norm):
    def row_copy(r, k):
        return pltpu.make_async_copy(ys_ref.at[pl.ds(dest_ref[r * TOP_K + k], 1), :],
                                     buf.at[k, pl.ds(r, 1), :], sem)

    def start_row(r, c):
        for k in range(TOP_K):
            row_copy(r, k).start()
        return c

    def wait_row(r, c):
        for k in range(TOP_K):
            row_copy(r, k).wait()
        return c

    lax.fori_loop(0, TM_COMBINE, start_row, 0)
    lax.fori_loop(0, TM_COMBINE, wait_row, 0)

    wt = wt_ref[...]
    y = wt[:, 0:1] * buf[0]
    for k in range(1, TOP_K):
        y = y + wt[:, k:k + 1] * buf[k]
    out = x1_ref[...] + g2_ref[0] * y
    if final_norm:
        out = _rms(out, fg_ref[...])
    o_ref[...] = out


def _combine_call(x1, wts, g2, final_g, ys, dest_flat, seq, final_norm):
    n, d = x1.shape
    tpb = seq // TM_COMBINE
    row = lambda i: (i, 0)
    return pl.pallas_call(
        functools.partial(_combine_kernel, final_norm=final_norm),
        grid=(n // TM_COMBINE,),
        in_specs=[
            pl.BlockSpec((TM_COMBINE * TOP_K,), lambda i: (i,), memory_space=pltpu.SMEM),
            pl.BlockSpec((TM_COMBINE, d), row),
            pl.BlockSpec((TM_COMBINE, LANES), row),
            pl.BlockSpec((1, 1, d), lambda i: (i // tpb, 0, 0)),
            pl.BlockSpec((1, d), lambda i: (0, 0)),
            pl.BlockSpec(memory_space=pl.ANY),
        ],
        out_specs=pl.BlockSpec((TM_COMBINE, d), row),
        out_shape=jax.ShapeDtypeStruct((n, d), jnp.float32),
        scratch_shapes=[pltpu.VMEM((TOP_K, TM_COMBINE, d), jnp.float32), pltpu.SemaphoreType.DMA(())],
        compiler_params=_cparams(("arbitrary",)),
        name="moe_combine",
    )(dest_flat, x1, wts, g2, final_g, ys)


def _regroup_w_in(w_in):
    o_q, o_kv = 0, ATTN_WIDTH
    o_qi = o_kv + KV_LATENT
    o_ki = o_qi + IDX_HEADS * IDX_DIM
    o_wi = o_ki + IDX_DIM
    o_pool = o_wi + IDX_HEADS
    o_ga = o_pool + POOL_WIDTH
    ki = w_in[..., o_ki:o_wi]
    w_r = jnp.concatenate([w_in[..., o_q:o_ki], ki, ki, ki, ki, w_in[..., o_pool:o_ga]], axis=-1)
    w_wi_t = jnp.pad(jnp.swapaxes(w_in[..., o_wi:o_pool], 1, 2), ((0, 0), (0, SUBLANES - IDX_HEADS), (0, 0)))
    return _bf(w_r), _bf(w_wi_t), _bf(w_in[..., o_ga:])


def _block_diag_uk_t(w_uk):
    depth = w_uk.shape[0]
    eye = jnp.eye(N_HEADS, dtype=w_uk.dtype)
    t = jnp.einsum('zlhd,hg->zglhd', w_uk, eye)
    return _bf(t.reshape(depth, N_HEADS * KV_LATENT, N_HEADS * HEAD_DIM))


def _block_diag_uv(w_uv):
    depth = w_uv.shape[0]
    eye = jnp.eye(N_HEADS, dtype=w_uv.dtype)
    t = jnp.einsum('zlhd,hg->zhlgd', w_uv, eye)
    return _bf(t.reshape(depth, N_HEADS * KV_LATENT, N_HEADS * HEAD_DIM))


def kernel(x, c, norm1_g, norm2_g, w_ada, b_ada, w_in, kv_norm_g, w_uk, w_uv, w_pool, pool_scale, w_a_up, w_b_up, w_out, w_router, b_router, w_gu, b_gu, w_down, b_down, final_g):
    bsz, seq, d = x.shape
    depth = w_in.shape[0]
    n = bsz * seq
    assert TM_PROJ == TK and seq % TK == 0 and n % MOE_BLOCK == 0

    ada = _ada_call(c, w_ada, b_ada).reshape(depth, bsz, 6, 1, d)
    w_r, w_wi_t, w_gate = _regroup_w_in(w_in)
    wukt_bd = _block_diag_uk_t(w_uk)
    wuv_bd = _block_diag_uv(w_uv)
    w_pool_b, w_a_up_b, w_b_up_b, w_out_b = _bf(w_pool), _bf(w_a_up), _bf(w_b_up), _bf(w_out)
    w_router_p = _bf(jnp.pad(w_router, ((0, 0), (0, 0), (0, LANES - N_EXPERTS))))
    b_router_p = jnp.pad(b_router, ((0, 0), (0, LANES - N_EXPERTS)), constant_values=NEG_BIG)
    w_gu_b, w_down_b = _bf(w_gu), _bf(w_down)

    n_asg = n * TOP_K
    n_blocks = -(-n_asg // MOE_BLOCK) + N_EXPERTS
    n_rows = n_blocks * MOE_BLOCK

    xf = x.reshape(n, d)
    for l in range(depth):
        sh1, sc1, g1, sh2, sc2, g2 = [ada[l, :, j] for j in range(6)]
        q, ckv, ckv_t, qi, ki4, wi_t, apool = _proj_call(
            xf, sh1, sc1, norm1_g[l][None], w_r[l], w_wi_t[l], kv_norm_g[l][None], seq)
        o_a = _attn_call(q, qi, wi_t, ckv, ckv_t, ki4, wukt_bd[l], wuv_bd[l], bsz, seq)
        x1, h2, logits = _mix_call(xf, o_a, apool, sh1, sc1, g1, norm1_g[l][None], w_gate[l], w_pool_b[l],
                                   pool_scale[l][None], w_a_up_b[l], w_b_up_b[l], w_out_b[l],
                                   sh2, sc2, norm2_g[l][None], w_router_p[l], b_router_p[l][None], seq)
        idx_l, wts, rank_l, counts = _route_call(logits)

        sizes = counts[0, :N_EXPERTS].astype(jnp.int32)
        padded = ((sizes + MOE_BLOCK - 1) // MOE_BLOCK) * MOE_BLOCK
        end_pad = jnp.cumsum(padded)
        start_pad = end_pad - padded
        dest = start_pad[idx_l[:, :TOP_K]] + rank_l[:, :TOP_K]
        dest_flat = dest.reshape(-1).astype(jnp.int32)
        blk_start = jnp.arange(n_blocks, dtype=jnp.int32) * MOE_BLOCK
        blk_e = jnp.minimum(jnp.searchsorted(end_pad, blk_start, side='right'), N_EXPERTS - 1).astype(jnp.int32)
        n_used = end_pad[-1] // MOE_BLOCK
        blk_idx = jnp.arange(n_blocks, dtype=jnp.int32)
        blk_on = (blk_idx < n_used).astype(jnp.int32)
        blk_src = jnp.minimum(blk_idx, n_used - 1).astype(jnp.int32)
        blk_e = jnp.where(blk_on == 1, blk_e, blk_e[n_used - 1])

        xs = _scatter_call(h2, dest_flat, end_pad.astype(jnp.int32), n_rows)
        ys = _expert_call(xs, blk_e, blk_src, blk_on, w_gu_b[l], b_gu[l][:, None, :], w_down_b[l], b_down[l][:, None, :])
        xf = _combine_call(x1, wts, g2, final_g[None], ys, dest_flat, seq, final_norm=(l == depth - 1))
    return xf.reshape(bsz, seq, d)
```

```python
import functools

import jax
import jax.numpy as jnp
from jax import lax
from jax.experimental import pallas as pl
from jax.experimental.pallas import tpu as pltpu

N_HEADS = 8
HEAD_DIM = 64
ATTN_WIDTH = N_HEADS * HEAD_DIM
KV_LATENT = 128
IDX_HEADS = 4
IDX_DIM = 32
TOPK_MAX = 256
ATTN_SCALE = HEAD_DIM ** -0.5
IDX_W_SCALE = (IDX_HEADS ** -0.5) * (IDX_DIM ** -0.5)
POOL_WINDOWS = (2, 4, 8, 16)
POOL_GROUP_DIM = 128
POOL_WIDTH = len(POOL_WINDOWS) * POOL_GROUP_DIM
N_EXPERTS = 32
TOP_K = 4
MOE_BLOCK = 512
SWIGLU_LIMIT = 7.0
SWIGLU_ALPHA = 1.702
RMS_EPS = 1e-6

LANES = 128
SUBLANES = 8
POOL_HALO = 16
NEG_BIG = -1e30
INT_MIN = -2 ** 31
HALF_BITS = 16
HALF_OFFSET = 2 ** (HALF_BITS - 1)

TM_PROJ = 512
TQ = 128
TK = 512
GROUP_HEADS = 8
CNT_ROWS = 64
ONES_ROWS = 16
KVT_ROWS = KV_LATENT + ONES_ROWS
LOG2E = 1.4426950408889634
TM_MIX = 256
TM_ROUTE = 512
TM_SCATTER = 256
TM_COMBINE = 256
VMEM_LIMIT = 56 * 1024 * 1024


def _cparams(sem):
    return pltpu.CompilerParams(dimension_semantics=sem, vmem_limit_bytes=VMEM_LIMIT)


def _rms(xf, g):
    return xf * lax.rsqrt(jnp.mean(xf * xf, axis=-1, keepdims=True) + RMS_EPS) * g


def _bf(v):
    return v.astype(jnp.bfloat16)


def _dot(a, b):
    return jnp.dot(a, b, preferred_element_type=jnp.float32)


def _dot_nt(a, b):
    return lax.dot_general(a, b, (((1,), (1,)), ((), ())), preferred_element_type=jnp.float32)


def _strict_lower(n):
    return _bf(jnp.where(lax.broadcasted_iota(jnp.int32, (n, n), 1)
                         < lax.broadcasted_iota(jnp.int32, (n, n), 0), 1.0, 0.0))


def _ada_kernel(c_ref, w_ref, b_ref, o_ref):
    cf = c_ref[...]
    cond = cf * jax.nn.sigmoid(cf)
    o_ref[0] = _dot(_bf(cond), _bf(w_ref[0])) + b_ref[0]


def _ada_call(c, w_ada, b_ada):
    depth, d, n6 = w_ada.shape
    bsz = c.shape[0]
    tn = 1024
    return pl.pallas_call(
        _ada_kernel,
        grid=(depth, n6 // tn),
        in_specs=[
            pl.BlockSpec((bsz, d), lambda l, j: (0, 0)),
            pl.BlockSpec((1, d, tn), lambda l, j: (l, 0, j)),
            pl.BlockSpec((1, 1, tn), lambda l, j: (l, 0, j)),
        ],
        out_specs=pl.BlockSpec((1, bsz, tn), lambda l, j: (l, 0, j)),
        out_shape=jax.ShapeDtypeStruct((depth, bsz, n6), jnp.float32),
        compiler_params=_cparams(("arbitrary", "arbitrary")),
        name="ada",
    )(c, w_ada, b_ada.reshape(depth, 1, n6))


PROJ_COLS = ATTN_WIDTH + 3 * LANES + POOL_WIDTH


def _proj_kernel(x_ref, sh_ref, sc_ref, g_ref, w_ref, wwi_ref, kvg_ref,
                 q_ref, ckv_ref, ckvt_ref, qi_ref, ki_ref, wit_ref, ap_ref):
    h = _bf(_rms(x_ref[...], g_ref[...]) * (1.0 + sc_ref[0]) + sh_ref[0])
    p = _dot(h, w_ref[...])
    o = 0
    q_ref[...] = _bf(p[:, o:o + ATTN_WIDTH]); o += ATTN_WIDTH
    ckv = _bf(_rms(p[:, o:o + KV_LATENT], kvg_ref[...])); o += KV_LATENT
    ckv_ref[...] = ckv
    eye = _bf(jnp.where(lax.broadcasted_iota(jnp.int32, (KV_LATENT, KV_LATENT), 0)
                        == lax.broadcasted_iota(jnp.int32, (KV_LATENT, KV_LATENT), 1), 1.0, 0.0))
    ckvt_ref[0, :KV_LATENT, :] = _bf(_dot_nt(eye, ckv))
    ckvt_ref[0, KV_LATENT:, :] = jnp.ones((ONES_ROWS, TM_PROJ), jnp.bfloat16)
    qi_ref[...] = _bf(p[:, o:o + LANES]); o += LANES
    ki_ref[...] = _bf(p[:, o:o + LANES]); o += LANES
    ap_ref[...] = p[:, o:o + POOL_WIDTH]
    wit_ref[...] = _dot_nt(wwi_ref[...], h)


def _proj_call(xf, sh1, sc1, g1n, w_r, w_wi_t, kvg, seq):
    n, d = xf.shape
    tpb = seq // TM_PROJ
    row = lambda i: (i, 0)
    per_b = lambda i: (i // tpb, 0, 0)
    fixed = lambda i: (0, 0)
    return pl.pallas_call(
        _proj_kernel,
        grid=(n // TM_PROJ,),
        in_specs=[
            pl.BlockSpec((TM_PROJ, d), row),
            pl.BlockSpec((1, 1, d), per_b),
            pl.BlockSpec((1, 1, d), per_b),
            pl.BlockSpec((1, d), fixed),
            pl.BlockSpec((d, PROJ_COLS), fixed),
            pl.BlockSpec((SUBLANES, d), fixed),
            pl.BlockSpec((1, KV_LATENT), fixed),
        ],
        out_specs=[
            pl.BlockSpec((TM_PROJ, ATTN_WIDTH), row),
            pl.BlockSpec((TM_PROJ, KV_LATENT), row),
            pl.BlockSpec((1, KVT_ROWS, TM_PROJ), lambda i: (i, 0, 0)),
            pl.BlockSpec((TM_PROJ, LANES), row),
            pl.BlockSpec((TM_PROJ, LANES), row),
            pl.BlockSpec((SUBLANES, TM_PROJ), lambda i: (0, i)),
            pl.BlockSpec((TM_PROJ, POOL_WIDTH), row),
        ],
        out_shape=[
            jax.ShapeDtypeStruct((n, ATTN_WIDTH), jnp.bfloat16),
            jax.ShapeDtypeStruct((n, KV_LATENT), jnp.bfloat16),
            jax.ShapeDtypeStruct((n // TM_PROJ, KVT_ROWS, TM_PROJ), jnp.bfloat16),
            jax.ShapeDtypeStruct((n, LANES), jnp.bfloat16),
            jax.ShapeDtypeStruct((n, LANES), jnp.bfloat16),
            jax.ShapeDtypeStruct((SUBLANES, n), jnp.float32),
            jax.ShapeDtypeStruct((n, POOL_WIDTH), jnp.float32),
        ],
        compiler_params=_cparams(("arbitrary",)),
        name="in_proj",
    )(xf, sh1, sc1, g1n, w_r, w_wi_t, kvg)


def _attn_kernel(q_ref, qi_ref, wit_ref, ckv_ref, ckvt_ref, ki_ref, wukt_ref, wuv_ref, o_ref,
                 key_buf, hi_buf, lo_buf, qs_buf, acc_buf, m_buf, *, k_sel):
    qb = pl.program_id(1)
    q_start = qb * TQ
    n_kc = (q_start + TQ + TK - 1) // TK

    krow = lax.broadcasted_iota(jnp.int32, (TK, TQ), 0)
    qcol = q_start + lax.broadcasted_iota(jnp.int32, (TK, TQ), 1)

    qi = qi_ref[...]
    lane = lax.broadcasted_iota(jnp.int32, (TQ, LANES), 1)
    q4 = jnp.concatenate(
        [jnp.where((lane >= h * IDX_DIM) & (lane < (h + 1) * IDX_DIM), qi, jnp.zeros_like(qi))
         for h in range(IDX_HEADS)], axis=0)
    wit = wit_ref[...] * IDX_W_SCALE
    w_rows = [wit[h:h + 1, :] for h in range(IDX_HEADS)]

    def score_chunk(kc, carry):
        k0 = pl.multiple_of(kc * TK, TK)
        raw = _dot_nt(ki_ref[pl.ds(k0, TK), :], q4)
        score = w_rows[0] * jnp.maximum(raw[:, 0:TQ], 0.0)
        for h in range(1, IDX_HEADS):
            score = score + w_rows[h] * jnp.maximum(raw[:, h * TQ:(h + 1) * TQ], 0.0)
        score = score + 0.0
        bits = pltpu.bitcast(score, jnp.int32)
        key = jnp.where(bits < 0, bits ^ jnp.int32(0x7FFFFFFF), bits)
        key = jnp.where(krow + k0 <= qcol, key, jnp.int32(INT_MIN))
        key_buf[kc] = key
        hi_buf[kc] = (key >> HALF_BITS).astype(jnp.int16)
        return carry

    lax.fori_loop(0, n_kc, score_chunk, 0)

    def count16(plane, cand, strict):
        def body(kc, acc):
            for j in range(TK // CNT_ROWS):
                pj = plane[kc, j * CNT_ROWS:(j + 1) * CNT_ROWS, :]
                hit = (pj > cand) if strict else (pj >= cand)
                acc = acc + jnp.where(hit, jnp.bfloat16(1.0), jnp.bfloat16(0.0))
            return acc

        acc = lax.fori_loop(0, n_kc, body, jnp.zeros((CNT_ROWS, TQ), jnp.bfloat16))
        return jnp.sum(acc.astype(jnp.float32), axis=0, keepdims=True)

    def to_plane(t_u):
        return (t_u - HALF_OFFSET).astype(jnp.int16)

    def search16(plane, need):
        def bit_step(i, t_u):
            cand_u = t_u | lax.shift_left(jnp.int32(1), HALF_BITS - 1 - i)
            cnt = count16(plane, to_plane(cand_u), False)
            return jnp.where(cnt >= need, cand_u, t_u)

        return lax.fori_loop(0, HALF_BITS, bit_step, jnp.zeros((1, TQ), jnp.int32))

    hi_u = search16(hi_buf, float(k_sel))
    thr_hi = to_plane(hi_u)
    need_lo = float(k_sel) - count16(hi_buf, thr_hi, True)

    def low_plane_chunk(kc, carry):
        lo = ((key_buf[kc] & jnp.int32(HALF_OFFSET * 2 - 1)) - HALF_OFFSET).astype(jnp.int16)
        lo_buf[kc] = jnp.where(hi_buf[kc] == thr_hi, lo, jnp.int16(-HALF_OFFSET))
        return carry

    lax.fori_loop(0, n_kc, low_plane_chunk, 0)
    lo_u = search16(lo_buf, need_lo)
    thr = lax.shift_left(hi_u - HALF_OFFSET, HALF_BITS) | lo_u
    n_tie_take = need_lo - count16(lo_buf, to_plane(lo_u), True)

    qlat_t = _dot_nt(wukt_ref[...], q_ref[...]) * (ATTN_SCALE * LOG2E)
    for h in range(N_HEADS):
        qs_buf[:, h * TQ:(h + 1) * TQ] = _bf(qlat_t[h * KV_LATENT:(h + 1) * KV_LATENT, :])
    lower = _strict_lower(LANES)
    m_buf[...] = jnp.full(m_buf.shape, -3e38, jnp.float32)
    acc_buf[...] = jnp.zeros(acc_buf.shape, jnp.float32)

    def attend_chunk(kc, tie_seen):
        k0 = pl.multiple_of(kc * TK, TK)
        key = key_buf[kc]
        eq = key == thr
        eq_f = jnp.where(eq, 1.0, 0.0)
        eq_b = _bf(eq_f)
        ranks = []
        for j in range(TK // LANES):
            rows = slice(j * LANES, (j + 1) * LANES)
            ranks.append(_dot(lower, eq_b[rows]) + tie_seen)
            tie_seen = tie_seen + jnp.sum(eq_f[rows], axis=0, keepdims=True)
        tie_rank = jnp.concatenate(ranks, axis=0)
        sel = ((key > thr) | (eq & (tie_rank < n_tie_take))) & (krow + k0 <= qcol)
        bias = jnp.where(sel, 0.0, NEG_BIG)
        kv = ckv_ref[pl.ds(k0, TK), :]
        kv_t = ckvt_ref[kc]
        for g in range(N_HEADS // GROUP_HEADS):
            gcols = slice(g * GROUP_HEADS * TQ, (g + 1) * GROUP_HEADS * TQ)
            logits = _dot(kv, qs_buf[:, gcols])
            ps, alphas = [], []
            for hh in range(GROUP_HEADS):
                cols = slice((g * GROUP_HEADS + hh) * TQ, (g * GROUP_HEADS + hh + 1) * TQ)
                lg = logits[:, hh * TQ:(hh + 1) * TQ] + bias
                m_old = m_buf[:, cols]
                m_new = jnp.maximum(m_old, jnp.max(lg, axis=0, keepdims=True))
                m_buf[:, cols] = m_new
                alphas.append(jnp.exp2(m_old - m_new))
                ps.append(_bf(jnp.exp2(lg - m_new)))
            acc_buf[:, gcols] = (acc_buf[:, gcols] * jnp.concatenate(alphas, axis=1)
                                 + _dot(kv_t, jnp.concatenate(ps, axis=1)))
        return tie_seen

    lax.fori_loop(0, n_kc, attend_chunk, jnp.zeros((1, TQ), jnp.float32))

    o_lat_t = acc_buf[:KV_LATENT, :] / acc_buf[KV_LATENT:KV_LATENT + 1, :]
    stacked = jnp.concatenate(
        [o_lat_t[:, h * TQ:(h + 1) * TQ] for h in range(N_HEADS)], axis=0)
    o_ref[...] = _bf(_dot(_bf(stacked.T), wuv_ref[...]))


def _attn_call(q, qi, wi_t, ckv, ckv_t, ki4, wukt_bd, wuv_bd, bsz, seq):
    n = q.shape[0]
    nq = seq // TQ
    nkc = seq // TK
    k_sel = min(TOPK_MAX, seq // 4)
    qrow = lambda b, i: (b * nq + i, 0)
    per_b = lambda b, i: (b, 0)
    fixed = lambda b, i: (0, 0)
    return pl.pallas_call(
        functools.partial(_attn_kernel, k_sel=k_sel),
        grid=(bsz, nq),
        in_specs=[
            pl.BlockSpec((TQ, ATTN_WIDTH), qrow),
            pl.BlockSpec((TQ, LANES), qrow),
            pl.BlockSpec((SUBLANES, TQ), lambda b, i: (0, b * nq + i)),
            pl.BlockSpec((seq, KV_LATENT), per_b),
            pl.BlockSpec((nkc, KVT_ROWS, TK), lambda b, i: (b, 0, 0)),
            pl.BlockSpec((seq, LANES), per_b),
            pl.BlockSpec((N_HEADS * KV_LATENT, ATTN_WIDTH), fixed),
            pl.BlockSpec((N_HEADS * KV_LATENT, ATTN_WIDTH), fixed),
        ],
        out_specs=pl.BlockSpec((TQ, ATTN_WIDTH), qrow),
        out_shape=jax.ShapeDtypeStruct((n, ATTN_WIDTH), jnp.bfloat16),
        scratch_shapes=[
            pltpu.VMEM((nkc, TK, TQ), jnp.int32),
            pltpu.VMEM((nkc, TK, TQ), jnp.int16),
            pltpu.VMEM((nkc, TK, TQ), jnp.int16),
            pltpu.VMEM((KV_LATENT, N_HEADS * TQ), jnp.bfloat16),
            pltpu.VMEM((KVT_ROWS, N_HEADS * TQ), jnp.float32),
            pltpu.VMEM((1, N_HEADS * TQ), jnp.float32),
        ],
        compiler_params=_cparams(("arbitrary", "arbitrary")),
        name="dsa_attention",
    )(q, qi, wi_t, ckv, ckv_t, ki4, wukt_bd, wuv_bd)


def _mix_kernel(x_ref, oa_ref, ap_ref, halo_ref, sh1_ref, sc1_ref, g1_ref, n1g_ref,
                wgate_ref, wpool_ref, pscale_ref, waup_ref, wbup_ref, wout_ref,
                sh2_ref, sc2_ref, n2g_ref, wr_ref, br_ref,
                x1_ref, h2_ref, lg_ref, *, tiles_per_batch):
    i = pl.program_id(0)
    t_in_b = i % tiles_per_batch
    x = x_ref[...]
    d = x.shape[1]
    h = _rms(x, n1g_ref[...]) * (1.0 + sc1_ref[0]) + sh1_ref[0]
    gates = _dot(_bf(h), wgate_ref[...])

    pos1 = (t_in_b * TM_MIX + 1 + lax.broadcasted_iota(jnp.int32, (TM_MIX, 1), 0)).astype(jnp.float32)
    halo = jnp.where(t_in_b == 0, 0.0, halo_ref[...])
    zs = []
    for g, w in enumerate(POOL_WINDOWS):
        sl = slice(g * POOL_GROUP_DIM, (g + 1) * POOL_GROUP_DIM)
        a = ap_ref[:, sl]
        ext = jnp.concatenate([halo[:, sl], a], axis=0)
        span = 1
        while span < w:
            ext = ext + pltpu.roll(ext, span, 0)
            span *= 2
        mean = ext[POOL_HALO:] / jnp.minimum(pos1, float(w))
        zs.append(_dot(_bf(mean - a), wpool_ref[g]))
    o_b = jnp.concatenate(zs, axis=1) * pscale_ref[...]

    merged = (jax.nn.sigmoid(gates[:, :d]) * _dot(oa_ref[...], waup_ref[...])
              + jax.nn.sigmoid(gates[:, d:]) * _dot(_bf(o_b), wbup_ref[...]))
    x1 = x + g1_ref[0] * _dot(_bf(merged), wout_ref[...])
    x1_ref[...] = x1
    h2 = _rms(x1, n2g_ref[...]) * (1.0 + sc2_ref[0]) + sh2_ref[0]
    h2_ref[...] = h2
    lg_ref[...] = _dot(_bf(h2), wr_ref[...]) + br_ref[...]


def _mix_call(xf, o_a, apool, sh1, sc1, g1, n1g, wgate, wpool, pscale, waup, wbup, wout,
              sh2, sc2, n2g, wr, br, seq):
    n, d = xf.shape
    tpb = seq // TM_MIX
    row = lambda i: (i, 0)
    per_b = lambda i: (i // tpb, 0, 0)
    fixed = lambda i: (0, 0)
    fixed3 = lambda i: (0, 0, 0)
    halo_idx = lambda i: (jnp.maximum(i * (TM_MIX // POOL_HALO) - 1, 0), 0)
    mod = pl.BlockSpec((1, 1, d), per_b)
    return pl.pallas_call(
        functools.partial(_mix_kernel, tiles_per_batch=tpb),
        grid=(n // TM_MIX,),
        in_specs=[
            pl.BlockSpec((TM_MIX, d), row),
            pl.BlockSpec((TM_MIX, ATTN_WIDTH), row),
            pl.BlockSpec((TM_MIX, POOL_WIDTH), row),
            pl.BlockSpec((POOL_HALO, POOL_WIDTH), halo_idx),
            mod, mod, mod,
            pl.BlockSpec((1, d), fixed),
            pl.BlockSpec((d, 2 * d), fixed),
            pl.BlockSpec((len(POOL_WINDOWS), POOL_GROUP_DIM, POOL_GROUP_DIM), fixed3),
            pl.BlockSpec((1, POOL_WIDTH), fixed),
            pl.BlockSpec((ATTN_WIDTH, d), fixed),
            pl.BlockSpec((POOL_WIDTH, d), fixed),
            pl.BlockSpec((d, d), fixed),
            mod, mod,
            pl.BlockSpec((1, d), fixed),
            pl.BlockSpec((d, LANES), fixed),
            pl.BlockSpec((1, LANES), fixed),
        ],
        out_specs=[pl.BlockSpec((TM_MIX, d), row), pl.BlockSpec((TM_MIX, d), row),
                   pl.BlockSpec((TM_MIX, LANES), row)],
        out_shape=[jax.ShapeDtypeStruct((n, d), jnp.float32), jax.ShapeDtypeStruct((n, d), jnp.float32),
                   jax.ShapeDtypeStruct((n, LANES), jnp.float32)],
        compiler_params=_cparams(("arbitrary",)),
        name="mix_merge",
    )(xf, o_a, apool, apool, sh1, sc1, g1, n1g, wgate, wpool, pscale, waup, wbup, wout,
      sh2, sc2, n2g, wr, br)


def _route_kernel(lg_ref, idx_ref, wt_ref, rank_ref, cnt_ref, carry):
    @pl.when(pl.program_id(0) == 0)
    def _():
        carry[...] = jnp.zeros(carry.shape, jnp.float32)

    work = lg_ref[...]
    lane = lax.broadcasted_iota(jnp.int32, work.shape, 1).astype(jnp.float32)
    vals, idxs = [], []
    for _ in range(TOP_K):
        m = jnp.max(work, axis=1, keepdims=True)
        idx = jnp.min(jnp.where(work == m, lane, float(LANES)), axis=1, keepdims=True)
        vals.append(m)
        idxs.append(idx)
        work = jnp.where(lane == idx, -jnp.inf, work)
    exps = [jnp.exp(v - vals[0]) for v in vals]
    denom = exps[0] + exps[1] + exps[2] + exps[3]

    onehot = jnp.zeros(work.shape, jnp.float32)
    for idx in idxs:
        onehot = onehot + jnp.where(lane == idx, 1.0, 0.0)
    before = _dot(_strict_lower(work.shape[0]), _bf(onehot)) + carry[...]

    idx_out = jnp.zeros(work.shape, jnp.int32)
    wt_out = jnp.zeros(work.shape, jnp.float32)
    rank_out = jnp.zeros(work.shape, jnp.int32)
    for k in range(TOP_K):
        rank_k = jnp.sum(jnp.where(lane == idxs[k], before, 0.0), axis=1, keepdims=True)
        idx_out = jnp.where(lane == k, idxs[k].astype(jnp.int32), idx_out)
        wt_out = jnp.where(lane == k, exps[k] / denom, wt_out)
        rank_out = jnp.where(lane == k, rank_k.astype(jnp.int32), rank_out)
    idx_ref[...] = idx_out
    wt_ref[...] = wt_out
    rank_ref[...] = rank_out
    carry[...] = carry[...] + jnp.sum(onehot, axis=0, keepdims=True)
    cnt_ref[...] = carry[...]


def _route_call(logits):
    n = logits.shape[0]
    row = lambda i: (i, 0)
    return pl.pallas_call(
        _route_kernel,
        grid=(n // TM_ROUTE,),
        in_specs=[pl.BlockSpec((TM_ROUTE, LANES), row)],
        out_specs=[pl.BlockSpec((TM_ROUTE, LANES), row)] * 3 + [pl.BlockSpec((1, LANES), lambda i: (0, 0))],
        out_shape=[jax.ShapeDtypeStruct((n, LANES), jnp.int32), jax.ShapeDtypeStruct((n, LANES), jnp.float32),
                   jax.ShapeDtypeStruct((n, LANES), jnp.int32), jax.ShapeDtypeStruct((1, LANES), jnp.float32)],
        scratch_shapes=[pltpu.VMEM((1, LANES), jnp.float32)],
        compiler_params=_cparams(("arbitrary",)),
        name="route",
    )(logits)


def _scatter_kernel(endpad_ref, dest_ref, h2_ref, xs_ref, zero_buf, sem_zero, sem_rows):
    @pl.when(pl.program_id(0) == 0)
    def _():
        zero_buf[...] = jnp.zeros(zero_buf.shape, zero_buf.dtype)

        def zero_block(start):
            return pltpu.make_async_copy(
                zero_buf, xs_ref.at[pl.ds(pl.multiple_of(start, MOE_BLOCK), MOE_BLOCK), :], sem_zero)

        def last_block_of(e):
            return jnp.maximum(endpad_ref[e] - MOE_BLOCK, 0)

        def start_zero(e, c):
            zero_block(last_block_of(e)).start()
            return c

        def wait_zero(e, c):
            zero_block(last_block_of(e)).wait()
            return c

        def start_tail(b, c):
            zero_block(b * MOE_BLOCK).start()
            return c

        def wait_tail(b, c):
            zero_block(b * MOE_BLOCK).wait()
            return c

        n_used = endpad_ref[N_EXPERTS - 1] // MOE_BLOCK
        n_blocks = xs_ref.shape[0] // MOE_BLOCK
        lax.fori_loop(0, N_EXPERTS, start_zero, 0)
        lax.fori_loop(n_used, n_blocks, start_tail, 0)
        lax.fori_loop(0, N_EXPERTS, wait_zero, 0)
        lax.fori_loop(n_used, n_blocks, wait_tail, 0)

    def row_copy(r, k):
        return pltpu.make_async_copy(h2_ref.at[pl.ds(r, 1), :],
                                     xs_ref.at[pl.ds(dest_ref[r * TOP_K + k], 1), :], sem_rows)

    def start_row(r, c):
        for k in range(TOP_K):
            row_copy(r, k).start()
        return c

    def wait_row(r, c):
        for k in range(TOP_K):
            row_copy(r, k).wait()
        return c

    lax.fori_loop(0, TM_SCATTER, start_row, 0)
    lax.fori_loop(0, TM_SCATTER, wait_row, 0)


def _scatter_call(h2, dest_flat, end_pad, n_rows):
    n, d = h2.shape
    grid_spec = pltpu.PrefetchScalarGridSpec(
        num_scalar_prefetch=1,
        grid=(n // TM_SCATTER,),
        in_specs=[
            pl.BlockSpec((TM_SCATTER * TOP_K,), lambda i, ep: (i,), memory_space=pltpu.SMEM),
            pl.BlockSpec((TM_SCATTER, d), lambda i, ep: (i, 0)),
        ],
        out_specs=pl.BlockSpec(memory_space=pl.ANY),
        scratch_shapes=[pltpu.VMEM((MOE_BLOCK, d), h2.dtype),
                        pltpu.SemaphoreType.DMA(()), pltpu.SemaphoreType.DMA(())],
    )
    return pl.pallas_call(
        _scatter_kernel,
        grid_spec=grid_spec,
        out_shape=jax.ShapeDtypeStruct((n_rows, d), h2.dtype),
        compiler_params=_cparams(("arbitrary",)),
        name="moe_scatter",
    )(end_pad, dest_flat, h2)


def _expert_kernel(blk_e_ref, blk_src_ref, blk_on_ref, xs_ref, wgu_ref, bgu_ref, wdn_ref, bdn_ref, ys_ref):
    on = blk_on_ref[pl.program_id(0)] == 1

    @pl.when(jnp.logical_not(on))
    def _():
        ys_ref[...] = jnp.zeros(ys_ref.shape, ys_ref.dtype)

    @pl.when(on)
    def _():
        f = wdn_ref.shape[1]
        gu = _dot(_bf(xs_ref[...]), wgu_ref[0]) + bgu_ref[0]
        gt = jnp.minimum(gu[:, :f], SWIGLU_LIMIT)
        up = jnp.clip(gu[:, f:], -SWIGLU_LIMIT, SWIGLU_LIMIT)
        act = gt * jax.nn.sigmoid(SWIGLU_ALPHA * gt) * (up + 1.0)
        ys_ref[...] = _dot(_bf(act), wdn_ref[0]) + bdn_ref[0]


def _expert_call(xs, blk_e, blk_src, blk_on, wgu, bgu, wdn, bdn):
    n_rows, d = xs.shape
    n_blocks = n_rows // MOE_BLOCK
    f2 = wgu.shape[2]
    f = wdn.shape[1]
    src = lambda i, be, bs, bo: (bs[i], 0)
    by_e = lambda i, be, bs, bo: (be[i], 0, 0)
    grid_spec = pltpu.PrefetchScalarGridSpec(
        num_scalar_prefetch=3,
        grid=(n_blocks,),
        in_specs=[
            pl.BlockSpec((MOE_BLOCK, d), src),
            pl.BlockSpec((1, d, f2), by_e),
            pl.BlockSpec((1, 1, f2), by_e),
            pl.BlockSpec((1, f, d), by_e),
            pl.BlockSpec((1, 1, d), by_e),
        ],
        out_specs=pl.BlockSpec((MOE_BLOCK, d), lambda i, be, bs, bo: (i, 0)),
    )
    return pl.pallas_call(
        _expert_kernel,
        grid_spec=grid_spec,
        out_shape=jax.ShapeDtypeStruct((n_rows, d), jnp.float32),
        compiler_params=_cparams(("arbitrary",)),
        name="moe_experts",
    )(blk_e, blk_src, blk_on, xs, wgu, bgu, wdn, bdn)


def _combine_kernel(dest_ref, x1_ref, wt_ref, g2_ref, fg_ref, ys_ref, o_ref, buf, sem, *, final_norm):
    def row_copy(r, k):
        return pltpu.make_async_copy(ys_ref.at[pl.ds(dest_ref[r * TOP_K + k], 1), :],
                                     buf.at[k, pl.ds(r, 1), :], sem)

    def start_row(r, c):
        for k in range(TOP_K):
            row_copy(r, k).start()
        return c

    def wait_row(r, c):
        for k in range(TOP_K):
            row_copy(r, k).wait()
        return c

    lax.fori_loop(0, TM_COMBINE, start_row, 0)
    lax.fori_loop(0, TM_COMBINE, wait_row, 0)

    wt = wt_ref[...]
    y = wt[:, 0:1] * buf[0]
    for k in range(1, TOP_K):
        y = y + wt[:, k:k + 1] * buf[k]
    out = x1_ref[...] + g2_ref[0] * y
    if final_norm:
        out = _rms(out, fg_ref[...])
    o_ref[...] = out


def _combine_call(x1, wts, g2, final_g, ys, dest_flat, seq, final_norm):
    n, d = x1.shape
    tpb = seq // TM_COMBINE
    row = lambda i: (i, 0)
    return pl.pallas_call(
        functools.partial(_combine_kernel, final_norm=final_norm),
        grid=(n // TM_COMBINE,),
        in_specs=[
            pl.BlockSpec((TM_COMBINE * TOP_K,), lambda i: (i,), memory_space=pltpu.SMEM),
            pl.BlockSpec((TM_COMBINE, d), row),
            pl.BlockSpec((TM_COMBINE, LANES), row),
            pl.BlockSpec((1, 1, d), lambda i: (i // tpb, 0, 0)),
            pl.BlockSpec((1, d), lambda i: (0, 0)),
            pl.BlockSpec(memory_space=pl.ANY),
        ],
        out_specs=pl.BlockSpec((TM_COMBINE, d), row),
        out_shape=jax.ShapeDtypeStruct((n, d), jnp.float32),
        scratch_shapes=[pltpu.VMEM((TOP_K, TM_COMBINE, d), jnp.float32), pltpu.SemaphoreType.DMA(())],
        compiler_params=_cparams(("arbitrary",)),
        name="moe_combine",
    )(dest_flat, x1, wts, g2, final_g, ys)


def _regroup_w_in(w_in):
    o_q, o_kv = 0, ATTN_WIDTH
    o_qi = o_kv + KV_LATENT
    o_ki = o_qi + IDX_HEADS * IDX_DIM
    o_wi = o_ki + IDX_DIM
    o_pool = o_wi + IDX_HEADS
    o_ga = o_pool + POOL_WIDTH
    ki = w_in[..., o_ki:o_wi]
    w_r = jnp.concatenate([w_in[..., o_q:o_ki], ki, ki, ki, ki, w_in[..., o_pool:o_ga]], axis=-1)
    w_wi_t = jnp.pad(jnp.swapaxes(w_in[..., o_wi:o_pool], 1, 2), ((0, 0), (0, SUBLANES - IDX_HEADS), (0, 0)))
    return _bf(w_r), _bf(w_wi_t), _bf(w_in[..., o_ga:])


def _block_diag_uk_t(w_uk):
    depth = w_uk.shape[0]
    eye = jnp.eye(N_HEADS, dtype=w_uk.dtype)
    t = jnp.einsum('zlhd,hg->zglhd', w_uk, eye)
    return _bf(t.reshape(depth, N_HEADS * KV_LATENT, N_HEADS * HEAD_DIM))


def _block_diag_uv(w_uv):
    depth = w_uv.shape[0]
    eye = jnp.eye(N_HEADS, dtype=w_uv.dtype)
    t = jnp.einsum('zlhd,hg->zhlgd', w_uv, eye)
    return _bf(t.reshape(depth, N_HEADS * KV_LATENT, N_HEADS * HEAD_DIM))


def kernel(x, c, norm1_g, norm2_g, w_ada, b_ada, w_in, kv_norm_g, w_uk, w_uv, w_pool, pool_scale, w_a_up, w_b_up, w_out, w_router, b_router, w_gu, b_gu, w_down, b_down, final_g):
    bsz, seq, d = x.shape
    depth = w_in.shape[0]
    n = bsz * seq
    assert TM_PROJ == TK and seq % TK == 0 and n % MOE_BLOCK == 0
    assert seq // CNT_ROWS <= 256

    ada = _ada_call(c, w_ada, b_ada).reshape(depth, bsz, 6, 1, d)
    w_r, w_wi_t, w_gate = _regroup_w_in(w_in)
    wukt_bd = _block_diag_uk_t(w_uk)
    wuv_bd = _block_diag_uv(w_uv)
    w_pool_b, w_a_up_b, w_b_up_b, w_out_b = _bf(w_pool), _bf(w_a_up), _bf(w_b_up), _bf(w_out)
    w_router_p = _bf(jnp.pad(w_router, ((0, 0), (0, 0), (0, LANES - N_EXPERTS))))
    b_router_p = jnp.pad(b_router, ((0, 0), (0, LANES - N_EXPERTS)), constant_values=NEG_BIG)
    w_gu_b, w_down_b = _bf(w_gu), _bf(w_down)

    n_asg = n * TOP_K
    n_blocks = -(-n_asg // MOE_BLOCK) + N_EXPERTS
    n_rows = n_blocks * MOE_BLOCK

    xf = x.reshape(n, d)
    for l in range(depth):
        sh1, sc1, g1, sh2, sc2, g2 = [ada[l, :, j] for j in range(6)]
        q, ckv, ckv_t, qi, ki4, wi_t, apool = _proj_call(
            xf, sh1, sc1, norm1_g[l][None], w_r[l], w_wi_t[l], kv_norm_g[l][None], seq)
        o_a = _attn_call(q, qi, wi_t, ckv, ckv_t, ki4, wukt_bd[l], wuv_bd[l], bsz, seq)
        x1, h2, logits = _mix_call(xf, o_a, apool, sh1, sc1, g1, norm1_g[l][None], w_gate[l], w_pool_b[l],
                                   pool_scale[l][None], w_a_up_b[l], w_b_up_b[l], w_out_b[l],
                                   sh2, sc2, norm2_g[l][None], w_router_p[l], b_router_p[l][None], seq)
        idx_l, wts, rank_l, counts = _route_call(logits)

        sizes = counts[0, :N_EXPERTS].astype(jnp.int32)
        padded = ((sizes + MOE_BLOCK - 1) // MOE_BLOCK) * MOE_BLOCK
        end_pad = jnp.cumsum(padded)
        start_pad = end_pad - padded
        dest = start_pad[idx_l[:, :TOP_K]] + rank_l[:, :TOP_K]
        dest_flat = dest.reshape(-1).astype(jnp.int32)
        blk_start = jnp.arange(n_blocks, dtype=jnp.int32) * MOE_BLOCK
        blk_e = jnp.minimum(jnp.searchsorted(end_pad, blk_start, side='right'), N_EXPERTS - 1).astype(jnp.int32)
        n_used = end_pad[-1] // MOE_BLOCK
        blk_idx = jnp.arange(n_blocks, dtype=jnp.int32)
        blk_on = (blk_idx < n_used).astype(jnp.int32)
        blk_src = jnp.minimum(blk_idx, n_used - 1).astype(jnp.int32)
        blk_e = jnp.where(blk_on == 1, blk_e, blk_e[n_used - 1])

        xs = _scatter_call(h2, dest_flat, end_pad.astype(jnp.int32), n_rows)
        ys = _expert_call(xs, blk_e, blk_src, blk_on, w_gu_b[l], b_gu[l][:, None, :], w_down_b[l], b_down[l][:, None, :])
        xf = _combine_call(x1, wts, g2, final_g[None], ys, dest_flat, seq, final_norm=(l == depth - 1))
    return xf.reshape(bsz, seq, d)
```

```python
import functools

import jax
import jax.numpy as jnp
from jax import lax
from jax.experimental import pallas as pl
from jax.experimental.pallas import tpu as pltpu

N_HEADS = 8
HEAD_DIM = 64
ATTN_WIDTH = N_HEADS * HEAD_DIM
KV_LATENT = 128
IDX_HEADS = 4
IDX_DIM = 32
TOPK_MAX = 256
ATTN_SCALE = HEAD_DIM ** -0.5
IDX_W_SCALE = (IDX_HEADS ** -0.5) * (IDX_DIM ** -0.5)
POOL_WINDOWS = (2, 4, 8, 16)
POOL_GROUP_DIM = 128
POOL_WIDTH = len(POOL_WINDOWS) * POOL_GROUP_DIM
N_EXPERTS = 32
TOP_K = 4
MOE_BLOCK = 512
SWIGLU_LIMIT = 7.0
SWIGLU_ALPHA = 1.702
RMS_EPS = 1e-6

LANES = 128
SUBLANES = 8
POOL_HALO = 16
NEG_BIG = -1e30
INT_MIN = -2 ** 31
HALF_BITS = 16
HALF_OFFSET = 2 ** (HALF_BITS - 1)

TM_PROJ = 512
TQ = 256
TK = 512
GROUP_HEADS = 8
CNT_ROWS = 64
ONES_ROWS = 16
KVT_ROWS = KV_LATENT + ONES_ROWS
LOG2E = 1.4426950408889634
TM_MIX = 256
TM_ROUTE = 512
TM_SCATTER = 256
TM_COMBINE = 256
VMEM_LIMIT = 56 * 1024 * 1024


def _cparams(sem):
    return pltpu.CompilerParams(dimension_semantics=sem, vmem_limit_bytes=VMEM_LIMIT)


def _rms(xf, g):
    return xf * lax.rsqrt(jnp.mean(xf * xf, axis=-1, keepdims=True) + RMS_EPS) * g


def _bf(v):
    return v.astype(jnp.bfloat16)


def _dot(a, b):
    return jnp.dot(a, b, preferred_element_type=jnp.float32)


def _dot_nt(a, b):
    return lax.dot_general(a, b, (((1,), (1,)), ((), ())), preferred_element_type=jnp.float32)


def _strict_lower(n):
    return _bf(jnp.where(lax.broadcasted_iota(jnp.int32, (n, n), 1)
                         < lax.broadcasted_iota(jnp.int32, (n, n), 0), 1.0, 0.0))


def _ada_kernel(c_ref, w_ref, b_ref, o_ref):
    cf = c_ref[...]
    cond = cf * jax.nn.sigmoid(cf)
    o_ref[0] = _dot(_bf(cond), _bf(w_ref[0])) + b_ref[0]


def _ada_call(c, w_ada, b_ada):
    depth, d, n6 = w_ada.shape
    bsz = c.shape[0]
    tn = 1024
    return pl.pallas_call(
        _ada_kernel,
        grid=(depth, n6 // tn),
        in_specs=[
            pl.BlockSpec((bsz, d), lambda l, j: (0, 0)),
            pl.BlockSpec((1, d, tn), lambda l, j: (l, 0, j)),
            pl.BlockSpec((1, 1, tn), lambda l, j: (l, 0, j)),
        ],
        out_specs=pl.BlockSpec((1, bsz, tn), lambda l, j: (l, 0, j)),
        out_shape=jax.ShapeDtypeStruct((depth, bsz, n6), jnp.float32),
        compiler_params=_cparams(("arbitrary", "arbitrary")),
        name="ada",
    )(c, w_ada, b_ada.reshape(depth, 1, n6))


PROJ_COLS = ATTN_WIDTH + 3 * LANES + POOL_WIDTH


def _proj_kernel(x_ref, sh_ref, sc_ref, g_ref, w_ref, wwi_ref, kvg_ref,
                 q_ref, ckv_ref, ckvt_ref, qi_ref, ki_ref, wit_ref, ap_ref):
    h = _bf(_rms(x_ref[...], g_ref[...]) * (1.0 + sc_ref[0]) + sh_ref[0])
    p = _dot(h, w_ref[...])
    o = 0
    q_ref[...] = _bf(p[:, o:o + ATTN_WIDTH]); o += ATTN_WIDTH
    ckv = _bf(_rms(p[:, o:o + KV_LATENT], kvg_ref[...])); o += KV_LATENT
    ckv_ref[...] = ckv
    eye = _bf(jnp.where(lax.broadcasted_iota(jnp.int32, (KV_LATENT, KV_LATENT), 0)
                        == lax.broadcasted_iota(jnp.int32, (KV_LATENT, KV_LATENT), 1), 1.0, 0.0))
    ckvt_ref[0, :KV_LATENT, :] = _bf(_dot_nt(eye, ckv))
    ckvt_ref[0, KV_LATENT:, :] = jnp.ones((ONES_ROWS, TM_PROJ), jnp.bfloat16)
    qi_ref[...] = _bf(p[:, o:o + LANES]); o += LANES
    ki_ref[...] = _bf(p[:, o:o + LANES]); o += LANES
    ap_ref[...] = p[:, o:o + POOL_WIDTH]
    wit_ref[...] = _dot_nt(wwi_ref[...], h)


def _proj_call(xf, sh1, sc1, g1n, w_r, w_wi_t, kvg, seq):
    n, d = xf.shape
    tpb = seq // TM_PROJ
    row = lambda i: (i, 0)
    per_b = lambda i: (i // tpb, 0, 0)
    fixed = lambda i: (0, 0)
    return pl.pallas_call(
        _proj_kernel,
        grid=(n // TM_PROJ,),
        in_specs=[
            pl.BlockSpec((TM_PROJ, d), row),
            pl.BlockSpec((1, 1, d), per_b),
            pl.BlockSpec((1, 1, d), per_b),
            pl.BlockSpec((1, d), fixed),
            pl.BlockSpec((d, PROJ_COLS), fixed),
            pl.BlockSpec((SUBLANES, d), fixed),
            pl.BlockSpec((1, KV_LATENT), fixed),
        ],
        out_specs=[
            pl.BlockSpec((TM_PROJ, ATTN_WIDTH), row),
            pl.BlockSpec((TM_PROJ, KV_LATENT), row),
            pl.BlockSpec((1, KVT_ROWS, TM_PROJ), lambda i: (i, 0, 0)),
            pl.BlockSpec((TM_PROJ, LANES), row),
            pl.BlockSpec((TM_PROJ, LANES), row),
            pl.BlockSpec((SUBLANES, TM_PROJ), lambda i: (0, i)),
            pl.BlockSpec((TM_PROJ, POOL_WIDTH), row),
        ],
        out_shape=[
            jax.ShapeDtypeStruct((n, ATTN_WIDTH), jnp.bfloat16),
            jax.ShapeDtypeStruct((n, KV_LATENT), jnp.bfloat16),
            jax.ShapeDtypeStruct((n // TM_PROJ, KVT_ROWS, TM_PROJ), jnp.bfloat16),
            jax.ShapeDtypeStruct((n, LANES), jnp.bfloat16),
            jax.ShapeDtypeStruct((n, LANES), jnp.bfloat16),
            jax.ShapeDtypeStruct((SUBLANES, n), jnp.float32),
            jax.ShapeDtypeStruct((n, POOL_WIDTH), jnp.float32),
        ],
        compiler_params=_cparams(("arbitrary",)),
        name="in_proj",
    )(xf, sh1, sc1, g1n, w_r, w_wi_t, kvg)


def _attn_kernel(q_ref, qi_ref, wit_ref, ckv_ref, ckvt_ref, ki_ref, wukt_ref, wuv_ref, o_ref,
                 key_buf, hi_buf, lo_buf, qs_buf, acc_buf, m_buf, *, k_sel):
    qb = pl.program_id(1)
    q_start = qb * TQ
    n_kc = (q_start + TQ + TK - 1) // TK

    krow = lax.broadcasted_iota(jnp.int32, (TK, TQ), 0)
    qcol = q_start + lax.broadcasted_iota(jnp.int32, (TK, TQ), 1)

    qi = qi_ref[...]
    lane = lax.broadcasted_iota(jnp.int32, (TQ, LANES), 1)
    q4 = jnp.concatenate(
        [jnp.where((lane >= h * IDX_DIM) & (lane < (h + 1) * IDX_DIM), qi, jnp.zeros_like(qi))
         for h in range(IDX_HEADS)], axis=0)
    wit = wit_ref[...] * IDX_W_SCALE
    w_rows = [wit[h:h + 1, :] for h in range(IDX_HEADS)]

    def score_chunk(kc, carry):
        k0 = pl.multiple_of(kc * TK, TK)
        raw = _dot_nt(ki_ref[pl.ds(k0, TK), :], q4)
        score = w_rows[0] * jnp.maximum(raw[:, 0:TQ], 0.0)
        for h in range(1, IDX_HEADS):
            score = score + w_rows[h] * jnp.maximum(raw[:, h * TQ:(h + 1) * TQ], 0.0)
        score = score + 0.0
        bits = pltpu.bitcast(score, jnp.int32)
        key = jnp.where(bits < 0, bits ^ jnp.int32(0x7FFFFFFF), bits)
        key = jnp.where(krow + k0 <= qcol, key, jnp.int32(INT_MIN))
        key_buf[kc] = key
        hi_buf[kc] = (key >> HALF_BITS).astype(jnp.int16)
        return carry

    lax.fori_loop(0, n_kc, score_chunk, 0)

    def count16(plane, cand, strict):
        def body(kc, acc):
            for j in range(TK // CNT_ROWS):
                pj = plane[kc, j * CNT_ROWS:(j + 1) * CNT_ROWS, :]
                hit = (pj > cand) if strict else (pj >= cand)
                acc = acc + jnp.where(hit, jnp.bfloat16(1.0), jnp.bfloat16(0.0))
            return acc

        acc = lax.fori_loop(0, n_kc, body, jnp.zeros((CNT_ROWS, TQ), jnp.bfloat16))
        return jnp.sum(acc.astype(jnp.float32), axis=0, keepdims=True)

    def to_plane(t_u):
        return (t_u - HALF_OFFSET).astype(jnp.int16)

    def search16(plane, need):
        def bit_step(i, t_u):
            cand_u = t_u | lax.shift_left(jnp.int32(1), HALF_BITS - 1 - i)
            cnt = count16(plane, to_plane(cand_u), False)
            return jnp.where(cnt >= need, cand_u, t_u)

        return lax.fori_loop(0, HALF_BITS, bit_step, jnp.zeros((1, TQ), jnp.int32))

    hi_u = search16(hi_buf, float(k_sel))
    thr_hi = to_plane(hi_u)
    need_lo = float(k_sel) - count16(hi_buf, thr_hi, True)

    def low_plane_chunk(kc, carry):
        lo = ((key_buf[kc] & jnp.int32(HALF_OFFSET * 2 - 1)) - HALF_OFFSET).astype(jnp.int16)
        lo_buf[kc] = jnp.where(hi_buf[kc] == thr_hi, lo, jnp.int16(-HALF_OFFSET))
        return carry

    lax.fori_loop(0, n_kc, low_plane_chunk, 0)
    lo_u = search16(lo_buf, need_lo)
    thr = lax.shift_left(hi_u - HALF_OFFSET, HALF_BITS) | lo_u
    n_tie_take = need_lo - count16(lo_buf, to_plane(lo_u), True)

    qlat_t = _dot_nt(wukt_ref[...], q_ref[...]) * (ATTN_SCALE * LOG2E)
    for h in range(N_HEADS):
        qs_buf[:, h * TQ:(h + 1) * TQ] = _bf(qlat_t[h * KV_LATENT:(h + 1) * KV_LATENT, :])
    lower = _strict_lower(LANES)
    m_buf[...] = jnp.full(m_buf.shape, -3e38, jnp.float32)
    acc_buf[...] = jnp.zeros(acc_buf.shape, jnp.float32)

    def attend_chunk(kc, tie_seen):
        k0 = pl.multiple_of(kc * TK, TK)
        key = key_buf[kc]
        eq = key == thr
        eq_f = jnp.where(eq, 1.0, 0.0)
        eq_b = _bf(eq_f)
        ranks = []
        for j in range(TK // LANES):
            rows = slice(j * LANES, (j + 1) * LANES)
            ranks.append(_dot(lower, eq_b[rows]) + tie_seen)
            tie_seen = tie_seen + jnp.sum(eq_f[rows], axis=0, keepdims=True)
        tie_rank = jnp.concatenate(ranks, axis=0)
        sel = ((key > thr) | (eq & (tie_rank < n_tie_take))) & (krow + k0 <= qcol)
        bias = jnp.where(sel, 0.0, NEG_BIG)
        kv = ckv_ref[pl.ds(k0, TK), :]
        kv_t = ckvt_ref[kc]
        for g in range(N_HEADS // GROUP_HEADS):
            gcols = slice(g * GROUP_HEADS * TQ, (g + 1) * GROUP_HEADS * TQ)
            logits = _dot(kv, qs_buf[:, gcols])
            ps, alphas = [], []
            for hh in range(GROUP_HEADS):
                cols = slice((g * GROUP_HEADS + hh) * TQ, (g * GROUP_HEADS + hh + 1) * TQ)
                lg = logits[:, hh * TQ:(hh + 1) * TQ] + bias
                m_old = m_buf[:, cols]
                m_new = jnp.maximum(m_old, jnp.max(lg, axis=0, keepdims=True))
                m_buf[:, cols] = m_new
                alphas.append(jnp.exp2(m_old - m_new))
                ps.append(_bf(jnp.exp2(lg - m_new)))
            acc_buf[:, gcols] = (acc_buf[:, gcols] * jnp.concatenate(alphas, axis=1)
                                 + _dot(kv_t, jnp.concatenate(ps, axis=1)))
        return tie_seen

    lax.fori_loop(0, n_kc, attend_chunk, jnp.zeros((1, TQ), jnp.float32))

    o_lat_t = acc_buf[:KV_LATENT, :] / acc_buf[KV_LATENT:KV_LATENT + 1, :]
    stacked = jnp.concatenate(
        [o_lat_t[:, h * TQ:(h + 1) * TQ] for h in range(N_HEADS)], axis=0)
    o_ref[...] = _bf(_dot(_bf(stacked.T), wuv_ref[...]))


def _attn_call(q, qi, wi_t, ckv, ckv_t, ki4, wukt_bd, wuv_bd, bsz, seq):
    n = q.shape[0]
    nq = seq // TQ
    nkc = seq // TK
    k_sel = min(TOPK_MAX, seq // 4)
    qrow = lambda b, i: (b * nq + i, 0)
    per_b = lambda b, i: (b, 0)
    fixed = lambda b, i: (0, 0)
    return pl.pallas_call(
        functools.partial(_attn_kernel, k_sel=k_sel),
        grid=(bsz, nq),
        in_specs=[
            pl.BlockSpec((TQ, ATTN_WIDTH), qrow),
            pl.BlockSpec((TQ, LANES), qrow),
            pl.BlockSpec((SUBLANES, TQ), lambda b, i: (0, b * nq + i)),
            pl.BlockSpec((seq, KV_LATENT), per_b),
            pl.BlockSpec((nkc, KVT_ROWS, TK), lambda b, i: (b, 0, 0)),
            pl.BlockSpec((seq, LANES), per_b),
            pl.BlockSpec((N_HEADS * KV_LATENT, ATTN_WIDTH), fixed),
            pl.BlockSpec((N_HEADS * KV_LATENT, ATTN_WIDTH), fixed),
        ],
        out_specs=pl.BlockSpec((TQ, ATTN_WIDTH), qrow),
        out_shape=jax.ShapeDtypeStruct((n, ATTN_WIDTH), jnp.bfloat16),
        scratch_shapes=[
            pltpu.VMEM((nkc, TK, TQ), jnp.int32),
            pltpu.VMEM((nkc, TK, TQ), jnp.int16),
            pltpu.VMEM((nkc, TK, TQ), jnp.int16),
            pltpu.VMEM((KV_LATENT, N_HEADS * TQ), jnp.bfloat16),
            pltpu.VMEM((KVT_ROWS, N_HEADS * TQ), jnp.float32),
            pltpu.VMEM((1, N_HEADS * TQ), jnp.float32),
        ],
        compiler_params=_cparams(("arbitrary", "arbitrary")),
        name="dsa_attention",
    )(q, qi, wi_t, ckv, ckv_t, ki4, wukt_bd, wuv_bd)


def _mix_kernel(x_ref, oa_ref, ap_ref, halo_ref, sh1_ref, sc1_ref, g1_ref, n1g_ref,
                wgate_ref, wpool_ref, pscale_ref, waup_ref, wbup_ref, wout_ref,
                sh2_ref, sc2_ref, n2g_ref, wr_ref, br_ref,
                x1_ref, h2_ref, lg_ref, *, tiles_per_batch):
    i = pl.program_id(0)
    t_in_b = i % tiles_per_batch
    x = x_ref[...]
    d = x.shape[1]
    h = _rms(x, n1g_ref[...]) * (1.0 + sc1_ref[0]) + sh1_ref[0]
    gates = _dot(_bf(h), wgate_ref[...])

    pos1 = (t_in_b * TM_MIX + 1 + lax.broadcasted_iota(jnp.int32, (TM_MIX, 1), 0)).astype(jnp.float32)
    halo = jnp.where(t_in_b == 0, 0.0, halo_ref[...])
    zs = []
    for g, w in enumerate(POOL_WINDOWS):
        sl = slice(g * POOL_GROUP_DIM, (g + 1) * POOL_GROUP_DIM)
        a = ap_ref[:, sl]
        ext = jnp.concatenate([halo[:, sl], a], axis=0)
        span = 1
        while span < w:
            ext = ext + pltpu.roll(ext, span, 0)
            span *= 2
        mean = ext[POOL_HALO:] / jnp.minimum(pos1, float(w))
        zs.append(_dot(_bf(mean - a), wpool_ref[g]))
    o_b = jnp.concatenate(zs, axis=1) * pscale_ref[...]

    merged = (jax.nn.sigmoid(gates[:, :d]) * _dot(oa_ref[...], waup_ref[...])
              + jax.nn.sigmoid(gates[:, d:]) * _dot(_bf(o_b), wbup_ref[...]))
    x1 = x + g1_ref[0] * _dot(_bf(merged), wout_ref[...])
    x1_ref[...] = x1
    h2 = _rms(x1, n2g_ref[...]) * (1.0 + sc2_ref[0]) + sh2_ref[0]
    h2_ref[...] = h2
    lg_ref[...] = _dot(_bf(h2), wr_ref[...]) + br_ref[...]


def _mix_call(xf, o_a, apool, sh1, sc1, g1, n1g, wgate, wpool, pscale, waup, wbup, wout,
              sh2, sc2, n2g, wr, br, seq):
    n, d = xf.shape
    tpb = seq // TM_MIX
    row = lambda i: (i, 0)
    per_b = lambda i: (i // tpb, 0, 0)
    fixed = lambda i: (0, 0)
    fixed3 = lambda i: (0, 0, 0)
    halo_idx = lambda i: (jnp.maximum(i * (TM_MIX // POOL_HALO) - 1, 0), 0)
    mod = pl.BlockSpec((1, 1, d), per_b)
    return pl.pallas_call(
        functools.partial(_mix_kernel, tiles_per_batch=tpb),
        grid=(n // TM_MIX,),
        in_specs=[
            pl.BlockSpec((TM_MIX, d), row),
            pl.BlockSpec((TM_MIX, ATTN_WIDTH), row),
            pl.BlockSpec((TM_MIX, POOL_WIDTH), row),
            pl.BlockSpec((POOL_HALO, POOL_WIDTH), halo_idx),
            mod, mod, mod,
            pl.BlockSpec((1, d), fixed),
            pl.BlockSpec((d, 2 * d), fixed),
            pl.BlockSpec((len(POOL_WINDOWS), POOL_GROUP_DIM, POOL_GROUP_DIM), fixed3),
            pl.BlockSpec((1, POOL_WIDTH), fixed),
            pl.BlockSpec((ATTN_WIDTH, d), fixed),
            pl.BlockSpec((POOL_WIDTH, d), fixed),
            pl.BlockSpec((d, d), fixed),
            mod, mod,
            pl.BlockSpec((1, d), fixed),
            pl.BlockSpec((d, LANES), fixed),
            pl.BlockSpec((1, LANES), fixed),
        ],
        out_specs=[pl.BlockSpec((TM_MIX, d), row), pl.BlockSpec((TM_MIX, d), row),
                   pl.BlockSpec((TM_MIX, LANES), row)],
        out_shape=[jax.ShapeDtypeStruct((n, d), jnp.float32), jax.ShapeDtypeStruct((n, d), jnp.float32),
                   jax.ShapeDtypeStruct((n, LANES), jnp.float32)],
        compiler_params=_cparams(("arbitrary",)),
        name="mix_merge",
    )(xf, o_a, apool, apool, sh1, sc1, g1, n1g, wgate, wpool, pscale, waup, wbup, wout,
      sh2, sc2, n2g, wr, br)


def _route_kernel(lg_ref, idx_ref, wt_ref, rank_ref, cnt_ref, carry):
    @pl.when(pl.program_id(0) == 0)
    def _():
        carry[...] = jnp.zeros(carry.shape, jnp.float32)

    work = lg_ref[...]
    lane = lax.broadcasted_iota(jnp.int32, work.shape, 1).astype(jnp.float32)
    vals, idxs = [], []
    for _ in range(TOP_K):
        m = jnp.max(work, axis=1, keepdims=True)
        idx = jnp.min(jnp.where(work == m, lane, float(LANES)), axis=1, keepdims=True)
        vals.append(m)
        idxs.append(idx)
        work = jnp.where(lane == idx, -jnp.inf, work)
    exps = [jnp.exp(v - vals[0]) for v in vals]
    denom = exps[0] + exps[1] + exps[2] + exps[3]

    onehot = jnp.zeros(work.shape, jnp.float32)
    for idx in idxs:
        onehot = onehot + jnp.where(lane == idx, 1.0, 0.0)
    before = _dot(_strict_lower(work.shape[0]), _bf(onehot)) + carry[...]

    idx_out = jnp.zeros(work.shape, jnp.int32)
    wt_out = jnp.zeros(work.shape, jnp.float32)
    rank_out = jnp.zeros(work.shape, jnp.int32)
    for k in range(TOP_K):
        rank_k = jnp.sum(jnp.where(lane == idxs[k], before, 0.0), axis=1, keepdims=True)
        idx_out = jnp.where(lane == k, idxs[k].astype(jnp.int32), idx_out)
        wt_out = jnp.where(lane == k, exps[k] / denom, wt_out)
        rank_out = jnp.where(lane == k, rank_k.astype(jnp.int32), rank_out)
    idx_ref[...] = idx_out
    wt_ref[...] = wt_out
    rank_ref[...] = rank_out
    carry[...] = carry[...] + jnp.sum(onehot, axis=0, keepdims=True)
    cnt_ref[...] = carry[...]


def _route_call(logits):
    n = logits.shape[0]
    row = lambda i: (i, 0)
    return pl.pallas_call(
        _route_kernel,
        grid=(n // TM_ROUTE,),
        in_specs=[pl.BlockSpec((TM_ROUTE, LANES), row)],
        out_specs=[pl.BlockSpec((TM_ROUTE, LANES), row)] * 3 + [pl.BlockSpec((1, LANES), lambda i: (0, 0))],
        out_shape=[jax.ShapeDtypeStruct((n, LANES), jnp.int32), jax.ShapeDtypeStruct((n, LANES), jnp.float32),
                   jax.ShapeDtypeStruct((n, LANES), jnp.int32), jax.ShapeDtypeStruct((1, LANES), jnp.float32)],
        scratch_shapes=[pltpu.VMEM((1, LANES), jnp.float32)],
        compiler_params=_cparams(("arbitrary",)),
        name="route",
    )(logits)


def _scatter_kernel(endpad_ref, dest_ref, h2_ref, xs_ref, zero_buf, sem_zero, sem_rows):
    @pl.when(pl.program_id(0) == 0)
    def _():
        zero_buf[...] = jnp.zeros(zero_buf.shape, zero_buf.dtype)

        def zero_block(start):
            return pltpu.make_async_copy(
                zero_buf, xs_ref.at[pl.ds(pl.multiple_of(start, MOE_BLOCK), MOE_BLOCK), :], sem_zero)

        def last_block_of(e):
            return jnp.maximum(endpad_ref[e] - MOE_BLOCK, 0)

        def start_zero(e, c):
            zero_block(last_block_of(e)).start()
            return c

        def wait_zero(e, c):
            zero_block(last_block_of(e)).wait()
            return c

        def start_tail(b, c):
            zero_block(b * MOE_BLOCK).start()
            return c

        def wait_tail(b, c):
            zero_block(b * MOE_BLOCK).wait()
            return c

        n_used = endpad_ref[N_EXPERTS - 1] // MOE_BLOCK
        n_blocks = xs_ref.shape[0] // MOE_BLOCK
        lax.fori_loop(0, N_EXPERTS, start_zero, 0)
        lax.fori_loop(n_used, n_blocks, start_tail, 0)
        lax.fori_loop(0, N_EXPERTS, wait_zero, 0)
        lax.fori_loop(n_used, n_blocks, wait_tail, 0)

    def row_copy(r, k):
        return pltpu.make_async_copy(h2_ref.at[pl.ds(r, 1), :],
                                     xs_ref.at[pl.ds(dest_ref[r * TOP_K + k], 1), :], sem_rows)

    def start_row(r, c):
        for k in range(TOP_K):
            row_copy(r, k).start()
        return c

    def wait_row(r, c):
        for k in range(TOP_K):
            row_copy(r, k).wait()
        return c

    lax.fori_loop(0, TM_SCATTER, start_row, 0)
    lax.fori_loop(0, TM_SCATTER, wait_row, 0)


def _scatter_call(h2, dest_flat, end_pad, n_rows):
    n, d = h2.shape
    grid_spec = pltpu.PrefetchScalarGridSpec(
        num_scalar_prefetch=1,
        grid=(n // TM_SCATTER,),
        in_specs=[
            pl.BlockSpec((TM_SCATTER * TOP_K,), lambda i, ep: (i,), memory_space=pltpu.SMEM),
            pl.BlockSpec((TM_SCATTER, d), lambda i, ep: (i, 0)),
        ],
        out_specs=pl.BlockSpec(memory_space=pl.ANY),
        scratch_shapes=[pltpu.VMEM((MOE_BLOCK, d), h2.dtype),
                        pltpu.SemaphoreType.DMA(()), pltpu.SemaphoreType.DMA(())],
    )
    return pl.pallas_call(
        _scatter_kernel,
        grid_spec=grid_spec,
        out_shape=jax.ShapeDtypeStruct((n_rows, d), h2.dtype),
        compiler_params=_cparams(("arbitrary",)),
        name="moe_scatter",
    )(end_pad, dest_flat, h2)


def _expert_kernel(blk_e_ref, blk_src_ref, blk_on_ref, xs_ref, wgu_ref, bgu_ref, wdn_ref, bdn_ref, ys_ref):
    on = blk_on_ref[pl.program_id(0)] == 1

    @pl.when(jnp.logical_not(on))
    def _():
        ys_ref[...] = jnp.zeros(ys_ref.shape, ys_ref.dtype)

    @pl.when(on)
    def _():
        f = wdn_ref.shape[1]
        gu = _dot(_bf(xs_ref[...]), wgu_ref[0]) + bgu_ref[0]
        gt = jnp.minimum(gu[:, :f], SWIGLU_LIMIT)
        up = jnp.clip(gu[:, f:], -SWIGLU_LIMIT, SWIGLU_LIMIT)
        act = gt * jax.nn.sigmoid(SWIGLU_ALPHA * gt) * (up + 1.0)
        ys_ref[...] = _dot(_bf(act), wdn_ref[0]) + bdn_ref[0]


def _expert_call(xs, blk_e, blk_src, blk_on, wgu, bgu, wdn, bdn):
    n_rows, d = xs.shape
    n_blocks = n_rows // MOE_BLOCK
    f2 = wgu.shape[2]
    f = wdn.shape[1]
    src = lambda i, be, bs, bo: (bs[i], 0)
    by_e = lambda i, be, bs, bo: (be[i], 0, 0)
    grid_spec = pltpu.PrefetchScalarGridSpec(
        num_scalar_prefetch=3,
        grid=(n_blocks,),
        in_specs=[
            pl.BlockSpec((MOE_BLOCK, d), src),
            pl.BlockSpec((1, d, f2), by_e),
            pl.BlockSpec((1, 1, f2), by_e),
            pl.BlockSpec((1, f, d), by_e),
            pl.BlockSpec((1, 1, d), by_e),
        ],
        out_specs=pl.BlockSpec((MOE_BLOCK, d), lambda i, be, bs, bo: (i, 0)),
    )
    return pl.pallas_call(
        _expert_kernel,
        grid_spec=grid_spec,
        out_shape=jax.ShapeDtypeStruct((n_rows, d), jnp.float32),
        compiler_params=_cparams(("arbitrary",)),
        name="moe_experts",
    )(blk_e, blk_src, blk_on, xs, wgu, bgu, wdn, bdn)


def _combine_kernel(dest_ref, x1_ref, wt_ref, g2_ref, fg_ref, ys_ref, o_ref, buf, sem, *, final_norm):
    def row_copy(r, k):
        return pltpu.make_async_copy(ys_ref.at[pl.ds(dest_ref[r * TOP_K + k], 1), :],
                                     buf.at[k, pl.ds(r, 1), :], sem)

    def start_row(r, c):
        for k in range(TOP_K):
            row_copy(r, k).start()
        return c

    def wait_row(r, c):
        for k in range(TOP_K):
            row_copy(r, k).wait()
        return c

    lax.fori_loop(0, TM_COMBINE, start_row, 0)
    lax.fori_loop(0, TM_COMBINE, wait_row, 0)

    wt = wt_ref[...]
    y = wt[:, 0:1] * buf[0]
    for k in range(1, TOP_K):
        y = y + wt[:, k:k + 1] * buf[k]
    out = x1_ref[...] + g2_ref[0] * y
    if final_norm:
        out = _rms(out, fg_ref[...])
    o_ref[...] = out


def _combine_call(x1, wts, g2, final_g, ys, dest_flat, seq, final_norm):
    n, d = x1.shape
    tpb = seq // TM_COMBINE
    row = lambda i: (i, 0)
    return pl.pallas_call(
        functools.partial(_combine_kernel, final_norm=final_norm),
        grid=(n // TM_COMBINE,),
        in_specs=[
            pl.BlockSpec((TM_COMBINE * TOP_K,), lambda i: (i,), memory_space=pltpu.SMEM),
            pl.BlockSpec((TM_COMBINE, d), row),
            pl.BlockSpec((TM_COMBINE, LANES), row),
            pl.BlockSpec((1, 1, d), lambda i: (i // tpb, 0, 0)),
            pl.BlockSpec((1, d), lambda i: (0, 0)),
            pl.BlockSpec(memory_space=pl.ANY),
        ],
        out_specs=pl.BlockSpec((TM_COMBINE, d), row),
        out_shape=jax.ShapeDtypeStruct((n, d), jnp.float32),
        scratch_shapes=[pltpu.VMEM((TOP_K, TM_COMBINE, d), jnp.float32), pltpu.SemaphoreType.DMA(())],
        compiler_params=_cparams(("arbitrary",)),
        name="moe_combine",
    )(dest_flat, x1, wts, g2, final_g, ys)


def _regroup_w_in(w_in):
    o_q, o_kv = 0, ATTN_WIDTH
    o_qi = o_kv + KV_LATENT
    o_ki = o_qi + IDX_HEADS * IDX_DIM
    o_wi = o_ki + IDX_DIM
    o_pool = o_wi + IDX_HEADS
    o_ga = o_pool + POOL_WIDTH
    ki = w_in[..., o_ki:o_wi]
    w_r = jnp.concatenate([w_in[..., o_q:o_ki], ki, ki, ki, ki, w_in[..., o_pool:o_ga]], axis=-1)
    w_wi_t = jnp.pad(jnp.swapaxes(w_in[..., o_wi:o_pool], 1, 2), ((0, 0), (0, SUBLANES - IDX_HEADS), (0, 0)))
    return _bf(w_r), _bf(w_wi_t), _bf(w_in[..., o_ga:])


def _block_diag_uk_t(w_uk):
    depth = w_uk.shape[0]
    eye = jnp.eye(N_HEADS, dtype=w_uk.dtype)
    t = jnp.einsum('zlhd,hg->zglhd', w_uk, eye)
    return _bf(t.reshape(depth, N_HEADS * KV_LATENT, N_HEADS * HEAD_DIM))


def _block_diag_uv(w_uv):
    depth = w_uv.shape[0]
    eye = jnp.eye(N_HEADS, dtype=w_uv.dtype)
    t = jnp.einsum('zlhd,hg->zhlgd', w_uv, eye)
    return _bf(t.reshape(depth, N_HEADS * KV_LATENT, N_HEADS * HEAD_DIM))


def kernel(x, c, norm1_g, norm2_g, w_ada, b_ada, w_in, kv_norm_g, w_uk, w_uv, w_pool, pool_scale, w_a_up, w_b_up, w_out, w_router, b_router, w_gu, b_gu, w_down, b_down, final_g):
    bsz, seq, d = x.shape
    depth = w_in.shape[0]
    n = bsz * seq
    assert TM_PROJ == TK and seq % TK == 0 and n % MOE_BLOCK == 0
    assert seq // CNT_ROWS <= 256

    ada = _ada_call(c, w_ada, b_ada).reshape(depth, bsz, 6, 1, d)
    w_r, w_wi_t, w_gate = _regroup_w_in(w_in)
    wukt_bd = _block_diag_uk_t(w_uk)
    wuv_bd = _block_diag_uv(w_uv)
    w_pool_b, w_a_up_b, w_b_up_b, w_out_b = _bf(w_pool), _bf(w_a_up), _bf(w_b_up), _bf(w_out)
    w_router_p = _bf(jnp.pad(w_router, ((0, 0), (0, 0), (0, LANES - N_EXPERTS))))
    b_router_p = jnp.pad(b_router, ((0, 0), (0, LANES - N_EXPERTS)), constant_values=NEG_BIG)
    w_gu_b, w_down_b = _bf(w_gu), _bf(w_down)

    n_asg = n * TOP_K
    n_blocks = -(-n_asg // MOE_BLOCK) + N_EXPERTS
    n_rows = n_blocks * MOE_BLOCK

    xf = x.reshape(n, d)
    for l in range(depth):
        sh1, sc1, g1, sh2, sc2, g2 = [ada[l, :, j] for j in range(6)]
        q, ckv, ckv_t, qi, ki4, wi_t, apool = _proj_call(
            xf, sh1, sc1, norm1_g[l][None], w_r[l], w_wi_t[l], kv_norm_g[l][None], seq)
        o_a = _attn_call(q, qi, wi_t, ckv, ckv_t, ki4, wukt_bd[l], wuv_bd[l], bsz, seq)
        x1, h2, logits = _mix_call(xf, o_a, apool, sh1, sc1, g1, norm1_g[l][None], w_gate[l], w_pool_b[l],
                                   pool_scale[l][None], w_a_up_b[l], w_b_up_b[l], w_out_b[l],
                                   sh2, sc2, norm2_g[l][None], w_router_p[l], b_router_p[l][None], seq)
        idx_l, wts, rank_l, counts = _route_call(logits)

        sizes = counts[0, :N_EXPERTS].astype(jnp.int32)
        padded = ((sizes + MOE_BLOCK - 1) // MOE_BLOCK) * MOE_BLOCK
        end_pad = jnp.cumsum(padded)
        start_pad = end_pad - padded
        dest = start_pad[idx_l[:, :TOP_K]] + rank_l[:, :TOP_K]
        dest_flat = dest.reshape(-1).astype(jnp.int32)
        blk_start = jnp.arange(n_blocks, dtype=jnp.int32) * MOE_BLOCK
        blk_e = jnp.minimum(jnp.searchsorted(end_pad, blk_start, side='right'), N_EXPERTS - 1).astype(jnp.int32)
        n_used = end_pad[-1] // MOE_BLOCK
        blk_idx = jnp.arange(n_blocks, dtype=jnp.int32)
        blk_on = (blk_idx < n_used).astype(jnp.int32)
        blk_src = jnp.minimum(blk_idx, n_used - 1).astype(jnp.int32)
        blk_e = jnp.where(blk_on == 1, blk_e, blk_e[n_used - 1])

        xs = _scatter_call(h2, dest_flat, end_pad.astype(jnp.int32), n_rows)
        ys = _expert_call(xs, blk_e, blk_src, blk_on, w_gu_b[l], b_gu[l][:, None, :], w_down_b[l], b_down[l][:, None, :])
        xf = _combine_call(x1, wts, g2, final_g[None], ys, dest_flat, seq, final_norm=(l == depth - 1))
    return xf.reshape(bsz, seq, d)
```

```python
import functools

import jax
import jax.numpy as jnp
from jax import lax
from jax.experimental import pallas as pl
from jax.experimental.pallas import tpu as pltpu

N_HEADS = 8
HEAD_DIM = 64
ATTN_WIDTH = N_HEADS * HEAD_DIM
KV_LATENT = 128
IDX_HEADS = 4
IDX_DIM = 32
TOPK_MAX = 256
ATTN_SCALE = HEAD_DIM ** -0.5
IDX_W_SCALE = (IDX_HEADS ** -0.5) * (IDX_DIM ** -0.5)
POOL_WINDOWS = (2, 4, 8, 16)
POOL_GROUP_DIM = 128
POOL_WIDTH = len(POOL_WINDOWS) * POOL_GROUP_DIM
N_EXPERTS = 32
TOP_K = 4
MOE_BLOCK = 512
SWIGLU_LIMIT = 7.0
SWIGLU_ALPHA = 1.702
RMS_EPS = 1e-6

LANES = 128
SUBLANES = 8
POOL_HALO = 16
DMA_PRIORITIES = 2
NEG_BIG = -1e30
INT_MIN = -2 ** 31
HALF_BITS = 16
HALF_OFFSET = 2 ** (HALF_BITS - 1)

TM_PROJ = 512
TQ = 512
TK = 512
GROUP_HEADS = 8
CNT_ROWS = 64
ONES_ROWS = 16
KVT_ROWS = KV_LATENT + ONES_ROWS
LOG2E = 1.4426950408889634
TM_MIX = 256
TM_ROUTE = 512
TM_SCATTER = 256
TM_COMBINE = 256
VMEM_LIMIT = 56 * 1024 * 1024


def _cparams(sem):
    return pltpu.CompilerParams(dimension_semantics=sem, vmem_limit_bytes=VMEM_LIMIT)


def _rms(xf, g):
    return xf * lax.rsqrt(jnp.mean(xf * xf, axis=-1, keepdims=True) + RMS_EPS) * g


def _bf(v):
    return v.astype(jnp.bfloat16)


def _dot(a, b):
    return jnp.dot(a, b, preferred_element_type=jnp.float32)


def _dot_nt(a, b):
    return lax.dot_general(a, b, (((1,), (1,)), ((), ())), preferred_element_type=jnp.float32)


def _store_rows_as_tiles(ref, v):
    rows = v.shape[0]
    for s in range(SUBLANES):
        ref[pl.ds(s, rows, stride=SUBLANES), :] = v[:, s * LANES:(s + 1) * LANES]


def _tile_rows(ref, row, count):
    return ref.at[pl.ds(pl.multiple_of(row * SUBLANES, SUBLANES), count * SUBLANES), :]


def _load_rows_from_tiles(ref, rows):
    return jnp.concatenate([ref[pl.ds(s, rows, stride=SUBLANES), :] for s in range(SUBLANES)], axis=1)


def _strict_lower(n):
    return _bf(jnp.where(lax.broadcasted_iota(jnp.int32, (n, n), 1)
                         < lax.broadcasted_iota(jnp.int32, (n, n), 0), 1.0, 0.0))


def _ada_kernel(c_ref, w_ref, b_ref, o_ref):
    cf = c_ref[...]
    cond = cf * jax.nn.sigmoid(cf)
    o_ref[0] = _dot(_bf(cond), _bf(w_ref[0])) + b_ref[0]


def _ada_call(c, w_ada, b_ada):
    depth, d, n6 = w_ada.shape
    bsz = c.shape[0]
    tn = 1024
    return pl.pallas_call(
        _ada_kernel,
        grid=(depth, n6 // tn),
        in_specs=[
            pl.BlockSpec((bsz, d), lambda l, j: (0, 0)),
            pl.BlockSpec((1, d, tn), lambda l, j: (l, 0, j)),
            pl.BlockSpec((1, 1, tn), lambda l, j: (l, 0, j)),
        ],
        out_specs=pl.BlockSpec((1, bsz, tn), lambda l, j: (l, 0, j)),
        out_shape=jax.ShapeDtypeStruct((depth, bsz, n6), jnp.float32),
        compiler_params=_cparams(("arbitrary", "arbitrary")),
        name="ada",
    )(c, w_ada, b_ada.reshape(depth, 1, n6))


PROJ_COLS = ATTN_WIDTH + 3 * LANES + POOL_WIDTH


def _proj_kernel(x_ref, sh_ref, sc_ref, g_ref, w_ref, wwi_ref, kvg_ref,
                 q_ref, ckv_ref, ckvt_ref, qi_ref, ki_ref, wit_ref, ap_ref):
    h = _bf(_rms(x_ref[...], g_ref[...]) * (1.0 + sc_ref[0]) + sh_ref[0])
    p = _dot(h, w_ref[...])
    o = 0
    q_ref[...] = _bf(p[:, o:o + ATTN_WIDTH]); o += ATTN_WIDTH
    ckv = _bf(_rms(p[:, o:o + KV_LATENT], kvg_ref[...])); o += KV_LATENT
    ckv_ref[...] = ckv
    eye = _bf(jnp.where(lax.broadcasted_iota(jnp.int32, (KV_LATENT, KV_LATENT), 0)
                        == lax.broadcasted_iota(jnp.int32, (KV_LATENT, KV_LATENT), 1), 1.0, 0.0))
    ckvt_ref[0, :KV_LATENT, :] = _bf(_dot_nt(eye, ckv))
    ckvt_ref[0, KV_LATENT:, :] = jnp.ones((ONES_ROWS, TM_PROJ), jnp.bfloat16)
    qi_ref[...] = _bf(p[:, o:o + LANES]); o += LANES
    ki_ref[...] = _bf(p[:, o:o + LANES]); o += LANES
    ap_ref[...] = p[:, o:o + POOL_WIDTH]
    wit_ref[...] = _dot_nt(wwi_ref[...], h)


def _proj_call(xf, sh1, sc1, g1n, w_r, w_wi_t, kvg, seq):
    n, d = xf.shape
    tpb = seq // TM_PROJ
    row = lambda i: (i, 0)
    per_b = lambda i: (i // tpb, 0, 0)
    fixed = lambda i: (0, 0)
    return pl.pallas_call(
        _proj_kernel,
        grid=(n // TM_PROJ,),
        in_specs=[
            pl.BlockSpec((TM_PROJ, d), row),
            pl.BlockSpec((1, 1, d), per_b),
            pl.BlockSpec((1, 1, d), per_b),
            pl.BlockSpec((1, d), fixed),
            pl.BlockSpec((d, PROJ_COLS), fixed),
            pl.BlockSpec((SUBLANES, d), fixed),
            pl.BlockSpec((1, KV_LATENT), fixed),
        ],
        out_specs=[
            pl.BlockSpec((TM_PROJ, ATTN_WIDTH), row),
            pl.BlockSpec((TM_PROJ, KV_LATENT), row),
            pl.BlockSpec((1, KVT_ROWS, TM_PROJ), lambda i: (i, 0, 0)),
            pl.BlockSpec((TM_PROJ, LANES), row),
            pl.BlockSpec((TM_PROJ, LANES), row),
            pl.BlockSpec((SUBLANES, TM_PROJ), lambda i: (0, i)),
            pl.BlockSpec((TM_PROJ, POOL_WIDTH), row),
        ],
        out_shape=[
            jax.ShapeDtypeStruct((n, ATTN_WIDTH), jnp.bfloat16),
            jax.ShapeDtypeStruct((n, KV_LATENT), jnp.bfloat16),
            jax.ShapeDtypeStruct((n // TM_PROJ, KVT_ROWS, TM_PROJ), jnp.bfloat16),
            jax.ShapeDtypeStruct((n, LANES), jnp.bfloat16),
            jax.ShapeDtypeStruct((n, LANES), jnp.bfloat16),
            jax.ShapeDtypeStruct((SUBLANES, n), jnp.float32),
            jax.ShapeDtypeStruct((n, POOL_WIDTH), jnp.float32),
        ],
        compiler_params=_cparams(("arbitrary",)),
        name="in_proj",
    )(xf, sh1, sc1, g1n, w_r, w_wi_t, kvg)


def _attn_kernel(q_ref, qi_ref, wit_ref, ckv_ref, ckvt_ref, ki_ref, wukt_ref, wuv_ref, o_ref,
                 key_buf, hi_buf, lo_buf, qs_buf, acc_buf, m_buf, *, k_sel):
    qb = pl.program_id(1)
    q_start = qb * TQ
    n_kc = (q_start + TQ + TK - 1) // TK

    krow = lax.broadcasted_iota(jnp.int32, (TK, TQ), 0)
    qcol = q_start + lax.broadcasted_iota(jnp.int32, (TK, TQ), 1)

    qi = qi_ref[...]
    lane = lax.broadcasted_iota(jnp.int32, (TQ, LANES), 1)
    q4 = jnp.concatenate(
        [jnp.where((lane >= h * IDX_DIM) & (lane < (h + 1) * IDX_DIM), qi, jnp.zeros_like(qi))
         for h in range(IDX_HEADS)], axis=0)
    wit = wit_ref[...] * IDX_W_SCALE
    w_rows = [wit[h:h + 1, :] for h in range(IDX_HEADS)]

    def score_chunk(kc, carry):
        k0 = pl.multiple_of(kc * TK, TK)
        raw = _dot_nt(ki_ref[pl.ds(k0, TK), :], q4)
        score = w_rows[0] * jnp.maximum(raw[:, 0:TQ], 0.0)
        for h in range(1, IDX_HEADS):
            score = score + w_rows[h] * jnp.maximum(raw[:, h * TQ:(h + 1) * TQ], 0.0)
        score = score + 0.0
        bits = pltpu.bitcast(score, jnp.int32)
        key = jnp.where(bits < 0, bits ^ jnp.int32(0x7FFFFFFF), bits)
        key = jnp.where(krow + k0 <= qcol, key, jnp.int32(INT_MIN))
        key_buf[kc] = key
        hi_buf[kc] = (key >> HALF_BITS).astype(jnp.int16)
        return carry

    lax.fori_loop(0, n_kc, score_chunk, 0)

    def count16(plane, cand, strict):
        def body(kc, acc):
            for j in range(TK // CNT_ROWS):
                pj = plane[kc, j * CNT_ROWS:(j + 1) * CNT_ROWS, :]
                hit = (pj > cand) if strict else (pj >= cand)
                acc = acc + jnp.where(hit, jnp.bfloat16(1.0), jnp.bfloat16(0.0))
            return acc

        acc = lax.fori_loop(0, n_kc, body, jnp.zeros((CNT_ROWS, TQ), jnp.bfloat16))
        return jnp.sum(acc.astype(jnp.float32), axis=0, keepdims=True)

    def to_plane(t_u):
        return (t_u - HALF_OFFSET).astype(jnp.int16)

    def search16(plane, need):
        def bit_step(i, t_u):
            cand_u = t_u | lax.shift_left(jnp.int32(1), HALF_BITS - 1 - i)
            cnt = count16(plane, to_plane(cand_u), False)
            return jnp.where(cnt >= need, cand_u, t_u)

        return lax.fori_loop(0, HALF_BITS, bit_step, jnp.zeros((1, TQ), jnp.int32))

    hi_u = search16(hi_buf, float(k_sel))
    thr_hi = to_plane(hi_u)
    need_lo = float(k_sel) - count16(hi_buf, thr_hi, True)

    def low_plane_chunk(kc, carry):
        lo = ((key_buf[kc] & jnp.int32(HALF_OFFSET * 2 - 1)) - HALF_OFFSET).astype(jnp.int16)
        lo_buf[kc] = jnp.where(hi_buf[kc] == thr_hi, lo, jnp.int16(-HALF_OFFSET))
        return carry

    lax.fori_loop(0, n_kc, low_plane_chunk, 0)
    lo_u = search16(lo_buf, need_lo)
    thr = lax.shift_left(hi_u - HALF_OFFSET, HALF_BITS) | lo_u
    n_tie_take = need_lo - count16(lo_buf, to_plane(lo_u), True)

    qlat_t = _dot_nt(wukt_ref[...], q_ref[...]) * (ATTN_SCALE * LOG2E)
    for h in range(N_HEADS):
        qs_buf[:, h * TQ:(h + 1) * TQ] = _bf(qlat_t[h * KV_LATENT:(h + 1) * KV_LATENT, :])
    lower = _strict_lower(LANES)
    m_buf[...] = jnp.full(m_buf.shape, -3e38, jnp.float32)
    acc_buf[...] = jnp.zeros(acc_buf.shape, jnp.float32)

    def attend_chunk(kc, tie_seen):
        k0 = pl.multiple_of(kc * TK, TK)
        key = key_buf[kc]
        eq = key == thr
        eq_f = jnp.where(eq, 1.0, 0.0)
        eq_b = _bf(eq_f)
        ranks = []
        for j in range(TK // LANES):
            rows = slice(j * LANES, (j + 1) * LANES)
            ranks.append(_dot(lower, eq_b[rows]) + tie_seen)
            tie_seen = tie_seen + jnp.sum(eq_f[rows], axis=0, keepdims=True)
        tie_rank = jnp.concatenate(ranks, axis=0)
        sel = ((key > thr) | (eq & (tie_rank < n_tie_take))) & (krow + k0 <= qcol)
        bias = jnp.where(sel, 0.0, NEG_BIG)
        kv = ckv_ref[pl.ds(k0, TK), :]
        kv_t = ckvt_ref[kc]
        for g in range(N_HEADS // GROUP_HEADS):
            gcols = slice(g * GROUP_HEADS * TQ, (g + 1) * GROUP_HEADS * TQ)
            logits = _dot(kv, qs_buf[:, gcols])
            ps, alphas = [], []
            for hh in range(GROUP_HEADS):
                cols = slice((g * GROUP_HEADS + hh) * TQ, (g * GROUP_HEADS + hh + 1) * TQ)
                lg = logits[:, hh * TQ:(hh + 1) * TQ] + bias
                m_old = m_buf[:, cols]
                m_new = jnp.maximum(m_old, jnp.max(lg, axis=0, keepdims=True))
                m_buf[:, cols] = m_new
                alphas.append(jnp.exp2(m_old - m_new))
                ps.append(_bf(jnp.exp2(lg - m_new)))
            acc_buf[:, gcols] = (acc_buf[:, gcols] * jnp.concatenate(alphas, axis=1)
                                 + _dot(kv_t, jnp.concatenate(ps, axis=1)))
        return tie_seen

    lax.fori_loop(0, n_kc, attend_chunk, jnp.zeros((1, TQ), jnp.float32))

    o_lat_t = acc_buf[:KV_LATENT, :] / acc_buf[KV_LATENT:KV_LATENT + 1, :]
    stacked = jnp.concatenate(
        [o_lat_t[:, h * TQ:(h + 1) * TQ] for h in range(N_HEADS)], axis=0)
    o_ref[...] = _bf(_dot(_bf(stacked.T), wuv_ref[...]))


def _attn_call(q, qi, wi_t, ckv, ckv_t, ki4, wukt_bd, wuv_bd, bsz, seq):
    n = q.shape[0]
    nq = seq // TQ
    nkc = seq // TK
    k_sel = min(TOPK_MAX, seq // 4)
    qrow = lambda b, i: (b * nq + i, 0)
    per_b = lambda b, i: (b, 0)
    fixed = lambda b, i: (0, 0)
    return pl.pallas_call(
        functools.partial(_attn_kernel, k_sel=k_sel),
        grid=(bsz, nq),
        in_specs=[
            pl.BlockSpec((TQ, ATTN_WIDTH), qrow),
            pl.BlockSpec((TQ, LANES), qrow),
            pl.BlockSpec((SUBLANES, TQ), lambda b, i: (0, b * nq + i)),
            pl.BlockSpec((seq, KV_LATENT), per_b),
            pl.BlockSpec((nkc, KVT_ROWS, TK), lambda b, i: (b, 0, 0)),
            pl.BlockSpec((seq, LANES), per_b),
            pl.BlockSpec((N_HEADS * KV_LATENT, ATTN_WIDTH), fixed),
            pl.BlockSpec((N_HEADS * KV_LATENT, ATTN_WIDTH), fixed),
        ],
        out_specs=pl.BlockSpec((TQ, ATTN_WIDTH), qrow),
        out_shape=jax.ShapeDtypeStruct((n, ATTN_WIDTH), jnp.bfloat16),
        scratch_shapes=[
            pltpu.VMEM((nkc, TK, TQ), jnp.int32),
            pltpu.VMEM((nkc, TK, TQ), jnp.int16),
            pltpu.VMEM((nkc, TK, TQ), jnp.int16),
            pltpu.VMEM((KV_LATENT, N_HEADS * TQ), jnp.bfloat16),
            pltpu.VMEM((KVT_ROWS, N_HEADS * TQ), jnp.float32),
            pltpu.VMEM((1, N_HEADS * TQ), jnp.float32),
        ],
        compiler_params=_cparams(("arbitrary", "arbitrary")),
        name="dsa_attention",
    )(q, qi, wi_t, ckv, ckv_t, ki4, wukt_bd, wuv_bd)


def _mix_kernel(x_ref, oa_ref, ap_ref, halo_ref, sh1_ref, sc1_ref, g1_ref, n1g_ref,
                wgate_ref, wpool_ref, pscale_ref, waup_ref, wbup_ref, wout_ref,
                sh2_ref, sc2_ref, n2g_ref, wr_ref, br_ref,
                x1_ref, h2_ref, lg_ref, *, tiles_per_batch):
    i = pl.program_id(0)
    t_in_b = i % tiles_per_batch
    x = x_ref[...]
    d = x.shape[1]
    h = _rms(x, n1g_ref[...]) * (1.0 + sc1_ref[0]) + sh1_ref[0]
    gates = _dot(_bf(h), wgate_ref[...])

    pos1 = (t_in_b * TM_MIX + 1 + lax.broadcasted_iota(jnp.int32, (TM_MIX, 1), 0)).astype(jnp.float32)
    halo = jnp.where(t_in_b == 0, 0.0, halo_ref[...])
    zs = []
    for g, w in enumerate(POOL_WINDOWS):
        sl = slice(g * POOL_GROUP_DIM, (g + 1) * POOL_GROUP_DIM)
        a = ap_ref[:, sl]
        ext = jnp.concatenate([halo[:, sl], a], axis=0)
        span = 1
        while span < w:
            ext = ext + pltpu.roll(ext, span, 0)
            span *= 2
        mean = ext[POOL_HALO:] / jnp.minimum(pos1, float(w))
        zs.append(_dot(_bf(mean - a), wpool_ref[g]))
    o_b = jnp.concatenate(zs, axis=1) * pscale_ref[...]

    merged = (jax.nn.sigmoid(gates[:, :d]) * _dot(oa_ref[...], waup_ref[...])
              + jax.nn.sigmoid(gates[:, d:]) * _dot(_bf(o_b), wbup_ref[...]))
    x1 = x + g1_ref[0] * _dot(_bf(merged), wout_ref[...])
    x1_ref[...] = x1
    h2 = _rms(x1, n2g_ref[...]) * (1.0 + sc2_ref[0]) + sh2_ref[0]
    _store_rows_as_tiles(h2_ref, h2)
    lg_ref[...] = _dot(_bf(h2), wr_ref[...]) + br_ref[...]


def _mix_call(xf, o_a, apool, sh1, sc1, g1, n1g, wgate, wpool, pscale, waup, wbup, wout,
              sh2, sc2, n2g, wr, br, seq):
    n, d = xf.shape
    tpb = seq // TM_MIX
    row = lambda i: (i, 0)
    per_b = lambda i: (i // tpb, 0, 0)
    fixed = lambda i: (0, 0)
    fixed3 = lambda i: (0, 0, 0)
    halo_idx = lambda i: (jnp.maximum(i * (TM_MIX // POOL_HALO) - 1, 0), 0)
    mod = pl.BlockSpec((1, 1, d), per_b)
    return pl.pallas_call(
        functools.partial(_mix_kernel, tiles_per_batch=tpb),
        grid=(n // TM_MIX,),
        in_specs=[
            pl.BlockSpec((TM_MIX, d), row),
            pl.BlockSpec((TM_MIX, ATTN_WIDTH), row),
            pl.BlockSpec((TM_MIX, POOL_WIDTH), row),
            pl.BlockSpec((POOL_HALO, POOL_WIDTH), halo_idx),
            mod, mod, mod,
            pl.BlockSpec((1, d), fixed),
            pl.BlockSpec((d, 2 * d), fixed),
            pl.BlockSpec((len(POOL_WINDOWS), POOL_GROUP_DIM, POOL_GROUP_DIM), fixed3),
            pl.BlockSpec((1, POOL_WIDTH), fixed),
            pl.BlockSpec((ATTN_WIDTH, d), fixed),
            pl.BlockSpec((POOL_WIDTH, d), fixed),
            pl.BlockSpec((d, d), fixed),
            mod, mod,
            pl.BlockSpec((1, d), fixed),
            pl.BlockSpec((d, LANES), fixed),
            pl.BlockSpec((1, LANES), fixed),
        ],
        out_specs=[pl.BlockSpec((TM_MIX, d), row), pl.BlockSpec((TM_MIX * SUBLANES, LANES), row),
                   pl.BlockSpec((TM_MIX, LANES), row)],
        out_shape=[jax.ShapeDtypeStruct((n, d), jnp.float32),
                   jax.ShapeDtypeStruct((n * SUBLANES, LANES), jnp.float32),
                   jax.ShapeDtypeStruct((n, LANES), jnp.float32)],
        compiler_params=_cparams(("arbitrary",)),
        name="mix_merge",
    )(xf, o_a, apool, apool, sh1, sc1, g1, n1g, wgate, wpool, pscale, waup, wbup, wout,
      sh2, sc2, n2g, wr, br)


def _route_kernel(lg_ref, idx_ref, wt_ref, rank_ref, cnt_ref, carry):
    @pl.when(pl.program_id(0) == 0)
    def _():
        carry[...] = jnp.zeros(carry.shape, jnp.float32)

    work = lg_ref[...]
    lane = lax.broadcasted_iota(jnp.int32, work.shape, 1).astype(jnp.float32)
    vals, idxs = [], []
    for _ in range(TOP_K):
        m = jnp.max(work, axis=1, keepdims=True)
        idx = jnp.min(jnp.where(work == m, lane, float(LANES)), axis=1, keepdims=True)
        vals.append(m)
        idxs.append(idx)
        work = jnp.where(lane == idx, -jnp.inf, work)
    exps = [jnp.exp(v - vals[0]) for v in vals]
    denom = exps[0] + exps[1] + exps[2] + exps[3]

    onehot = jnp.zeros(work.shape, jnp.float32)
    for idx in idxs:
        onehot = onehot + jnp.where(lane == idx, 1.0, 0.0)
    before = _dot(_strict_lower(work.shape[0]), _bf(onehot)) + carry[...]

    idx_out = jnp.zeros(work.shape, jnp.int32)
    wt_out = jnp.zeros(work.shape, jnp.float32)
    rank_out = jnp.zeros(work.shape, jnp.int32)
    for k in range(TOP_K):
        rank_k = jnp.sum(jnp.where(lane == idxs[k], before, 0.0), axis=1, keepdims=True)
        idx_out = jnp.where(lane == k, idxs[k].astype(jnp.int32), idx_out)
        wt_out = jnp.where(lane == k, exps[k] / denom, wt_out)
        rank_out = jnp.where(lane == k, rank_k.astype(jnp.int32), rank_out)
    idx_ref[...] = idx_out
    wt_ref[...] = wt_out
    rank_ref[...] = rank_out
    carry[...] = carry[...] + jnp.sum(onehot, axis=0, keepdims=True)
    cnt_ref[...] = carry[...]


def _route_call(logits):
    n = logits.shape[0]
    row = lambda i: (i, 0)
    return pl.pallas_call(
        _route_kernel,
        grid=(n // TM_ROUTE,),
        in_specs=[pl.BlockSpec((TM_ROUTE, LANES), row)],
        out_specs=[pl.BlockSpec((TM_ROUTE, LANES), row)] * 3 + [pl.BlockSpec((1, LANES), lambda i: (0, 0))],
        out_shape=[jax.ShapeDtypeStruct((n, LANES), jnp.int32), jax.ShapeDtypeStruct((n, LANES), jnp.float32),
                   jax.ShapeDtypeStruct((n, LANES), jnp.int32), jax.ShapeDtypeStruct((1, LANES), jnp.float32)],
        scratch_shapes=[pltpu.VMEM((1, LANES), jnp.float32)],
        compiler_params=_cparams(("arbitrary",)),
        name="route",
    )(logits)


def _scatter_kernel(endpad_ref, dest_ref, h2_ref, xs_ref, zero_buf, sem_zero, sem_rows):
    @pl.when(pl.program_id(0) == 0)
    def _():
        zero_buf[...] = jnp.zeros(zero_buf.shape, zero_buf.dtype)

        def zero_block(start):
            return pltpu.make_async_copy(zero_buf, _tile_rows(xs_ref, start, MOE_BLOCK), sem_zero)

        def last_block_of(e):
            return jnp.maximum(endpad_ref[e] - MOE_BLOCK, 0)

        def start_zero(e, c):
            zero_block(last_block_of(e)).start()
            return c

        def wait_zero(e, c):
            zero_block(last_block_of(e)).wait()
            return c

        def start_tail(b, c):
            zero_block(b * MOE_BLOCK).start()
            return c

        def wait_tail(b, c):
            zero_block(b * MOE_BLOCK).wait()
            return c

        n_used = endpad_ref[N_EXPERTS - 1] // MOE_BLOCK
        n_blocks = xs_ref.shape[0] // (MOE_BLOCK * SUBLANES)
        lax.fori_loop(0, N_EXPERTS, start_zero, 0)
        lax.fori_loop(n_used, n_blocks, start_tail, 0)
        lax.fori_loop(0, N_EXPERTS, wait_zero, 0)
        lax.fori_loop(n_used, n_blocks, wait_tail, 0)

    def row_copy(r, k):
        return pltpu.make_async_copy(_tile_rows(h2_ref, r, 1),
                                     _tile_rows(xs_ref, dest_ref[r * TOP_K + k], 1), sem_rows)

    def start_row(r, c):
        for k in range(TOP_K):
            row_copy(r, k).start(priority=k % DMA_PRIORITIES)
        return c

    def wait_row(r, c):
        for k in range(TOP_K):
            row_copy(r, k).wait()
        return c

    lax.fori_loop(0, TM_SCATTER, start_row, 0)
    lax.fori_loop(0, TM_SCATTER, wait_row, 0)


def _scatter_call(h2, dest_flat, end_pad, n_rows):
    n = h2.shape[0] // SUBLANES
    grid_spec = pltpu.PrefetchScalarGridSpec(
        num_scalar_prefetch=1,
        grid=(n // TM_SCATTER,),
        in_specs=[
            pl.BlockSpec((TM_SCATTER * TOP_K,), lambda i, ep: (i,), memory_space=pltpu.SMEM),
            pl.BlockSpec((TM_SCATTER * SUBLANES, LANES), lambda i, ep: (i, 0)),
        ],
        out_specs=pl.BlockSpec(memory_space=pl.ANY),
        scratch_shapes=[pltpu.VMEM((MOE_BLOCK * SUBLANES, LANES), h2.dtype),
                        pltpu.SemaphoreType.DMA(()), pltpu.SemaphoreType.DMA(())],
    )
    return pl.pallas_call(
        _scatter_kernel,
        grid_spec=grid_spec,
        out_shape=jax.ShapeDtypeStruct((n_rows * SUBLANES, LANES), h2.dtype),
        compiler_params=_cparams(("arbitrary",)),
        name="moe_scatter",
    )(end_pad, dest_flat, h2)


def _expert_kernel(blk_e_ref, blk_src_ref, blk_on_ref, xs_ref, wgu_ref, bgu_ref, wdn_ref, bdn_ref, ys_ref):
    on = blk_on_ref[pl.program_id(0)] == 1

    @pl.when(jnp.logical_not(on))
    def _():
        ys_ref[...] = jnp.zeros(ys_ref.shape, ys_ref.dtype)

    @pl.when(on)
    def _():
        f = wdn_ref.shape[1]
        gu = _dot(_bf(_load_rows_from_tiles(xs_ref, MOE_BLOCK)), wgu_ref[0]) + bgu_ref[0]
        gt = jnp.minimum(gu[:, :f], SWIGLU_LIMIT)
        up = jnp.clip(gu[:, f:], -SWIGLU_LIMIT, SWIGLU_LIMIT)
        act = gt * jax.nn.sigmoid(SWIGLU_ALPHA * gt) * (up + 1.0)
        _store_rows_as_tiles(ys_ref, _dot(_bf(act), wdn_ref[0]) + bdn_ref[0])


def _expert_call(xs, blk_e, blk_src, blk_on, wgu, bgu, wdn, bdn):
    n_blocks = xs.shape[0] // (MOE_BLOCK * SUBLANES)
    d, f2 = wgu.shape[1], wgu.shape[2]
    f = wdn.shape[1]
    src = lambda i, be, bs, bo: (bs[i], 0)
    by_e = lambda i, be, bs, bo: (be[i], 0, 0)
    grid_spec = pltpu.PrefetchScalarGridSpec(
        num_scalar_prefetch=3,
        grid=(n_blocks,),
        in_specs=[
            pl.BlockSpec((MOE_BLOCK * SUBLANES, LANES), src),
            pl.BlockSpec((1, d, f2), by_e),
            pl.BlockSpec((1, 1, f2), by_e),
            pl.BlockSpec((1, f, d), by_e),
            pl.BlockSpec((1, 1, d), by_e),
        ],
        out_specs=pl.BlockSpec((MOE_BLOCK * SUBLANES, LANES), lambda i, be, bs, bo: (i, 0)),
    )
    return pl.pallas_call(
        _expert_kernel,
        grid_spec=grid_spec,
        out_shape=jax.ShapeDtypeStruct(xs.shape, jnp.float32),
        compiler_params=_cparams(("arbitrary",)),
        name="moe_experts",
    )(blk_e, blk_src, blk_on, xs, wgu, bgu, wdn, bdn)


def _combine_kernel(dest_ref, x1_ref, wt_ref, g2_ref, fg_ref, ys_ref, o_ref, buf, sem, *, final_norm):
    def row_copy(r, k):
        return pltpu.make_async_copy(_tile_rows(ys_ref, dest_ref[r * TOP_K + k], 1),
                                     _tile_rows(buf.at[k], r, 1), sem)

    def start_row(r, c):
        for k in range(TOP_K):
            row_copy(r, k).start(priority=k % DMA_PRIORITIES)
        return c

    def wait_row(r, c):
        for k in range(TOP_K):
            row_copy(r, k).wait()
        return c

    lax.fori_loop(0, TM_COMBINE, start_row, 0)
    lax.fori_loop(0, TM_COMBINE, wait_row, 0)

    wt = wt_ref[...]
    y = wt[:, 0:1] * _load_rows_from_tiles(buf.at[0], TM_COMBINE)
    for k in range(1, TOP_K):
        y = y + wt[:, k:k + 1] * _load_rows_from_tiles(buf.at[k], TM_COMBINE)
    out = x1_ref[...] + g2_ref[0] * y
    if final_norm:
        out = _rms(out, fg_ref[...])
    o_ref[...] = out


def _combine_call(x1, wts, g2, final_g, ys, dest_flat, seq, final_norm):
    n, d = x1.shape
    tpb = seq // TM_COMBINE
    row = lambda i: (i, 0)
    return pl.pallas_call(
        functools.partial(_combine_kernel, final_norm=final_norm),
        grid=(n // TM_COMBINE,),
        in_specs=[
            pl.BlockSpec((TM_COMBINE * TOP_K,), lambda i: (i,), memory_space=pltpu.SMEM),
            pl.BlockSpec((TM_COMBINE, d), row),
            pl.BlockSpec((TM_COMBINE, LANES), row),
            pl.BlockSpec((1, 1, d), lambda i: (i // tpb, 0, 0)),
            pl.BlockSpec((1, d), lambda i: (0, 0)),
            pl.BlockSpec(memory_space=pl.ANY),
        ],
        out_specs=pl.BlockSpec((TM_COMBINE, d), row),
        out_shape=jax.ShapeDtypeStruct((n, d), jnp.float32),
        scratch_shapes=[pltpu.VMEM((TOP_K, TM_COMBINE * SUBLANES, LANES), jnp.float32),
                        pltpu.SemaphoreType.DMA(())],
        compiler_params=_cparams(("arbitrary",)),
        name="moe_combine",
    )(dest_flat, x1, wts, g2, final_g, ys)


def _regroup_w_in(w_in):
    o_q, o_kv = 0, ATTN_WIDTH
    o_qi = o_kv + KV_LATENT
    o_ki = o_qi + IDX_HEADS * IDX_DIM
    o_wi = o_ki + IDX_DIM
    o_pool = o_wi + IDX_HEADS
    o_ga = o_pool + POOL_WIDTH
    ki = w_in[..., o_ki:o_wi]
    w_r = jnp.concatenate([w_in[..., o_q:o_ki], ki, ki, ki, ki, w_in[..., o_pool:o_ga]], axis=-1)
    w_wi_t = jnp.pad(jnp.swapaxes(w_in[..., o_wi:o_pool], 1, 2), ((0, 0), (0, SUBLANES - IDX_HEADS), (0, 0)))
    return _bf(w_r), _bf(w_wi_t), _bf(w_in[..., o_ga:])


def _block_diag_uk_t(w_uk):
    depth = w_uk.shape[0]
    eye = jnp.eye(N_HEADS, dtype=w_uk.dtype)
    t = jnp.einsum('zlhd,hg->zglhd', w_uk, eye)
    return _bf(t.reshape(depth, N_HEADS * KV_LATENT, N_HEADS * HEAD_DIM))


def _block_diag_uv(w_uv):
    depth = w_uv.shape[0]
    eye = jnp.eye(N_HEADS, dtype=w_uv.dtype)
    t = jnp.einsum('zlhd,hg->zhlgd', w_uv, eye)
    return _bf(t.reshape(depth, N_HEADS * KV_LATENT, N_HEADS * HEAD_DIM))


def kernel(x, c, norm1_g, norm2_g, w_ada, b_ada, w_in, kv_norm_g, w_uk, w_uv, w_pool, pool_scale, w_a_up, w_b_up, w_out, w_router, b_router, w_gu, b_gu, w_down, b_down, final_g):
    bsz, seq, d = x.shape
    depth = w_in.shape[0]
    n = bsz * seq
    assert TM_PROJ == TK and seq % TK == 0 and n % MOE_BLOCK == 0
    assert seq // CNT_ROWS <= 256
    assert d == SUBLANES * LANES

    ada = _ada_call(c, w_ada, b_ada).reshape(depth, bsz, 6, 1, d)
    w_r, w_wi_t, w_gate = _regroup_w_in(w_in)
    wukt_bd = _block_diag_uk_t(w_uk)
    wuv_bd = _block_diag_uv(w_uv)
    w_pool_b, w_a_up_b, w_b_up_b, w_out_b = _bf(w_pool), _bf(w_a_up), _bf(w_b_up), _bf(w_out)
    w_router_p = _bf(jnp.pad(w_router, ((0, 0), (0, 0), (0, LANES - N_EXPERTS))))
    b_router_p = jnp.pad(b_router, ((0, 0), (0, LANES - N_EXPERTS)), constant_values=NEG_BIG)
    w_gu_b, w_down_b = _bf(w_gu), _bf(w_down)

    n_asg = n * TOP_K
    n_blocks = -(-n_asg // MOE_BLOCK) + N_EXPERTS
    n_rows = n_blocks * MOE_BLOCK

    xf = x.reshape(n, d)
    for l in range(depth):
        sh1, sc1, g1, sh2, sc2, g2 = [ada[l, :, j] for j in range(6)]
        q, ckv, ckv_t, qi, ki4, wi_t, apool = _proj_call(
            xf, sh1, sc1, norm1_g[l][None], w_r[l], w_wi_t[l], kv_norm_g[l][None], seq)
        o_a = _attn_call(q, qi, wi_t, ckv, ckv_t, ki4, wukt_bd[l], wuv_bd[l], bsz, seq)
        x1, h2, logits = _mix_call(xf, o_a, apool, sh1, sc1, g1, norm1_g[l][None], w_gate[l], w_pool_b[l],
                                   pool_scale[l][None], w_a_up_b[l], w_b_up_b[l], w_out_b[l],
                                   sh2, sc2, norm2_g[l][None], w_router_p[l], b_router_p[l][None], seq)
        idx_l, wts, rank_l, counts = _route_call(logits)

        sizes = counts[0, :N_EXPERTS].astype(jnp.int32)
        padded = ((sizes + MOE_BLOCK - 1) // MOE_BLOCK) * MOE_BLOCK
        end_pad = jnp.cumsum(padded)
        start_pad = end_pad - padded
        dest = start_pad[idx_l[:, :TOP_K]] + rank_l[:, :TOP_K]
        dest_flat = dest.reshape(-1).astype(jnp.int32)
        blk_start = jnp.arange(n_blocks, dtype=jnp.int32) * MOE_BLOCK
        blk_e = jnp.minimum(jnp.searchsorted(end_pad, blk_start, side='right'), N_EXPERTS - 1).astype(jnp.int32)
        n_used = end_pad[-1] // MOE_BLOCK
        blk_idx = jnp.arange(n_blocks, dtype=jnp.int32)
        blk_on = (blk_idx < n_used).astype(jnp.int32)
        blk_src = jnp.minimum(blk_idx, n_used - 1).astype(jnp.int32)
        blk_e = jnp.where(blk_on == 1, blk_e, blk_e[n_used - 1])

        xs = _scatter_call(h2, dest_flat, end_pad.astype(jnp.int32), n_rows)
        ys = _expert_call(xs, blk_e, blk_src, blk_on, w_gu_b[l], b_gu[l][:, None, :], w_down_b[l], b_down[l][:, None, :])
        xf = _combine_call(x1, wts, g2, final_g[None], ys, dest_flat, seq, final_norm=(l == depth - 1))
    return xf.reshape(bsz, seq, d)
```

```python
import functools

import jax
import jax.numpy as jnp
from jax import lax
from jax.experimental import pallas as pl
from jax.experimental.pallas import tpu as pltpu

N_HEADS = 8
HEAD_DIM = 64
ATTN_WIDTH = N_HEADS * HEAD_DIM
KV_LATENT = 128
IDX_HEADS = 4
IDX_DIM = 32
TOPK_MAX = 256
ATTN_SCALE = HEAD_DIM ** -0.5
IDX_W_SCALE = (IDX_HEADS ** -0.5) * (IDX_DIM ** -0.5)
POOL_WINDOWS = (2, 4, 8, 16)
POOL_GROUP_DIM = 128
POOL_WIDTH = len(POOL_WINDOWS) * POOL_GROUP_DIM
N_EXPERTS = 32
TOP_K = 4
MOE_BLOCK = 512
SWIGLU_LIMIT = 7.0
SWIGLU_ALPHA = 1.702
RMS_EPS = 1e-6

LANES = 128
SUBLANES = 8
POOL_HALO = 16
DMA_PRIORITIES = 2
NEG_BIG = -1e30
INT_MIN = -2 ** 31
HALF_BITS = 16
HALF_OFFSET = 2 ** (HALF_BITS - 1)

TM_PROJ = 512
TQ = 512
TK = 512
GROUP_HEADS = 2
CNT_ROWS = 64
ONES_ROWS = 16
KVT_ROWS = KV_LATENT + ONES_ROWS
LOG2E = 1.4426950408889634
TM_MIX = 256
TM_ROUTE = 512
TM_SCATTER = 1024
TM_COMBINE = 512
VMEM_LIMIT = 56 * 1024 * 1024


def _cparams(sem):
    return pltpu.CompilerParams(dimension_semantics=sem, vmem_limit_bytes=VMEM_LIMIT)


def _rms(xf, g):
    return xf * lax.rsqrt(jnp.mean(xf * xf, axis=-1, keepdims=True) + RMS_EPS) * g


def _bf(v):
    return v.astype(jnp.bfloat16)


def _dot(a, b):
    return jnp.dot(a, b, preferred_element_type=jnp.float32)


def _dot_nt(a, b):
    return lax.dot_general(a, b, (((1,), (1,)), ((), ())), preferred_element_type=jnp.float32)


def _store_rows_as_tiles(ref, v):
    rows = v.shape[0]
    for s in range(SUBLANES):
        ref[pl.ds(s, rows, stride=SUBLANES), :] = v[:, s * LANES:(s + 1) * LANES]


def _tile_rows(ref, row, count):
    return ref.at[pl.ds(pl.multiple_of(row * SUBLANES, SUBLANES), count * SUBLANES), :]


def _load_rows_from_tiles(ref, rows):
    return jnp.concatenate([ref[pl.ds(s, rows, stride=SUBLANES), :] for s in range(SUBLANES)], axis=1)


def _strict_lower(n):
    return _bf(jnp.where(lax.broadcasted_iota(jnp.int32, (n, n), 1)
                         < lax.broadcasted_iota(jnp.int32, (n, n), 0), 1.0, 0.0))


def _ada_kernel(c_ref, w_ref, b_ref, o_ref):
    cf = c_ref[...]
    cond = cf * jax.nn.sigmoid(cf)
    o_ref[0] = _dot(_bf(cond), _bf(w_ref[0])) + b_ref[0]


def _ada_call(c, w_ada, b_ada):
    depth, d, n6 = w_ada.shape
    bsz = c.shape[0]
    tn = 1024
    return pl.pallas_call(
        _ada_kernel,
        grid=(depth, n6 // tn),
        in_specs=[
            pl.BlockSpec((bsz, d), lambda l, j: (0, 0)),
            pl.BlockSpec((1, d, tn), lambda l, j: (l, 0, j)),
            pl.BlockSpec((1, 1, tn), lambda l, j: (l, 0, j)),
        ],
        out_specs=pl.BlockSpec((1, bsz, tn), lambda l, j: (l, 0, j)),
        out_shape=jax.ShapeDtypeStruct((depth, bsz, n6), jnp.float32),
        compiler_params=_cparams(("arbitrary", "arbitrary")),
        name="ada",
    )(c, w_ada, b_ada.reshape(depth, 1, n6))


PROJ_COLS = ATTN_WIDTH + 3 * LANES + POOL_WIDTH


def _proj_kernel(x_ref, sh_ref, sc_ref, g_ref, w_ref, wwi_ref, kvg_ref,
                 q_ref, ckv_ref, ckvt_ref, qi_ref, ki_ref, wit_ref, ap_ref):
    h = _bf(_rms(x_ref[...], g_ref[...]) * (1.0 + sc_ref[0]) + sh_ref[0])
    p = _dot(h, w_ref[...])
    o = 0
    q_ref[...] = _bf(p[:, o:o + ATTN_WIDTH]); o += ATTN_WIDTH
    ckv = _bf(_rms(p[:, o:o + KV_LATENT], kvg_ref[...])); o += KV_LATENT
    ckv_ref[...] = ckv
    eye = _bf(jnp.where(lax.broadcasted_iota(jnp.int32, (KV_LATENT, KV_LATENT), 0)
                        == lax.broadcasted_iota(jnp.int32, (KV_LATENT, KV_LATENT), 1), 1.0, 0.0))
    ckvt_ref[0, :KV_LATENT, :] = _bf(_dot_nt(eye, ckv))
    ckvt_ref[0, KV_LATENT:, :] = jnp.ones((ONES_ROWS, TM_PROJ), jnp.bfloat16)
    qi_ref[...] = _bf(p[:, o:o + LANES]); o += LANES
    ki_ref[...] = _bf(p[:, o:o + LANES]); o += LANES
    ap_ref[...] = p[:, o:o + POOL_WIDTH]
    wit_ref[...] = _dot_nt(wwi_ref[...], h)


def _proj_call(xf, sh1, sc1, g1n, w_r, w_wi_t, kvg, seq):
    n, d = xf.shape
    tpb = seq // TM_PROJ
    row = lambda i: (i, 0)
    per_b = lambda i: (i // tpb, 0, 0)
    fixed = lambda i: (0, 0)
    return pl.pallas_call(
        _proj_kernel,
        grid=(n // TM_PROJ,),
        in_specs=[
            pl.BlockSpec((TM_PROJ, d), row),
            pl.BlockSpec((1, 1, d), per_b),
            pl.BlockSpec((1, 1, d), per_b),
            pl.BlockSpec((1, d), fixed),
            pl.BlockSpec((d, PROJ_COLS), fixed),
            pl.BlockSpec((SUBLANES, d), fixed),
            pl.BlockSpec((1, KV_LATENT), fixed),
        ],
        out_specs=[
            pl.BlockSpec((TM_PROJ, ATTN_WIDTH), row),
            pl.BlockSpec((TM_PROJ, KV_LATENT), row),
            pl.BlockSpec((1, KVT_ROWS, TM_PROJ), lambda i: (i, 0, 0)),
            pl.BlockSpec((TM_PROJ, LANES), row),
            pl.BlockSpec((TM_PROJ, LANES), row),
            pl.BlockSpec((SUBLANES, TM_PROJ), lambda i: (0, i)),
            pl.BlockSpec((TM_PROJ, POOL_WIDTH), row),
        ],
        out_shape=[
            jax.ShapeDtypeStruct((n, ATTN_WIDTH), jnp.bfloat16),
            jax.ShapeDtypeStruct((n, KV_LATENT), jnp.bfloat16),
            jax.ShapeDtypeStruct((n // TM_PROJ, KVT_ROWS, TM_PROJ), jnp.bfloat16),
            jax.ShapeDtypeStruct((n, LANES), jnp.bfloat16),
            jax.ShapeDtypeStruct((n, LANES), jnp.bfloat16),
            jax.ShapeDtypeStruct((SUBLANES, n), jnp.float32),
            jax.ShapeDtypeStruct((n, POOL_WIDTH), jnp.float32),
        ],
        compiler_params=_cparams(("arbitrary",)),
        name="in_proj",
    )(xf, sh1, sc1, g1n, w_r, w_wi_t, kvg)


def _attn_kernel(q_ref, qi_ref, wit_ref, ckv_ref, ckvt_ref, ki_ref, wukt_ref, wuv_ref, o_ref,
                 key_buf, hi_buf, lo_buf, qs_buf, acc_buf, m_buf, *, k_sel):
    qb = pl.program_id(1)
    q_start = qb * TQ
    n_kc = (q_start + TQ + TK - 1) // TK

    krow = lax.broadcasted_iota(jnp.int32, (TK, TQ), 0)
    qcol = q_start + lax.broadcasted_iota(jnp.int32, (TK, TQ), 1)

    qi = qi_ref[...]
    lane = lax.broadcasted_iota(jnp.int32, (TQ, LANES), 1)
    q4 = jnp.concatenate(
        [jnp.where((lane >= h * IDX_DIM) & (lane < (h + 1) * IDX_DIM), qi, jnp.zeros_like(qi))
         for h in range(IDX_HEADS)], axis=0)
    wit = wit_ref[...] * IDX_W_SCALE
    w_rows = [wit[h:h + 1, :] for h in range(IDX_HEADS)]

    def score_chunk(kc, carry):
        k0 = pl.multiple_of(kc * TK, TK)
        raw = _dot_nt(ki_ref[pl.ds(k0, TK), :], q4)
        score = w_rows[0] * jnp.maximum(raw[:, 0:TQ], 0.0)
        for h in range(1, IDX_HEADS):
            score = score + w_rows[h] * jnp.maximum(raw[:, h * TQ:(h + 1) * TQ], 0.0)
        score = score + 0.0
        bits = pltpu.bitcast(score, jnp.int32)
        key = jnp.where(bits < 0, bits ^ jnp.int32(0x7FFFFFFF), bits)
        key = jnp.where(krow + k0 <= qcol, key, jnp.int32(INT_MIN))
        key_buf[kc] = key
        hi_buf[kc] = (key >> HALF_BITS).astype(jnp.int16)
        return carry

    lax.fori_loop(0, n_kc, score_chunk, 0)

    def count16(plane, cand, strict):
        def body(kc, acc):
            for j in range(TK // CNT_ROWS):
                pj = plane[kc, j * CNT_ROWS:(j + 1) * CNT_ROWS, :]
                hit = (pj > cand) if strict else (pj >= cand)
                acc = acc + jnp.where(hit, jnp.bfloat16(1.0), jnp.bfloat16(0.0))
            return acc

        acc = lax.fori_loop(0, n_kc, body, jnp.zeros((CNT_ROWS, TQ), jnp.bfloat16))
        return jnp.sum(acc.astype(jnp.float32), axis=0, keepdims=True)

    def to_plane(t_u):
        return (t_u - HALF_OFFSET).astype(jnp.int16)

    def search16(plane, need):
        def bit_step(i, t_u):
            cand_u = t_u | lax.shift_left(jnp.int32(1), HALF_BITS - 1 - i)
            cnt = count16(plane, to_plane(cand_u), False)
            return jnp.where(cnt >= need, cand_u, t_u)

        return lax.fori_loop(0, HALF_BITS, bit_step, jnp.zeros((1, TQ), jnp.int32))

    hi_u = search16(hi_buf, float(k_sel))
    thr_hi = to_plane(hi_u)
    need_lo = float(k_sel) - count16(hi_buf, thr_hi, True)

    def low_plane_chunk(kc, carry):
        lo = ((key_buf[kc] & jnp.int32(HALF_OFFSET * 2 - 1)) - HALF_OFFSET).astype(jnp.int16)
        lo_buf[kc] = jnp.where(hi_buf[kc] == thr_hi, lo, jnp.int16(-HALF_OFFSET))
        return carry

    lax.fori_loop(0, n_kc, low_plane_chunk, 0)
    lo_u = search16(lo_buf, need_lo)
    thr = lax.shift_left(hi_u - HALF_OFFSET, HALF_BITS) | lo_u
    n_tie_take = need_lo - count16(lo_buf, to_plane(lo_u), True)

    qlat_t = _dot_nt(wukt_ref[...], q_ref[...]) * (ATTN_SCALE * LOG2E)
    for h in range(N_HEADS):
        qs_buf[:, h * TQ:(h + 1) * TQ] = _bf(qlat_t[h * KV_LATENT:(h + 1) * KV_LATENT, :])
    lower = _strict_lower(LANES)
    m_buf[...] = jnp.full(m_buf.shape, -3e38, jnp.float32)
    acc_buf[...] = jnp.zeros(acc_buf.shape, jnp.float32)

    def attend_chunk(kc, tie_seen):
        k0 = pl.multiple_of(kc * TK, TK)
        key = key_buf[kc]
        eq = key == thr
        eq_f = jnp.where(eq, 1.0, 0.0)
        eq_b = _bf(eq_f)
        ranks = []
        for j in range(TK // LANES):
            rows = slice(j * LANES, (j + 1) * LANES)
            ranks.append(_dot(lower, eq_b[rows]) + tie_seen)
            tie_seen = tie_seen + jnp.sum(eq_f[rows], axis=0, keepdims=True)
        tie_rank = jnp.concatenate(ranks, axis=0)
        sel = ((key > thr) | (eq & (tie_rank < n_tie_take))) & (krow + k0 <= qcol)
        bias = jnp.where(sel, 0.0, NEG_BIG)
        kv = ckv_ref[pl.ds(k0, TK), :]
        kv_t = ckvt_ref[kc]
        logits = _dot(kv, qs_buf[...])
        for g in range(N_HEADS // GROUP_HEADS):
            gcols = slice(g * GROUP_HEADS * TQ, (g + 1) * GROUP_HEADS * TQ)
            ps, alphas = [], []
            for hh in range(GROUP_HEADS):
                cols = slice((g * GROUP_HEADS + hh) * TQ, (g * GROUP_HEADS + hh + 1) * TQ)
                lg = logits[:, cols] + bias
                m_old = m_buf[:, cols]
                m_new = jnp.maximum(m_old, jnp.max(lg, axis=0, keepdims=True))
                m_buf[:, cols] = m_new
                alphas.append(jnp.exp2(m_old - m_new))
                ps.append(_bf(jnp.exp2(lg - m_new)))
            acc_buf[:, gcols] = (acc_buf[:, gcols] * jnp.concatenate(alphas, axis=1)
                                 + _dot(kv_t, jnp.concatenate(ps, axis=1)))
        return tie_seen

    lax.fori_loop(0, n_kc, attend_chunk, jnp.zeros((1, TQ), jnp.float32))

    o_lat_t = acc_buf[:KV_LATENT, :] / acc_buf[KV_LATENT:KV_LATENT + 1, :]
    stacked = jnp.concatenate(
        [o_lat_t[:, h * TQ:(h + 1) * TQ] for h in range(N_HEADS)], axis=0)
    o_ref[...] = _bf(_dot(_bf(stacked.T), wuv_ref[...]))


def _attn_call(q, qi, wi_t, ckv, ckv_t, ki4, wukt_bd, wuv_bd, bsz, seq):
    n = q.shape[0]
    nq = seq // TQ
    nkc = seq // TK
    k_sel = min(TOPK_MAX, seq // 4)
    qrow = lambda b, i: (b * nq + i, 0)
    per_b = lambda b, i: (b, 0)
    fixed = lambda b, i: (0, 0)
    return pl.pallas_call(
        functools.partial(_attn_kernel, k_sel=k_sel),
        grid=(bsz, nq),
        in_specs=[
            pl.BlockSpec((TQ, ATTN_WIDTH), qrow),
            pl.BlockSpec((TQ, LANES), qrow),
            pl.BlockSpec((SUBLANES, TQ), lambda b, i: (0, b * nq + i)),
            pl.BlockSpec((seq, KV_LATENT), per_b),
            pl.BlockSpec((nkc, KVT_ROWS, TK), lambda b, i: (b, 0, 0)),
            pl.BlockSpec((seq, LANES), per_b),
            pl.BlockSpec((N_HEADS * KV_LATENT, ATTN_WIDTH), fixed),
            pl.BlockSpec((N_HEADS * KV_LATENT, ATTN_WIDTH), fixed),
        ],
        out_specs=pl.BlockSpec((TQ, ATTN_WIDTH), qrow),
        out_shape=jax.ShapeDtypeStruct((n, ATTN_WIDTH), jnp.bfloat16),
        scratch_shapes=[
            pltpu.VMEM((nkc, TK, TQ), jnp.int32),
            pltpu.VMEM((nkc, TK, TQ), jnp.int16),
            pltpu.VMEM((nkc, TK, TQ), jnp.int16),
            pltpu.VMEM((KV_LATENT, N_HEADS * TQ), jnp.bfloat16),
            pltpu.VMEM((KVT_ROWS, N_HEADS * TQ), jnp.float32),
            pltpu.VMEM((1, N_HEADS * TQ), jnp.float32),
        ],
        compiler_params=_cparams(("arbitrary", "arbitrary")),
        name="dsa_attention",
    )(q, qi, wi_t, ckv, ckv_t, ki4, wukt_bd, wuv_bd)


def _mix_kernel(x_ref, oa_ref, ap_ref, halo_ref, sh1_ref, sc1_ref, g1_ref, n1g_ref,
                wgate_ref, wpool_ref, pscale_ref, waup_ref, wbup_ref, wout_ref,
                sh2_ref, sc2_ref, n2g_ref, wr_ref, br_ref,
                x1_ref, h2_ref, lg_ref, *, tiles_per_batch):
    i = pl.program_id(0)
    t_in_b = i % tiles_per_batch
    x = x_ref[...]
    d = x.shape[1]
    h = _rms(x, n1g_ref[...]) * (1.0 + sc1_ref[0]) + sh1_ref[0]
    gates = _dot(_bf(h), wgate_ref[...])

    pos1 = (t_in_b * TM_MIX + 1 + lax.broadcasted_iota(jnp.int32, (TM_MIX, 1), 0)).astype(jnp.float32)
    halo = jnp.where(t_in_b == 0, 0.0, halo_ref[...])
    zs = []
    for g, w in enumerate(POOL_WINDOWS):
        sl = slice(g * POOL_GROUP_DIM, (g + 1) * POOL_GROUP_DIM)
        a = ap_ref[:, sl]
        ext = jnp.concatenate([halo[:, sl], a], axis=0)
        span = 1
        while span < w:
            ext = ext + pltpu.roll(ext, span, 0)
            span *= 2
        mean = ext[POOL_HALO:] / jnp.minimum(pos1, float(w))
        zs.append(_dot(_bf(mean - a), wpool_ref[g]))
    o_b = jnp.concatenate(zs, axis=1) * pscale_ref[...]

    merged = (jax.nn.sigmoid(gates[:, :d]) * _dot(oa_ref[...], waup_ref[...])
              + jax.nn.sigmoid(gates[:, d:]) * _dot(_bf(o_b), wbup_ref[...]))
    x1 = x + g1_ref[0] * _dot(_bf(merged), wout_ref[...])
    x1_ref[...] = x1
    h2 = _rms(x1, n2g_ref[...]) * (1.0 + sc2_ref[0]) + sh2_ref[0]
    _store_rows_as_tiles(h2_ref, h2)
    lg_ref[...] = _dot(_bf(h2), wr_ref[...]) + br_ref[...]


def _mix_call(xf, o_a, apool, sh1, sc1, g1, n1g, wgate, wpool, pscale, waup, wbup, wout,
              sh2, sc2, n2g, wr, br, seq):
    n, d = xf.shape
    tpb = seq // TM_MIX
    row = lambda i: (i, 0)
    per_b = lambda i: (i // tpb, 0, 0)
    fixed = lambda i: (0, 0)
    fixed3 = lambda i: (0, 0, 0)
    halo_idx = lambda i: (jnp.maximum(i * (TM_MIX // POOL_HALO) - 1, 0), 0)
    mod = pl.BlockSpec((1, 1, d), per_b)
    return pl.pallas_call(
        functools.partial(_mix_kernel, tiles_per_batch=tpb),
        grid=(n // TM_MIX,),
        in_specs=[
            pl.BlockSpec((TM_MIX, d), row),
            pl.BlockSpec((TM_MIX, ATTN_WIDTH), row),
            pl.BlockSpec((TM_MIX, POOL_WIDTH), row),
            pl.BlockSpec((POOL_HALO, POOL_WIDTH), halo_idx),
            mod, mod, mod,
            pl.BlockSpec((1, d), fixed),
            pl.BlockSpec((d, 2 * d), fixed),
            pl.BlockSpec((len(POOL_WINDOWS), POOL_GROUP_DIM, POOL_GROUP_DIM), fixed3),
            pl.BlockSpec((1, POOL_WIDTH), fixed),
            pl.BlockSpec((ATTN_WIDTH, d), fixed),
            pl.BlockSpec((POOL_WIDTH, d), fixed),
            pl.BlockSpec((d, d), fixed),
            mod, mod,
            pl.BlockSpec((1, d), fixed),
            pl.BlockSpec((d, LANES), fixed),
            pl.BlockSpec((1, LANES), fixed),
        ],
        out_specs=[pl.BlockSpec((TM_MIX, d), row), pl.BlockSpec((TM_MIX * SUBLANES, LANES), row),
                   pl.BlockSpec((TM_MIX, LANES), row)],
        out_shape=[jax.ShapeDtypeStruct((n, d), jnp.float32),
                   jax.ShapeDtypeStruct((n * SUBLANES, LANES), jnp.float32),
                   jax.ShapeDtypeStruct((n, LANES), jnp.float32)],
        compiler_params=_cparams(("arbitrary",)),
        name="mix_merge",
    )(xf, o_a, apool, apool, sh1, sc1, g1, n1g, wgate, wpool, pscale, waup, wbup, wout,
      sh2, sc2, n2g, wr, br)


def _route_kernel(lg_ref, idx_ref, wt_ref, rank_ref, cnt_ref, carry):
    @pl.when(pl.program_id(0) == 0)
    def _():
        carry[...] = jnp.zeros(carry.shape, jnp.float32)

    work = lg_ref[...]
    lane = lax.broadcasted_iota(jnp.int32, work.shape, 1).astype(jnp.float32)
    vals, idxs = [], []
    for _ in range(TOP_K):
        m = jnp.max(work, axis=1, keepdims=True)
        idx = jnp.min(jnp.where(work == m, lane, float(LANES)), axis=1, keepdims=True)
        vals.append(m)
        idxs.append(idx)
        work = jnp.where(lane == idx, -jnp.inf, work)
    exps = [jnp.exp(v - vals[0]) for v in vals]
    denom = exps[0] + exps[1] + exps[2] + exps[3]

    onehot = jnp.zeros(work.shape, jnp.float32)
    for idx in idxs:
        onehot = onehot + jnp.where(lane == idx, 1.0, 0.0)
    before = _dot(_strict_lower(work.shape[0]), _bf(onehot)) + carry[...]

    idx_out = jnp.zeros(work.shape, jnp.int32)
    wt_out = jnp.zeros(work.shape, jnp.float32)
    rank_out = jnp.zeros(work.shape, jnp.int32)
    for k in range(TOP_K):
        rank_k = jnp.sum(jnp.where(lane == idxs[k], before, 0.0), axis=1, keepdims=True)
        idx_out = jnp.where(lane == k, idxs[k].astype(jnp.int32), idx_out)
        wt_out = jnp.where(lane == k, exps[k] / denom, wt_out)
        rank_out = jnp.where(lane == k, rank_k.astype(jnp.int32), rank_out)
    idx_ref[...] = idx_out
    wt_ref[...] = wt_out
    rank_ref[...] = rank_out
    carry[...] = carry[...] + jnp.sum(onehot, axis=0, keepdims=True)
    cnt_ref[...] = carry[...]


def _route_call(logits):
    n = logits.shape[0]
    row = lambda i: (i, 0)
    return pl.pallas_call(
        _route_kernel,
        grid=(n // TM_ROUTE,),
        in_specs=[pl.BlockSpec((TM_ROUTE, LANES), row)],
        out_specs=[pl.BlockSpec((TM_ROUTE, LANES), row)] * 3 + [pl.BlockSpec((1, LANES), lambda i: (0, 0))],
        out_shape=[jax.ShapeDtypeStruct((n, LANES), jnp.int32), jax.ShapeDtypeStruct((n, LANES), jnp.float32),
                   jax.ShapeDtypeStruct((n, LANES), jnp.int32), jax.ShapeDtypeStruct((1, LANES), jnp.float32)],
        scratch_shapes=[pltpu.VMEM((1, LANES), jnp.float32)],
        compiler_params=_cparams(("arbitrary",)),
        name="route",
    )(logits)


def _scatter_kernel(endpad_ref, dest_ref, h2_ref, xs_ref, zero_buf, sem_zero, sem_rows):
    @pl.when(pl.program_id(0) == 0)
    def _():
        zero_buf[...] = jnp.zeros(zero_buf.shape, zero_buf.dtype)

        def zero_block(start):
            return pltpu.make_async_copy(zero_buf, _tile_rows(xs_ref, start, MOE_BLOCK), sem_zero)

        def last_block_of(e):
            return jnp.maximum(endpad_ref[e] - MOE_BLOCK, 0)

        def start_zero(e, c):
            zero_block(last_block_of(e)).start()
            return c

        def wait_zero(e, c):
            zero_block(last_block_of(e)).wait()
            return c

        def start_tail(b, c):
            zero_block(b * MOE_BLOCK).start()
            return c

        def wait_tail(b, c):
            zero_block(b * MOE_BLOCK).wait()
            return c

        n_used = endpad_ref[N_EXPERTS - 1] // MOE_BLOCK
        n_blocks = xs_ref.shape[0] // (MOE_BLOCK * SUBLANES)
        lax.fori_loop(0, N_EXPERTS, start_zero, 0)
        lax.fori_loop(n_used, n_blocks, start_tail, 0)
        lax.fori_loop(0, N_EXPERTS, wait_zero, 0)
        lax.fori_loop(n_used, n_blocks, wait_tail, 0)

    def row_copy(r, k):
        return pltpu.make_async_copy(_tile_rows(h2_ref, r, 1),
                                     _tile_rows(xs_ref, dest_ref[r * TOP_K + k], 1), sem_rows)

    def start_row(r, c):
        for k in range(TOP_K):
            row_copy(r, k).start(priority=k % DMA_PRIORITIES)
        return c

    def wait_row(r, c):
        for k in range(TOP_K):
            row_copy(r, k).wait()
        return c

    lax.fori_loop(0, TM_SCATTER, start_row, 0)
    lax.fori_loop(0, TM_SCATTER, wait_row, 0)


def _scatter_call(h2, dest_flat, end_pad, n_rows):
    n = h2.shape[0] // SUBLANES
    grid_spec = pltpu.PrefetchScalarGridSpec(
        num_scalar_prefetch=1,
        grid=(n // TM_SCATTER,),
        in_specs=[
            pl.BlockSpec((TM_SCATTER * TOP_K,), lambda i, ep: (i,), memory_space=pltpu.SMEM),
            pl.BlockSpec((TM_SCATTER * SUBLANES, LANES), lambda i, ep: (i, 0)),
        ],
        out_specs=pl.BlockSpec(memory_space=pl.ANY),
        scratch_shapes=[pltpu.VMEM((MOE_BLOCK * SUBLANES, LANES), h2.dtype),
                        pltpu.SemaphoreType.DMA(()), pltpu.SemaphoreType.DMA(())],
    )
    return pl.pallas_call(
        _scatter_kernel,
        grid_spec=grid_spec,
        out_shape=jax.ShapeDtypeStruct((n_rows * SUBLANES, LANES), h2.dtype),
        compiler_params=_cparams(("arbitrary",)),
        name="moe_scatter",
    )(end_pad, dest_flat, h2)


def _expert_kernel(blk_e_ref, blk_src_ref, blk_on_ref, xs_ref, wgu_ref, bgu_ref, wdn_ref, bdn_ref, ys_ref,
                   wgu_b, wdn_b):
    i = pl.program_id(0)
    on = blk_on_ref[i] == 1

    @pl.when(jnp.logical_or(i == 0, blk_e_ref[i] != blk_e_ref[jnp.maximum(i - 1, 0)]))
    def _():
        wgu_b[...] = _bf(wgu_ref[0])
        wdn_b[...] = _bf(wdn_ref[0])

    @pl.when(jnp.logical_not(on))
    def _():
        ys_ref[...] = jnp.zeros(ys_ref.shape, ys_ref.dtype)

    @pl.when(on)
    def _():
        f = wdn_ref.shape[1]
        gu = _dot(_bf(_load_rows_from_tiles(xs_ref, MOE_BLOCK)), wgu_b[...]) + bgu_ref[0]
        gt = jnp.minimum(gu[:, :f], SWIGLU_LIMIT)
        up = jnp.clip(gu[:, f:], -SWIGLU_LIMIT, SWIGLU_LIMIT)
        act = gt * jax.nn.sigmoid(SWIGLU_ALPHA * gt) * (up + 1.0)
        _store_rows_as_tiles(ys_ref, _dot(_bf(act), wdn_b[...]) + bdn_ref[0])


def _expert_call(xs, blk_e, blk_src, blk_on, wgu, bgu, wdn, bdn):
    n_blocks = xs.shape[0] // (MOE_BLOCK * SUBLANES)
    d, f2 = wgu.shape[1], wgu.shape[2]
    f = wdn.shape[1]
    src = lambda i, be, bs, bo: (bs[i], 0)
    by_e = lambda i, be, bs, bo: (be[i], 0, 0)
    grid_spec = pltpu.PrefetchScalarGridSpec(
        num_scalar_prefetch=3,
        grid=(n_blocks,),
        in_specs=[
            pl.BlockSpec((MOE_BLOCK * SUBLANES, LANES), src),
            pl.BlockSpec((1, d, f2), by_e),
            pl.BlockSpec((1, 1, f2), by_e),
            pl.BlockSpec((1, f, d), by_e),
            pl.BlockSpec((1, 1, d), by_e),
        ],
        out_specs=pl.BlockSpec((MOE_BLOCK * SUBLANES, LANES), lambda i, be, bs, bo: (i, 0)),
        scratch_shapes=[pltpu.VMEM((d, f2), jnp.bfloat16), pltpu.VMEM((f, d), jnp.bfloat16)],
    )
    return pl.pallas_call(
        _expert_kernel,
        grid_spec=grid_spec,
        out_shape=jax.ShapeDtypeStruct(xs.shape, jnp.float32),
        compiler_params=_cparams(("arbitrary",)),
        name="moe_experts",
    )(blk_e, blk_src, blk_on, xs, wgu, bgu, wdn, bdn)


def _combine_kernel(dest_ref, x1_ref, wt_ref, g2_ref, fg_ref, ys_ref, o_ref, buf, sem, *, final_norm):
    def row_copy(r, k):
        return pltpu.make_async_copy(_tile_rows(ys_ref, dest_ref[r * TOP_K + k], 1),
                                     _tile_rows(buf.at[k], r, 1), sem)

    def start_row(r, c):
        for k in range(TOP_K):
            row_copy(r, k).start(priority=k % DMA_PRIORITIES)
        return c

    def wait_row(r, c):
        for k in range(TOP_K):
            row_copy(r, k).wait()
        return c

    lax.fori_loop(0, TM_COMBINE, start_row, 0)
    lax.fori_loop(0, TM_COMBINE, wait_row, 0)

    wt = wt_ref[...]
    y = wt[:, 0:1] * _load_rows_from_tiles(buf.at[0], TM_COMBINE)
    for k in range(1, TOP_K):
        y = y + wt[:, k:k + 1] * _load_rows_from_tiles(buf.at[k], TM_COMBINE)
    out = x1_ref[...] + g2_ref[0] * y
    if final_norm:
        out = _rms(out, fg_ref[...])
    o_ref[...] = out


def _combine_call(x1, wts, g2, final_g, ys, dest_flat, seq, final_norm):
    n, d = x1.shape
    tpb = seq // TM_COMBINE
    row = lambda i: (i, 0)
    return pl.pallas_call(
        functools.partial(_combine_kernel, final_norm=final_norm),
        grid=(n // TM_COMBINE,),
        in_specs=[
            pl.BlockSpec((TM_COMBINE * TOP_K,), lambda i: (i,), memory_space=pltpu.SMEM),
            pl.BlockSpec((TM_COMBINE, d), row),
            pl.BlockSpec((TM_COMBINE, LANES), row),
            pl.BlockSpec((1, 1, d), lambda i: (i // tpb, 0, 0)),
            pl.BlockSpec((1, d), lambda i: (0, 0)),
            pl.BlockSpec(memory_space=pl.ANY),
        ],
        out_specs=pl.BlockSpec((TM_COMBINE, d), row),
        out_shape=jax.ShapeDtypeStruct((n, d), jnp.float32),
        scratch_shapes=[pltpu.VMEM((TOP_K, TM_COMBINE * SUBLANES, LANES), jnp.float32),
                        pltpu.SemaphoreType.DMA(())],
        compiler_params=_cparams(("arbitrary",)),
        name="moe_combine",
    )(dest_flat, x1, wts, g2, final_g, ys)


def _regroup_w_in(w_in):
    o_q, o_kv = 0, ATTN_WIDTH
    o_qi = o_kv + KV_LATENT
    o_ki = o_qi + IDX_HEADS * IDX_DIM
    o_wi = o_ki + IDX_DIM
    o_pool = o_wi + IDX_HEADS
    o_ga = o_pool + POOL_WIDTH
    ki = w_in[..., o_ki:o_wi]
    w_r = jnp.concatenate([w_in[..., o_q:o_ki], ki, ki, ki, ki, w_in[..., o_pool:o_ga]], axis=-1)
    w_wi_t = jnp.pad(jnp.swapaxes(w_in[..., o_wi:o_pool], 1, 2), ((0, 0), (0, SUBLANES - IDX_HEADS), (0, 0)))
    return _bf(w_r), _bf(w_wi_t), _bf(w_in[..., o_ga:])


def _block_diag_uk_t(w_uk):
    depth = w_uk.shape[0]
    eye = jnp.eye(N_HEADS, dtype=w_uk.dtype)
    t = jnp.einsum('zlhd,hg->zglhd', w_uk, eye)
    return _bf(t.reshape(depth, N_HEADS * KV_LATENT, N_HEADS * HEAD_DIM))


def _block_diag_uv(w_uv):
    depth = w_uv.shape[0]
    eye = jnp.eye(N_HEADS, dtype=w_uv.dtype)
    t = jnp.einsum('zlhd,hg->zhlgd', w_uv, eye)
    return _bf(t.reshape(depth, N_HEADS * KV_LATENT, N_HEADS * HEAD_DIM))


def kernel(x, c, norm1_g, norm2_g, w_ada, b_ada, w_in, kv_norm_g, w_uk, w_uv, w_pool, pool_scale, w_a_up, w_b_up, w_out, w_router, b_router, w_gu, b_gu, w_down, b_down, final_g):
    bsz, seq, d = x.shape
    depth = w_in.shape[0]
    n = bsz * seq
    assert TM_PROJ == TK and seq % TK == 0 and n % MOE_BLOCK == 0
    assert seq // CNT_ROWS <= 256
    assert d == SUBLANES * LANES

    ada = _ada_call(c, w_ada, b_ada).reshape(depth, bsz, 6, 1, d)
    w_r, w_wi_t, w_gate = _regroup_w_in(w_in)
    wukt_bd = _block_diag_uk_t(w_uk)
    wuv_bd = _block_diag_uv(w_uv)
    w_pool_b, w_a_up_b, w_b_up_b, w_out_b = _bf(w_pool), _bf(w_a_up), _bf(w_b_up), _bf(w_out)
    w_router_p = _bf(jnp.pad(w_router, ((0, 0), (0, 0), (0, LANES - N_EXPERTS))))
    b_router_p = jnp.pad(b_router, ((0, 0), (0, LANES - N_EXPERTS)), constant_values=NEG_BIG)

    n_asg = n * TOP_K
    n_blocks = -(-n_asg // MOE_BLOCK) + N_EXPERTS
    n_rows = n_blocks * MOE_BLOCK

    xf = x.reshape(n, d)
    for l in range(depth):
        sh1, sc1, g1, sh2, sc2, g2 = [ada[l, :, j] for j in range(6)]
        q, ckv, ckv_t, qi, ki4, wi_t, apool = _proj_call(
            xf, sh1, sc1, norm1_g[l][None], w_r[l], w_wi_t[l], kv_norm_g[l][None], seq)
        o_a = _attn_call(q, qi, wi_t, ckv, ckv_t, ki4, wukt_bd[l], wuv_bd[l], bsz, seq)
        x1, h2, logits = _mix_call(xf, o_a, apool, sh1, sc1, g1, norm1_g[l][None], w_gate[l], w_pool_b[l],
                                   pool_scale[l][None], w_a_up_b[l], w_b_up_b[l], w_out_b[l],
                                   sh2, sc2, norm2_g[l][None], w_router_p[l], b_router_p[l][None], seq)
        idx_l, wts, rank_l, counts = _route_call(logits)

        sizes = counts[0, :N_EXPERTS].astype(jnp.int32)
        padded = ((sizes + MOE_BLOCK - 1) // MOE_BLOCK) * MOE_BLOCK
        end_pad = jnp.cumsum(padded)
        start_pad = end_pad - padded
        dest = start_pad[idx_l[:, :TOP_K]] + rank_l[:, :TOP_K]
        dest_flat = dest.reshape(-1).astype(jnp.int32)
        blk_start = jnp.arange(n_blocks, dtype=jnp.int32) * MOE_BLOCK
        blk_e = jnp.minimum(jnp.sum((end_pad[None, :] <= blk_start[:, None]).astype(jnp.int32), axis=1),
                            N_EXPERTS - 1)
        n_used = end_pad[-1] // MOE_BLOCK
        blk_idx = jnp.arange(n_blocks, dtype=jnp.int32)
        blk_on = (blk_idx < n_used).astype(jnp.int32)
        blk_src = jnp.minimum(blk_idx, n_used - 1).astype(jnp.int32)
        blk_e = jnp.where(blk_on == 1, blk_e, blk_e[n_used - 1])

        xs = _scatter_call(h2, dest_flat, end_pad.astype(jnp.int32), n_rows)
        ys = _expert_call(xs, blk_e, blk_src, blk_on, w_gu[l], b_gu[l][:, None, :], w_down[l], b_down[l][:, None, :])
        xf = _combine_call(x1, wts, g2, final_g[None], ys, dest_flat, seq, final_norm=(l == depth - 1))
    return xf.reshape(bsz, seq, d)
```

```python
import functools

import jax
import jax.numpy as jnp
from jax import lax
from jax.experimental import pallas as pl
from jax.experimental.pallas import tpu as pltpu

N_HEADS = 8
HEAD_DIM = 64
ATTN_WIDTH = N_HEADS * HEAD_DIM
KV_LATENT = 128
IDX_HEADS = 4
IDX_DIM = 32
TOPK_MAX = 256
ATTN_SCALE = HEAD_DIM ** -0.5
IDX_W_SCALE = (IDX_HEADS ** -0.5) * (IDX_DIM ** -0.5)
POOL_WINDOWS = (2, 4, 8, 16)
POOL_GROUP_DIM = 128
POOL_WIDTH = len(POOL_WINDOWS) * POOL_GROUP_DIM
N_EXPERTS = 32
TOP_K = 4
MOE_BLOCK = 512
SWIGLU_LIMIT = 7.0
SWIGLU_ALPHA = 1.702
RMS_EPS = 1e-6

LANES = 128
SUBLANES = 8
POOL_HALO = 16
DMA_PRIORITIES = 2
NEG_BIG = -1e30
INT_MIN = -2 ** 31
HALF_BITS = 16
HALF_OFFSET = 2 ** (HALF_BITS - 1)

TM_PROJ = 512
TQ = 512
TK = 512
GROUP_HEADS = 2
CNT_ROWS = 64
ONES_ROWS = 16
KVT_ROWS = KV_LATENT + ONES_ROWS
LOG2E = 1.4426950408889634
TM_MIX = 256
TM_ROUTE = 512
SEG_SHIFT = 4
SEG_ROWS = 1 << SEG_SHIFT
COMPACT_ROWS = TM_ROUTE * TOP_K + N_EXPERTS * SEG_ROWS
VMEM_LIMIT = 56 * 1024 * 1024


def _cparams(sem):
    return pltpu.CompilerParams(dimension_semantics=sem, vmem_limit_bytes=VMEM_LIMIT)


def _rms(xf, g):
    return xf * lax.rsqrt(jnp.mean(xf * xf, axis=-1, keepdims=True) + RMS_EPS) * g


def _bf(v):
    return v.astype(jnp.bfloat16)


def _dot(a, b):
    return jnp.dot(a, b, preferred_element_type=jnp.float32)


def _dot_nt(a, b):
    return lax.dot_general(a, b, (((1,), (1,)), ((), ())), preferred_element_type=jnp.float32)


def _store_rows_as_tiles(ref, v):
    rows = v.shape[0]
    for s in range(SUBLANES):
        ref[pl.ds(s, rows, stride=SUBLANES), :] = v[:, s * LANES:(s + 1) * LANES]


def _tile_rows(ref, row, count):
    return ref.at[pl.ds(pl.multiple_of(row * SUBLANES, SUBLANES), count * SUBLANES), :]


def _load_rows_from_tiles(ref, rows):
    return jnp.concatenate([ref[pl.ds(s, rows, stride=SUBLANES), :] for s in range(SUBLANES)], axis=1)


def _strict_lower(n):
    return _bf(jnp.where(lax.broadcasted_iota(jnp.int32, (n, n), 1)
                         < lax.broadcasted_iota(jnp.int32, (n, n), 0), 1.0, 0.0))


def _ada_kernel(c_ref, w_ref, b_ref, o_ref):
    cf = c_ref[...]
    cond = cf * jax.nn.sigmoid(cf)
    o_ref[0] = _dot(_bf(cond), _bf(w_ref[0])) + b_ref[0]


def _ada_call(c, w_ada, b_ada):
    depth, d, n6 = w_ada.shape
    bsz = c.shape[0]
    tn = 1024
    return pl.pallas_call(
        _ada_kernel,
        grid=(depth, n6 // tn),
        in_specs=[
            pl.BlockSpec((bsz, d), lambda l, j: (0, 0)),
            pl.BlockSpec((1, d, tn), lambda l, j: (l, 0, j)),
            pl.BlockSpec((1, 1, tn), lambda l, j: (l, 0, j)),
        ],
        out_specs=pl.BlockSpec((1, bsz, tn), lambda l, j: (l, 0, j)),
        out_shape=jax.ShapeDtypeStruct((depth, bsz, n6), jnp.float32),
        compiler_params=_cparams(("arbitrary", "arbitrary")),
        name="ada",
    )(c, w_ada, b_ada.reshape(depth, 1, n6))


PROJ_COLS = ATTN_WIDTH + 3 * LANES + POOL_WIDTH


def _proj_kernel(x_ref, sh_ref, sc_ref, g_ref, w_ref, wwi_ref, kvg_ref,
                 q_ref, ckv_ref, ckvt_ref, qi_ref, ki_ref, wit_ref, ap_ref):
    h = _bf(_rms(x_ref[...], g_ref[...]) * (1.0 + sc_ref[0]) + sh_ref[0])
    p = _dot(h, w_ref[...])
    o = 0
    q_ref[...] = _bf(p[:, o:o + ATTN_WIDTH]); o += ATTN_WIDTH
    ckv = _bf(_rms(p[:, o:o + KV_LATENT], kvg_ref[...])); o += KV_LATENT
    ckv_ref[...] = ckv
    eye = _bf(jnp.where(lax.broadcasted_iota(jnp.int32, (KV_LATENT, KV_LATENT), 0)
                        == lax.broadcasted_iota(jnp.int32, (KV_LATENT, KV_LATENT), 1), 1.0, 0.0))
    ckvt_ref[0, :KV_LATENT, :] = _bf(_dot_nt(eye, ckv))
    ckvt_ref[0, KV_LATENT:, :] = jnp.ones((ONES_ROWS, TM_PROJ), jnp.bfloat16)
    qi_ref[...] = _bf(p[:, o:o + LANES]); o += LANES
    ki_ref[...] = _bf(p[:, o:o + LANES]); o += LANES
    ap_ref[...] = p[:, o:o + POOL_WIDTH]
    wit_ref[...] = _dot_nt(wwi_ref[...], h)


def _proj_call(xf, sh1, sc1, g1n, w_r, w_wi_t, kvg, seq):
    n, d = xf.shape
    tpb = seq // TM_PROJ
    row = lambda i: (i, 0)
    per_b = lambda i: (i // tpb, 0, 0)
    fixed = lambda i: (0, 0)
    return pl.pallas_call(
        _proj_kernel,
        grid=(n // TM_PROJ,),
        in_specs=[
            pl.BlockSpec((TM_PROJ, d), row),
            pl.BlockSpec((1, 1, d), per_b),
            pl.BlockSpec((1, 1, d), per_b),
            pl.BlockSpec((1, d), fixed),
            pl.BlockSpec((d, PROJ_COLS), fixed),
            pl.BlockSpec((SUBLANES, d), fixed),
            pl.BlockSpec((1, KV_LATENT), fixed),
        ],
        out_specs=[
            pl.BlockSpec((TM_PROJ, ATTN_WIDTH), row),
            pl.BlockSpec((TM_PROJ, KV_LATENT), row),
            pl.BlockSpec((1, KVT_ROWS, TM_PROJ), lambda i: (i, 0, 0)),
            pl.BlockSpec((TM_PROJ, LANES), row),
            pl.BlockSpec((TM_PROJ, LANES), row),
            pl.BlockSpec((SUBLANES, TM_PROJ), lambda i: (0, i)),
            pl.BlockSpec((TM_PROJ, POOL_WIDTH), row),
        ],
        out_shape=[
            jax.ShapeDtypeStruct((n, ATTN_WIDTH), jnp.bfloat16),
            jax.ShapeDtypeStruct((n, KV_LATENT), jnp.bfloat16),
            jax.ShapeDtypeStruct((n // TM_PROJ, KVT_ROWS, TM_PROJ), jnp.bfloat16),
            jax.ShapeDtypeStruct((n, LANES), jnp.bfloat16),
            jax.ShapeDtypeStruct((n, LANES), jnp.bfloat16),
            jax.ShapeDtypeStruct((SUBLANES, n), jnp.float32),
            jax.ShapeDtypeStruct((n, POOL_WIDTH), jnp.float32),
        ],
        compiler_params=_cparams(("arbitrary",)),
        name="in_proj",
    )(xf, sh1, sc1, g1n, w_r, w_wi_t, kvg)


def _attn_kernel(q_ref, qi_ref, wit_ref, ckv_ref, ckvt_ref, ki_ref, wukt_ref, wuv_ref, o_ref,
                 key_buf, hi_buf, lo_buf, qs_buf, acc_buf, m_buf, *, k_sel):
    qb = pl.program_id(1)
    q_start = qb * TQ
    n_kc = (q_start + TQ + TK - 1) // TK

    krow = lax.broadcasted_iota(jnp.int32, (TK, TQ), 0)
    qcol = q_start + lax.broadcasted_iota(jnp.int32, (TK, TQ), 1)

    qi = qi_ref[...]
    lane = lax.broadcasted_iota(jnp.int32, (TQ, LANES), 1)
    q4 = jnp.concatenate(
        [jnp.where((lane >= h * IDX_DIM) & (lane < (h + 1) * IDX_DIM), qi, jnp.zeros_like(qi))
         for h in range(IDX_HEADS)], axis=0)
    wit = wit_ref[...] * IDX_W_SCALE
    w_rows = [wit[h:h + 1, :] for h in range(IDX_HEADS)]

    def score_chunk(kc, carry):
        k0 = pl.multiple_of(kc * TK, TK)
        raw = _dot_nt(ki_ref[pl.ds(k0, TK), :], q4)
        score = w_rows[0] * jnp.maximum(raw[:, 0:TQ], 0.0)
        for h in range(1, IDX_HEADS):
            score = score + w_rows[h] * jnp.maximum(raw[:, h * TQ:(h + 1) * TQ], 0.0)
        score = score + 0.0
        bits = pltpu.bitcast(score, jnp.int32)
        key = jnp.where(bits < 0, bits ^ jnp.int32(0x7FFFFFFF), bits)
        key = jnp.where(krow + k0 <= qcol, key, jnp.int32(INT_MIN))
        key_buf[kc] = key
        hi_buf[kc] = (key >> HALF_BITS).astype(jnp.int16)
        return carry

    lax.fori_loop(0, n_kc, score_chunk, 0)

    def count16(plane, cand, strict):
        def body(kc, acc):
            for j in range(TK // CNT_ROWS):
                pj = plane[kc, j * CNT_ROWS:(j + 1) * CNT_ROWS, :]
                hit = (pj > cand) if strict else (pj >= cand)
                acc = acc + jnp.where(hit, jnp.bfloat16(1.0), jnp.bfloat16(0.0))
            return acc

        acc = lax.fori_loop(0, n_kc, body, jnp.zeros((CNT_ROWS, TQ), jnp.bfloat16))
        return jnp.sum(acc.astype(jnp.float32), axis=0, keepdims=True)

    def to_plane(t_u):
        return (t_u - HALF_OFFSET).astype(jnp.int16)

    def search16(plane, need):
        def bit_step(i, t_u):
            cand_u = t_u | lax.shift_left(jnp.int32(1), HALF_BITS - 1 - i)
            cnt = count16(plane, to_plane(cand_u), False)
            return jnp.where(cnt >= need, cand_u, t_u)

        return lax.fori_loop(0, HALF_BITS, bit_step, jnp.zeros((1, TQ), jnp.int32))

    hi_u = search16(hi_buf, float(k_sel))
    thr_hi = to_plane(hi_u)
    need_lo = float(k_sel) - count16(hi_buf, thr_hi, True)

    def low_plane_chunk(kc, carry):
        lo = ((key_buf[kc] & jnp.int32(HALF_OFFSET * 2 - 1)) - HALF_OFFSET).astype(jnp.int16)
        lo_buf[kc] = jnp.where(hi_buf[kc] == thr_hi, lo, jnp.int16(-HALF_OFFSET))
        return carry

    lax.fori_loop(0, n_kc, low_plane_chunk, 0)
    lo_u = search16(lo_buf, need_lo)
    thr = lax.shift_left(hi_u - HALF_OFFSET, HALF_BITS) | lo_u
    n_tie_take = need_lo - count16(lo_buf, to_plane(lo_u), True)

    qlat_t = _dot_nt(wukt_ref[...], q_ref[...]) * (ATTN_SCALE * LOG2E)
    for h in range(N_HEADS):
        qs_buf[:, h * TQ:(h + 1) * TQ] = _bf(qlat_t[h * KV_LATENT:(h + 1) * KV_LATENT, :])
    lower = _strict_lower(LANES)
    m_buf[...] = jnp.full(m_buf.shape, -3e38, jnp.float32)
    acc_buf[...] = jnp.zeros(acc_buf.shape, jnp.float32)

    def attend_chunk(kc, tie_seen):
        k0 = pl.multiple_of(kc * TK, TK)
        key = key_buf[kc]
        eq = key == thr
        eq_f = jnp.where(eq, 1.0, 0.0)
        eq_b = _bf(eq_f)
        ranks = []
        for j in range(TK // LANES):
            rows = slice(j * LANES, (j + 1) * LANES)
            ranks.append(_dot(lower, eq_b[rows]) + tie_seen)
            tie_seen = tie_seen + jnp.sum(eq_f[rows], axis=0, keepdims=True)
        tie_rank = jnp.concatenate(ranks, axis=0)
        sel = ((key > thr) | (eq & (tie_rank < n_tie_take))) & (krow + k0 <= qcol)
        bias = jnp.where(sel, 0.0, NEG_BIG)
        kv = ckv_ref[pl.ds(k0, TK), :]
        kv_t = ckvt_ref[kc]
        logits = _dot(kv, qs_buf[...])
        for g in range(N_HEADS // GROUP_HEADS):
            gcols = slice(g * GROUP_HEADS * TQ, (g + 1) * GROUP_HEADS * TQ)
            ps, alphas = [], []
            for hh in range(GROUP_HEADS):
                cols = slice((g * GROUP_HEADS + hh) * TQ, (g * GROUP_HEADS + hh + 1) * TQ)
                lg = logits[:, cols] + bias
                m_old = m_buf[:, cols]
                m_new = jnp.maximum(m_old, jnp.max(lg, axis=0, keepdims=True))
                m_buf[:, cols] = m_new
                alphas.append(jnp.exp2(m_old - m_new))
                ps.append(_bf(jnp.exp2(lg - m_new)))
            acc_buf[:, gcols] = (acc_buf[:, gcols] * jnp.concatenate(alphas, axis=1)
                                 + _dot(kv_t, jnp.concatenate(ps, axis=1)))
        return tie_seen

    lax.fori_loop(0, n_kc, attend_chunk, jnp.zeros((1, TQ), jnp.float32))

    o_lat_t = acc_buf[:KV_LATENT, :] / acc_buf[KV_LATENT:KV_LATENT + 1, :]
    stacked = jnp.concatenate(
        [o_lat_t[:, h * TQ:(h + 1) * TQ] for h in range(N_HEADS)], axis=0)
    o_ref[...] = _bf(_dot(_bf(stacked.T), wuv_ref[...]))


def _attn_call(q, qi, wi_t, ckv, ckv_t, ki4, wukt_bd, wuv_bd, bsz, seq):
    n = q.shape[0]
    nq = seq // TQ
    nkc = seq // TK
    k_sel = min(TOPK_MAX, seq // 4)
    qrow = lambda b, i: (b * nq + i, 0)
    per_b = lambda b, i: (b, 0)
    fixed = lambda b, i: (0, 0)
    return pl.pallas_call(
        functools.partial(_attn_kernel, k_sel=k_sel),
        grid=(bsz, nq),
        in_specs=[
            pl.BlockSpec((TQ, ATTN_WIDTH), qrow),
            pl.BlockSpec((TQ, LANES), qrow),
            pl.BlockSpec((SUBLANES, TQ), lambda b, i: (0, b * nq + i)),
            pl.BlockSpec((seq, KV_LATENT), per_b),
            pl.BlockSpec((nkc, KVT_ROWS, TK), lambda b, i: (b, 0, 0)),
            pl.BlockSpec((seq, LANES), per_b),
            pl.BlockSpec((N_HEADS * KV_LATENT, ATTN_WIDTH), fixed),
            pl.BlockSpec((N_HEADS * KV_LATENT, ATTN_WIDTH), fixed),
        ],
        out_specs=pl.BlockSpec((TQ, ATTN_WIDTH), qrow),
        out_shape=jax.ShapeDtypeStruct((n, ATTN_WIDTH), jnp.bfloat16),
        scratch_shapes=[
            pltpu.VMEM((nkc, TK, TQ), jnp.int32),
            pltpu.VMEM((nkc, TK, TQ), jnp.int16),
            pltpu.VMEM((nkc, TK, TQ), jnp.int16),
            pltpu.VMEM((KV_LATENT, N_HEADS * TQ), jnp.bfloat16),
            pltpu.VMEM((KVT_ROWS, N_HEADS * TQ), jnp.float32),
            pltpu.VMEM((1, N_HEADS * TQ), jnp.float32),
        ],
        compiler_params=_cparams(("arbitrary", "arbitrary")),
        name="dsa_attention",
    )(q, qi, wi_t, ckv, ckv_t, ki4, wukt_bd, wuv_bd)


def _mix_kernel(x_ref, oa_ref, ap_ref, halo_ref, sh1_ref, sc1_ref, g1_ref, n1g_ref,
                wgate_ref, wpool_ref, pscale_ref, waup_ref, wbup_ref, wout_ref,
                sh2_ref, sc2_ref, n2g_ref, wr_ref, br_ref,
                x1_ref, h2_ref, lg_ref, *, tiles_per_batch):
    i = pl.program_id(0)
    t_in_b = i % tiles_per_batch
    x = x_ref[...]
    d = x.shape[1]
    h = _rms(x, n1g_ref[...]) * (1.0 + sc1_ref[0]) + sh1_ref[0]
    gates = _dot(_bf(h), wgate_ref[...])

    pos1 = (t_in_b * TM_MIX + 1 + lax.broadcasted_iota(jnp.int32, (TM_MIX, 1), 0)).astype(jnp.float32)
    halo = jnp.where(t_in_b == 0, 0.0, halo_ref[...])
    zs = []
    for g, w in enumerate(POOL_WINDOWS):
        sl = slice(g * POOL_GROUP_DIM, (g + 1) * POOL_GROUP_DIM)
        a = ap_ref[:, sl]
        ext = jnp.concatenate([halo[:, sl], a], axis=0)
        span = 1
        while span < w:
            ext = ext + pltpu.roll(ext, span, 0)
            span *= 2
        mean = ext[POOL_HALO:] / jnp.minimum(pos1, float(w))
        zs.append(_dot(_bf(mean - a), wpool_ref[g]))
    o_b = jnp.concatenate(zs, axis=1) * pscale_ref[...]

    merged = (jax.nn.sigmoid(gates[:, :d]) * _dot(oa_ref[...], waup_ref[...])
              + jax.nn.sigmoid(gates[:, d:]) * _dot(_bf(o_b), wbup_ref[...]))
    x1 = x + g1_ref[0] * _dot(_bf(merged), wout_ref[...])
    x1_ref[...] = x1
    h2 = _rms(x1, n2g_ref[...]) * (1.0 + sc2_ref[0]) + sh2_ref[0]
    _store_rows_as_tiles(h2_ref, h2)
    lg_ref[...] = _dot(_bf(h2), wr_ref[...]) + br_ref[...]


def _mix_call(xf, o_a, apool, sh1, sc1, g1, n1g, wgate, wpool, pscale, waup, wbup, wout,
              sh2, sc2, n2g, wr, br, seq):
    n, d = xf.shape
    tpb = seq // TM_MIX
    row = lambda i: (i, 0)
    per_b = lambda i: (i // tpb, 0, 0)
    fixed = lambda i: (0, 0)
    fixed3 = lambda i: (0, 0, 0)
    halo_idx = lambda i: (jnp.maximum(i * (TM_MIX // POOL_HALO) - 1, 0), 0)
    mod = pl.BlockSpec((1, 1, d), per_b)
    return pl.pallas_call(
        functools.partial(_mix_kernel, tiles_per_batch=tpb),
        grid=(n // TM_MIX,),
        in_specs=[
            pl.BlockSpec((TM_MIX, d), row),
            pl.BlockSpec((TM_MIX, ATTN_WIDTH), row),
            pl.BlockSpec((TM_MIX, POOL_WIDTH), row),
            pl.BlockSpec((POOL_HALO, POOL_WIDTH), halo_idx),
            mod, mod, mod,
            pl.BlockSpec((1, d), fixed),
            pl.BlockSpec((d, 2 * d), fixed),
            pl.BlockSpec((len(POOL_WINDOWS), POOL_GROUP_DIM, POOL_GROUP_DIM), fixed3),
            pl.BlockSpec((1, POOL_WIDTH), fixed),
            pl.BlockSpec((ATTN_WIDTH, d), fixed),
            pl.BlockSpec((POOL_WIDTH, d), fixed),
            pl.BlockSpec((d, d), fixed),
            mod, mod,
            pl.BlockSpec((1, d), fixed),
            pl.BlockSpec((d, LANES), fixed),
            pl.BlockSpec((1, LANES), fixed),
        ],
        out_specs=[pl.BlockSpec((TM_MIX, d), row), pl.BlockSpec((TM_MIX * SUBLANES, LANES), row),
                   pl.BlockSpec((TM_MIX, LANES), row)],
        out_shape=[jax.ShapeDtypeStruct((n, d), jnp.float32),
                   jax.ShapeDtypeStruct((n * SUBLANES, LANES), jnp.float32),
                   jax.ShapeDtypeStruct((n, LANES), jnp.float32)],
        compiler_params=_cparams(("arbitrary",)),
        name="mix_merge",
    )(xf, o_a, apool, apool, sh1, sc1, g1, n1g, wgate, wpool, pscale, waup, wbup, wout,
      sh2, sc2, n2g, wr, br)


def _route_kernel(lg_ref, wt_ref, lpos_ref, cnt_ref, off_ref, base_ref, tot_ref, carry):
    @pl.when(pl.program_id(0) == 0)
    def _():
        carry[...] = jnp.zeros(carry.shape, jnp.float32)

    work = lg_ref[...]
    lane = lax.broadcasted_iota(jnp.int32, work.shape, 1).astype(jnp.float32)
    vals, idxs = [], []
    for _ in range(TOP_K):
        m = jnp.max(work, axis=1, keepdims=True)
        idx = jnp.min(jnp.where(work == m, lane, float(LANES)), axis=1, keepdims=True)
        vals.append(m)
        idxs.append(idx)
        work = jnp.where(lane == idx, -jnp.inf, work)
    exps = [jnp.exp(v - vals[0]) for v in vals]
    denom = exps[0] + exps[1] + exps[2] + exps[3]

    onehot = jnp.zeros(work.shape, jnp.float32)
    for idx in idxs:
        onehot = onehot + jnp.where(lane == idx, 1.0, 0.0)
    tm = work.shape[0]
    earlier = _dot(_strict_lower(tm), _bf(onehot))
    count = jnp.sum(onehot, axis=0, keepdims=True)
    units = jnp.floor((count + (SEG_ROWS - 1)) * (1.0 / SEG_ROWS))
    upper = _bf(jnp.where(lax.broadcasted_iota(jnp.int32, (LANES, LANES), 0)
                          < lax.broadcasted_iota(jnp.int32, (LANES, LANES), 1), 1.0, 0.0))
    seg_off = _dot(_bf(jnp.broadcast_to(units, (SUBLANES, LANES))), upper)[0:1] * SEG_ROWS
    local = earlier + seg_off

    wt_out = jnp.zeros(work.shape, jnp.float32)
    lpos_out = jnp.zeros(work.shape, jnp.int32)
    for k in range(TOP_K):
        lpos_k = jnp.sum(jnp.where(lane == idxs[k], local, 0.0), axis=1, keepdims=True)
        wt_out = jnp.where(lane == k, exps[k] / denom, wt_out)
        lpos_out = jnp.where(lane == k, lpos_k.astype(jnp.int32), lpos_out)
    wt_ref[...] = wt_out
    lpos_ref[...] = lpos_out
    cnt_ref[0] = count.astype(jnp.int32)
    off_ref[0] = seg_off.astype(jnp.int32)
    base_ref[0] = carry[...].astype(jnp.int32)
    carry[...] = carry[...] + count
    tot_ref[...] = carry[...]


def _route_call(logits):
    n = logits.shape[0]
    n_tiles = n // TM_ROUTE
    row = lambda i: (i, 0)
    tile = lambda i: (i, 0, 0)
    table = jax.ShapeDtypeStruct((n_tiles, 1, LANES), jnp.int32)
    return pl.pallas_call(
        _route_kernel,
        grid=(n_tiles,),
        in_specs=[pl.BlockSpec((TM_ROUTE, LANES), row)],
        out_specs=[pl.BlockSpec((TM_ROUTE, LANES), row)] * 2 + [pl.BlockSpec((1, 1, LANES), tile)] * 3
        + [pl.BlockSpec((1, LANES), lambda i: (0, 0))],
        out_shape=[jax.ShapeDtypeStruct((n, LANES), jnp.float32), jax.ShapeDtypeStruct((n, LANES), jnp.int32),
                   table, table, table, jax.ShapeDtypeStruct((1, LANES), jnp.float32)],
        scratch_shapes=[pltpu.VMEM((1, LANES), jnp.float32)],
        compiler_params=_cparams(("arbitrary",)),
        name="route",
    )(logits)


def _segment_copies(cnt_ref, off_ref, gdst_ref, make_copy, act):
    def per_expert(e, c):
        def per_piece(p, c2):
            act(make_copy(off_ref[e] + p * SEG_ROWS, gdst_ref[e] + p * SEG_ROWS))
            return c2

        n_pieces = lax.shift_right_logical(cnt_ref[e] + (SEG_ROWS - 1), SEG_SHIFT)
        return lax.fori_loop(0, n_pieces, per_piece, c)

    lax.fori_loop(0, N_EXPERTS, per_expert, 0)


def _scatter_kernel(endpad_ref, lpos_ref, cnt_ref, off_ref, gdst_ref, h2_ref, xs_ref,
                    cbuf, zero_buf, sem_zero, sem_rows):
    @pl.when(pl.program_id(0) == 0)
    def _():
        zero_buf[...] = jnp.zeros(zero_buf.shape, zero_buf.dtype)
        cbuf[...] = jnp.zeros(cbuf.shape, cbuf.dtype)

        def zero_block(start):
            return pltpu.make_async_copy(zero_buf, _tile_rows(xs_ref, start, MOE_BLOCK), sem_zero)

        def last_block_of(e):
            return jnp.maximum(endpad_ref[e] - MOE_BLOCK, 0)

        def start_zero(e, c):
            zero_block(last_block_of(e)).start()
            return c

        def wait_zero(e, c):
            zero_block(last_block_of(e)).wait()
            return c

        def start_tail(b, c):
            zero_block(b * MOE_BLOCK).start()
            return c

        def wait_tail(b, c):
            zero_block(b * MOE_BLOCK).wait()
            return c

        n_used = endpad_ref[N_EXPERTS - 1] // MOE_BLOCK
        n_blocks = xs_ref.shape[0] // (MOE_BLOCK * SUBLANES)
        lax.fori_loop(0, N_EXPERTS, start_zero, 0)
        lax.fori_loop(n_used, n_blocks, start_tail, 0)
        lax.fori_loop(0, N_EXPERTS, wait_zero, 0)
        lax.fori_loop(n_used, n_blocks, wait_tail, 0)

    def place_token(t, c):
        row = _tile_rows(h2_ref, t, 1)[...]
        for k in range(TOP_K):
            _tile_rows(cbuf, lpos_ref[t * TOP_K + k], 1)[...] = row
        return c

    lax.fori_loop(0, TM_ROUTE, place_token, 0)

    def make_copy(local_row, global_row):
        return pltpu.make_async_copy(_tile_rows(cbuf, local_row, SEG_ROWS),
                                     _tile_rows(xs_ref, global_row, SEG_ROWS), sem_rows)

    _segment_copies(cnt_ref, off_ref, gdst_ref, make_copy, lambda cp: cp.start())
    _segment_copies(cnt_ref, off_ref, gdst_ref, make_copy, lambda cp: cp.wait())


def _tile_table_spec(index_map):
    return pl.BlockSpec((LANES,), index_map, memory_space=pltpu.SMEM)


def _scatter_call(h2, lpos_flat, cnt_flat, off_flat, gdst_flat, end_pad, n_rows):
    n = h2.shape[0] // SUBLANES
    by_tile = lambda i, ep: (i,)
    grid_spec = pltpu.PrefetchScalarGridSpec(
        num_scalar_prefetch=1,
        grid=(n // TM_ROUTE,),
        in_specs=[
            pl.BlockSpec((TM_ROUTE * TOP_K,), by_tile, memory_space=pltpu.SMEM),
            _tile_table_spec(by_tile), _tile_table_spec(by_tile), _tile_table_spec(by_tile),
            pl.BlockSpec((TM_ROUTE * SUBLANES, LANES), lambda i, ep: (i, 0)),
        ],
        out_specs=pl.BlockSpec(memory_space=pl.ANY),
        scratch_shapes=[pltpu.VMEM((COMPACT_ROWS * SUBLANES, LANES), h2.dtype),
                        pltpu.VMEM((MOE_BLOCK * SUBLANES, LANES), h2.dtype),
                        pltpu.SemaphoreType.DMA(()), pltpu.SemaphoreType.DMA(())],
    )
    return pl.pallas_call(
        _scatter_kernel,
        grid_spec=grid_spec,
        out_shape=jax.ShapeDtypeStruct((n_rows * SUBLANES, LANES), h2.dtype),
        compiler_params=_cparams(("arbitrary",)),
        name="moe_scatter",
    )(end_pad, lpos_flat, cnt_flat, off_flat, gdst_flat, h2)


def _expert_kernel(blk_e_ref, blk_src_ref, blk_on_ref, xs_ref, wgu_ref, bgu_ref, wdn_ref, bdn_ref, ys_ref,
                   wgu_b, wdn_b):
    i = pl.program_id(0)
    on = blk_on_ref[i] == 1

    @pl.when(jnp.logical_or(i == 0, blk_e_ref[i] != blk_e_ref[jnp.maximum(i - 1, 0)]))
    def _():
        wgu_b[...] = _bf(wgu_ref[0])
        wdn_b[...] = _bf(wdn_ref[0])

    @pl.when(jnp.logical_not(on))
    def _():
        ys_ref[...] = jnp.zeros(ys_ref.shape, ys_ref.dtype)

    @pl.when(on)
    def _():
        f = wdn_ref.shape[1]
        gu = _dot(_bf(_load_rows_from_tiles(xs_ref, MOE_BLOCK)), wgu_b[...]) + bgu_ref[0]
        gt = jnp.minimum(gu[:, :f], SWIGLU_LIMIT)
        up = jnp.clip(gu[:, f:], -SWIGLU_LIMIT, SWIGLU_LIMIT)
        act = gt * jax.nn.sigmoid(SWIGLU_ALPHA * gt) * (up + 1.0)
        _store_rows_as_tiles(ys_ref, _dot(_bf(act), wdn_b[...]) + bdn_ref[0])


def _expert_call(xs, blk_e, blk_src, blk_on, wgu, bgu, wdn, bdn):
    n_blocks = xs.shape[0] // (MOE_BLOCK * SUBLANES)
    d, f2 = wgu.shape[1], wgu.shape[2]
    f = wdn.shape[1]
    src = lambda i, be, bs, bo: (bs[i], 0)
    by_e = lambda i, be, bs, bo: (be[i], 0, 0)
    grid_spec = pltpu.PrefetchScalarGridSpec(
        num_scalar_prefetch=3,
        grid=(n_blocks,),
        in_specs=[
            pl.BlockSpec((MOE_BLOCK * SUBLANES, LANES), src),
            pl.BlockSpec((1, d, f2), by_e),
            pl.BlockSpec((1, 1, f2), by_e),
            pl.BlockSpec((1, f, d), by_e),
            pl.BlockSpec((1, 1, d), by_e),
        ],
        out_specs=pl.BlockSpec((MOE_BLOCK * SUBLANES, LANES), lambda i, be, bs, bo: (i, 0)),
        scratch_shapes=[pltpu.VMEM((d, f2), jnp.bfloat16), pltpu.VMEM((f, d), jnp.bfloat16)],
    )
    return pl.pallas_call(
        _expert_kernel,
        grid_spec=grid_spec,
        out_shape=jax.ShapeDtypeStruct(xs.shape, jnp.float32),
        compiler_params=_cparams(("arbitrary",)),
        name="moe_experts",
    )(blk_e, blk_src, blk_on, xs, wgu, bgu, wdn, bdn)


def _combine_kernel(lpos_ref, cnt_ref, off_ref, gdst_ref, x1_ref, wt_ref, g2_ref, fg_ref, ys_ref, o_ref,
                    ybuf, buf, sem, *, final_norm):
    def make_copy(local_row, global_row):
        return pltpu.make_async_copy(_tile_rows(ys_ref, global_row, SEG_ROWS),
                                     _tile_rows(ybuf, local_row, SEG_ROWS), sem)

    _segment_copies(cnt_ref, off_ref, gdst_ref, make_copy, lambda cp: cp.start())
    _segment_copies(cnt_ref, off_ref, gdst_ref, make_copy, lambda cp: cp.wait())

    def pick_token(t, c):
        for k in range(TOP_K):
            _tile_rows(buf.at[k], t, 1)[...] = _tile_rows(ybuf, lpos_ref[t * TOP_K + k], 1)[...]
        return c

    lax.fori_loop(0, TM_ROUTE, pick_token, 0)

    wt = wt_ref[...]
    y = wt[:, 0:1] * _load_rows_from_tiles(buf.at[0], TM_ROUTE)
    for k in range(1, TOP_K):
        y = y + wt[:, k:k + 1] * _load_rows_from_tiles(buf.at[k], TM_ROUTE)
    out = x1_ref[...] + g2_ref[0] * y
    if final_norm:
        out = _rms(out, fg_ref[...])
    o_ref[...] = out


def _combine_call(x1, wts, g2, final_g, ys, lpos_flat, cnt_flat, off_flat, gdst_flat, seq, final_norm):
    n, d = x1.shape
    tpb = seq // TM_ROUTE
    row = lambda i: (i, 0)
    by_tile = lambda i: (i,)
    return pl.pallas_call(
        functools.partial(_combine_kernel, final_norm=final_norm),
        grid=(n // TM_ROUTE,),
        in_specs=[
            pl.BlockSpec((TM_ROUTE * TOP_K,), by_tile, memory_space=pltpu.SMEM),
            _tile_table_spec(by_tile), _tile_table_spec(by_tile), _tile_table_spec(by_tile),
            pl.BlockSpec((TM_ROUTE, d), row),
            pl.BlockSpec((TM_ROUTE, LANES), row),
            pl.BlockSpec((1, 1, d), lambda i: (i // tpb, 0, 0)),
            pl.BlockSpec((1, d), lambda i: (0, 0)),
            pl.BlockSpec(memory_space=pl.ANY),
        ],
        out_specs=pl.BlockSpec((TM_ROUTE, d), row),
        out_shape=jax.ShapeDtypeStruct((n, d), jnp.float32),
        scratch_shapes=[pltpu.VMEM((COMPACT_ROWS * SUBLANES, LANES), jnp.float32),
                        pltpu.VMEM((TOP_K, TM_ROUTE * SUBLANES, LANES), jnp.float32),
                        pltpu.SemaphoreType.DMA(())],
        compiler_params=_cparams(("arbitrary",)),
        name="moe_combine",
    )(lpos_flat, cnt_flat, off_flat, gdst_flat, x1, wts, g2, final_g, ys)


def _regroup_w_in(w_in):
    o_q, o_kv = 0, ATTN_WIDTH
    o_qi = o_kv + KV_LATENT
    o_ki = o_qi + IDX_HEADS * IDX_DIM
    o_wi = o_ki + IDX_DIM
    o_pool = o_wi + IDX_HEADS
    o_ga = o_pool + POOL_WIDTH
    ki = w_in[..., o_ki:o_wi]
    w_r = jnp.concatenate([w_in[..., o_q:o_ki], ki, ki, ki, ki, w_in[..., o_pool:o_ga]], axis=-1)
    w_wi_t = jnp.pad(jnp.swapaxes(w_in[..., o_wi:o_pool], 1, 2), ((0, 0), (0, SUBLANES - IDX_HEADS), (0, 0)))
    return _bf(w_r), _bf(w_wi_t), _bf(w_in[..., o_ga:])


def _block_diag_uk_t(w_uk):
    depth = w_uk.shape[0]
    eye = jnp.eye(N_HEADS, dtype=w_uk.dtype)
    t = jnp.einsum('zlhd,hg->zglhd', w_uk, eye)
    return _bf(t.reshape(depth, N_HEADS * KV_LATENT, N_HEADS * HEAD_DIM))


def _block_diag_uv(w_uv):
    depth = w_uv.shape[0]
    eye = jnp.eye(N_HEADS, dtype=w_uv.dtype)
    t = jnp.einsum('zlhd,hg->zhlgd', w_uv, eye)
    return _bf(t.reshape(depth, N_HEADS * KV_LATENT, N_HEADS * HEAD_DIM))


def kernel(x, c, norm1_g, norm2_g, w_ada, b_ada, w_in, kv_norm_g, w_uk, w_uv, w_pool, pool_scale, w_a_up, w_b_up, w_out, w_router, b_router, w_gu, b_gu, w_down, b_down, final_g):
    bsz, seq, d = x.shape
    depth = w_in.shape[0]
    n = bsz * seq
    assert TM_PROJ == TK and seq % TK == 0 and n % MOE_BLOCK == 0
    assert seq // CNT_ROWS <= 256
    assert d == SUBLANES * LANES

    ada = _ada_call(c, w_ada, b_ada).reshape(depth, bsz, 6, 1, d)
    w_r, w_wi_t, w_gate = _regroup_w_in(w_in)
    wukt_bd = _block_diag_uk_t(w_uk)
    wuv_bd = _block_diag_uv(w_uv)
    w_pool_b, w_a_up_b, w_b_up_b, w_out_b = _bf(w_pool), _bf(w_a_up), _bf(w_b_up), _bf(w_out)
    w_router_p = _bf(jnp.pad(w_router, ((0, 0), (0, 0), (0, LANES - N_EXPERTS))))
    b_router_p = jnp.pad(b_router, ((0, 0), (0, LANES - N_EXPERTS)), constant_values=NEG_BIG)

    n_asg = n * TOP_K
    n_blocks = -(-(n_asg + N_EXPERTS * (SEG_ROWS - 1)) // MOE_BLOCK) + N_EXPERTS
    n_rows = n_blocks * MOE_BLOCK

    xf = x.reshape(n, d)
    for l in range(depth):
        sh1, sc1, g1, sh2, sc2, g2 = [ada[l, :, j] for j in range(6)]
        q, ckv, ckv_t, qi, ki4, wi_t, apool = _proj_call(
            xf, sh1, sc1, norm1_g[l][None], w_r[l], w_wi_t[l], kv_norm_g[l][None], seq)
        o_a = _attn_call(q, qi, wi_t, ckv, ckv_t, ki4, wukt_bd[l], wuv_bd[l], bsz, seq)
        x1, h2, logits = _mix_call(xf, o_a, apool, sh1, sc1, g1, norm1_g[l][None], w_gate[l], w_pool_b[l],
                                   pool_scale[l][None], w_a_up_b[l], w_b_up_b[l], w_out_b[l],
                                   sh2, sc2, norm2_g[l][None], w_router_p[l], b_router_p[l][None], seq)
        wts, lpos, cnt_t, off_t, base_t, counts = _route_call(logits)

        sizes = counts[0, :N_EXPERTS].astype(jnp.int32)
        padded = ((sizes + (SEG_ROWS - 1) + MOE_BLOCK - 1) // MOE_BLOCK) * MOE_BLOCK
        end_pad = jnp.cumsum(padded)
        start_pad = end_pad - padded
        gdst_t = base_t + jnp.pad(start_pad, (0, LANES - N_EXPERTS))[None, None, :]
        lpos_flat = lpos[:, :TOP_K].reshape(-1)
        cnt_flat, off_flat, gdst_flat = cnt_t.reshape(-1), off_t.reshape(-1), gdst_t.reshape(-1)
        blk_start = jnp.arange(n_blocks, dtype=jnp.int32) * MOE_BLOCK
        blk_e = jnp.minimum(jnp.sum((end_pad[None, :] <= blk_start[:, None]).astype(jnp.int32), axis=1),
                            N_EXPERTS - 1)
        n_used = end_pad[-1] // MOE_BLOCK
        blk_idx = jnp.arange(n_blocks, dtype=jnp.int32)
        blk_on = (blk_idx < n_used).astype(jnp.int32)
        blk_src = jnp.minimum(blk_idx, n_used - 1).astype(jnp.int32)
        blk_e = jnp.where(blk_on == 1, blk_e, blk_e[n_used - 1])

        xs = _scatter_call(h2, lpos_flat, cnt_flat, off_flat, gdst_flat, end_pad.astype(jnp.int32), n_rows)
        ys = _expert_call(xs, blk_e, blk_src, blk_on, w_gu[l], b_gu[l][:, None, :], w_down[l], b_down[l][:, None, :])
        xf = _combine_call(x1, wts, g2, final_g[None], ys, lpos_flat, cnt_flat, off_flat, gdst_flat, seq,
                           final_norm=(l == depth - 1))
    return xf.reshape(bsz, seq, d)
```

```python
import functools

import jax
import jax.numpy as jnp
from jax import lax
from jax.experimental import pallas as pl
from jax.experimental.pallas import tpu as pltpu

N_HEADS = 8
HEAD_DIM = 64
ATTN_WIDTH = N_HEADS * HEAD_DIM
KV_LATENT = 128
IDX_HEADS = 4
IDX_DIM = 32
TOPK_MAX = 256
ATTN_SCALE = HEAD_DIM ** -0.5
IDX_W_SCALE = (IDX_HEADS ** -0.5) * (IDX_DIM ** -0.5)
POOL_WINDOWS = (2, 4, 8, 16)
POOL_GROUP_DIM = 128
POOL_WIDTH = len(POOL_WINDOWS) * POOL_GROUP_DIM
N_EXPERTS = 32
TOP_K = 4
MOE_BLOCK = 512
SWIGLU_LIMIT = 7.0
SWIGLU_ALPHA = 1.702
RMS_EPS = 1e-6

LANES = 128
SUBLANES = 8
POOL_HALO = 16
DMA_PRIORITIES = 2
NEG_BIG = -1e30
INT_MIN = -2 ** 31
HALF_BITS = 16
HALF_OFFSET = 2 ** (HALF_BITS - 1)

TM_PROJ = 512
TQ = 512
TK = 512
GROUP_HEADS = 2
CNT_ROWS = 64
ONES_ROWS = 16
KVT_ROWS = KV_LATENT + ONES_ROWS
LOG2E = 1.4426950408889634
TM_MIX = 256
TM_ROUTE = 512
SEG_SHIFT = 4
SEG_ROWS = 1 << SEG_SHIFT
COMPACT_ROWS = TM_ROUTE * TOP_K + N_EXPERTS * SEG_ROWS
VMEM_LIMIT = 56 * 1024 * 1024


def _cparams(sem):
    return pltpu.CompilerParams(dimension_semantics=sem, vmem_limit_bytes=VMEM_LIMIT)


def _rms(xf, g):
    return xf * lax.rsqrt(jnp.mean(xf * xf, axis=-1, keepdims=True) + RMS_EPS) * g


def _bf(v):
    return v.astype(jnp.bfloat16)


def _dot(a, b):
    return jnp.dot(a, b, preferred_element_type=jnp.float32)


def _dot_nt(a, b):
    return lax.dot_general(a, b, (((1,), (1,)), ((), ())), preferred_element_type=jnp.float32)


def _store_rows_as_tiles(ref, v):
    rows = v.shape[0]
    for s in range(SUBLANES):
        ref[pl.ds(s, rows, stride=SUBLANES), :] = v[:, s * LANES:(s + 1) * LANES]


def _tile_rows(ref, row, count):
    return ref.at[pl.ds(pl.multiple_of(row * SUBLANES, SUBLANES), count * SUBLANES), :]


def _load_rows_from_tiles(ref, rows):
    return jnp.concatenate([ref[pl.ds(s, rows, stride=SUBLANES), :] for s in range(SUBLANES)], axis=1)


def _strict_lower(n):
    return _bf(jnp.where(lax.broadcasted_iota(jnp.int32, (n, n), 1)
                         < lax.broadcasted_iota(jnp.int32, (n, n), 0), 1.0, 0.0))


def _ada_kernel(c_ref, w_ref, b_ref, o_ref):
    cf = c_ref[...]
    cond = cf * jax.nn.sigmoid(cf)
    o_ref[0] = _dot(_bf(cond), _bf(w_ref[0])) + b_ref[0]


def _ada_call(c, w_ada, b_ada):
    depth, d, n6 = w_ada.shape
    bsz = c.shape[0]
    tn = 1024
    return pl.pallas_call(
        _ada_kernel,
        grid=(depth, n6 // tn),
        in_specs=[
            pl.BlockSpec((bsz, d), lambda l, j: (0, 0)),
            pl.BlockSpec((1, d, tn), lambda l, j: (l, 0, j)),
            pl.BlockSpec((1, 1, tn), lambda l, j: (l, 0, j)),
        ],
        out_specs=pl.BlockSpec((1, bsz, tn), lambda l, j: (l, 0, j)),
        out_shape=jax.ShapeDtypeStruct((depth, bsz, n6), jnp.float32),
        compiler_params=_cparams(("arbitrary", "arbitrary")),
        name="ada",
    )(c, w_ada, b_ada.reshape(depth, 1, n6))


PROJ_COLS = ATTN_WIDTH + 3 * LANES + POOL_WIDTH


def _proj_kernel(x_ref, sh_ref, sc_ref, g_ref, w_ref, wwi_ref, kvg_ref,
                 q_ref, ckv_ref, ckvt_ref, qi_ref, ki_ref, wit_ref, ap_ref):
    h = _bf(_rms(x_ref[...], g_ref[...]) * (1.0 + sc_ref[0]) + sh_ref[0])
    p = _dot(h, w_ref[...])
    o = 0
    q_ref[...] = _bf(p[:, o:o + ATTN_WIDTH]); o += ATTN_WIDTH
    ckv = _bf(_rms(p[:, o:o + KV_LATENT], kvg_ref[...])); o += KV_LATENT
    ckv_ref[...] = ckv
    eye = _bf(jnp.where(lax.broadcasted_iota(jnp.int32, (KV_LATENT, KV_LATENT), 0)
                        == lax.broadcasted_iota(jnp.int32, (KV_LATENT, KV_LATENT), 1), 1.0, 0.0))
    ckvt_ref[0, :KV_LATENT, :] = _bf(_dot_nt(eye, ckv))
    ckvt_ref[0, KV_LATENT:, :] = jnp.ones((ONES_ROWS, TM_PROJ), jnp.bfloat16)
    qi_ref[...] = _bf(p[:, o:o + LANES]); o += LANES
    ki_ref[...] = _bf(p[:, o:o + LANES]); o += LANES
    ap_ref[...] = p[:, o:o + POOL_WIDTH]
    wit_ref[...] = _dot_nt(wwi_ref[...], h)


def _proj_call(xf, sh1, sc1, g1n, w_r, w_wi_t, kvg, seq):
    n, d = xf.shape
    tpb = seq // TM_PROJ
    row = lambda i: (i, 0)
    per_b = lambda i: (i // tpb, 0, 0)
    fixed = lambda i: (0, 0)
    return pl.pallas_call(
        _proj_kernel,
        grid=(n // TM_PROJ,),
        in_specs=[
            pl.BlockSpec((TM_PROJ, d), row),
            pl.BlockSpec((1, 1, d), per_b),
            pl.BlockSpec((1, 1, d), per_b),
            pl.BlockSpec((1, d), fixed),
            pl.BlockSpec((d, PROJ_COLS), fixed),
            pl.BlockSpec((SUBLANES, d), fixed),
            pl.BlockSpec((1, KV_LATENT), fixed),
        ],
        out_specs=[
            pl.BlockSpec((TM_PROJ, ATTN_WIDTH), row),
            pl.BlockSpec((TM_PROJ, KV_LATENT), row),
            pl.BlockSpec((1, KVT_ROWS, TM_PROJ), lambda i: (i, 0, 0)),
            pl.BlockSpec((TM_PROJ, LANES), row),
            pl.BlockSpec((TM_PROJ, LANES), row),
            pl.BlockSpec((SUBLANES, TM_PROJ), lambda i: (0, i)),
            pl.BlockSpec((TM_PROJ, POOL_WIDTH), row),
        ],
        out_shape=[
            jax.ShapeDtypeStruct((n, ATTN_WIDTH), jnp.bfloat16),
            jax.ShapeDtypeStruct((n, KV_LATENT), jnp.bfloat16),
            jax.ShapeDtypeStruct((n // TM_PROJ, KVT_ROWS, TM_PROJ), jnp.bfloat16),
            jax.ShapeDtypeStruct((n, LANES), jnp.bfloat16),
            jax.ShapeDtypeStruct((n, LANES), jnp.bfloat16),
            jax.ShapeDtypeStruct((SUBLANES, n), jnp.float32),
            jax.ShapeDtypeStruct((n, POOL_WIDTH), jnp.float32),
        ],
        compiler_params=_cparams(("arbitrary",)),
        name="in_proj",
    )(xf, sh1, sc1, g1n, w_r, w_wi_t, kvg)


def _attn_kernel(q_ref, qi_ref, wit_ref, ckv_ref, ckvt_ref, ki_ref, wukt_ref, wuv_ref, o_ref,
                 key_buf, hi_buf, lo_buf, qs_buf, acc_buf, m_buf, *, k_sel):
    qb = pl.program_id(1)
    q_start = qb * TQ
    n_kc = (q_start + TQ + TK - 1) // TK

    krow = lax.broadcasted_iota(jnp.int32, (TK, TQ), 0)
    qcol = q_start + lax.broadcasted_iota(jnp.int32, (TK, TQ), 1)

    qi = qi_ref[...]
    lane = lax.broadcasted_iota(jnp.int32, (TQ, LANES), 1)
    q4 = jnp.concatenate(
        [jnp.where((lane >= h * IDX_DIM) & (lane < (h + 1) * IDX_DIM), qi, jnp.zeros_like(qi))
         for h in range(IDX_HEADS)], axis=0)
    wit = wit_ref[...] * IDX_W_SCALE
    w_rows = [wit[h:h + 1, :] for h in range(IDX_HEADS)]

    def score_chunk(kc, carry):
        k0 = pl.multiple_of(kc * TK, TK)
        raw = _dot_nt(ki_ref[pl.ds(k0, TK), :], q4)
        score = w_rows[0] * jnp.maximum(raw[:, 0:TQ], 0.0)
        for h in range(1, IDX_HEADS):
            score = score + w_rows[h] * jnp.maximum(raw[:, h * TQ:(h + 1) * TQ], 0.0)
        score = score + 0.0
        bits = pltpu.bitcast(score, jnp.int32)
        key = jnp.where(bits < 0, bits ^ jnp.int32(0x7FFFFFFF), bits)
        key = jnp.where(krow + k0 <= qcol, key, jnp.int32(INT_MIN))
        key_buf[kc] = key
        hi_buf[kc] = (key >> HALF_BITS).astype(jnp.int16)
        return carry

    lax.fori_loop(0, n_kc, score_chunk, 0)

    def count16(plane, cand, strict):
        def body(kc, acc):
            for j in range(TK // CNT_ROWS):
                pj = plane[kc, j * CNT_ROWS:(j + 1) * CNT_ROWS, :]
                hit = (pj > cand) if strict else (pj >= cand)
                acc = acc + jnp.where(hit, jnp.bfloat16(1.0), jnp.bfloat16(0.0))
            return acc

        acc = lax.fori_loop(0, n_kc, body, jnp.zeros((CNT_ROWS, TQ), jnp.bfloat16))
        return jnp.sum(acc.astype(jnp.float32), axis=0, keepdims=True)

    def to_plane(t_u):
        return (t_u - HALF_OFFSET).astype(jnp.int16)

    def search16(plane, need):
        def bit_step(i, t_u):
            cand_u = t_u | lax.shift_left(jnp.int32(1), HALF_BITS - 1 - i)
            cnt = count16(plane, to_plane(cand_u), False)
            return jnp.where(cnt >= need, cand_u, t_u)

        return lax.fori_loop(0, HALF_BITS, bit_step, jnp.zeros((1, TQ), jnp.int32))

    hi_u = search16(hi_buf, float(k_sel))
    thr_hi = to_plane(hi_u)
    need_lo = float(k_sel) - count16(hi_buf, thr_hi, True)

    def low_plane_chunk(kc, carry):
        lo = ((key_buf[kc] & jnp.int32(HALF_OFFSET * 2 - 1)) - HALF_OFFSET).astype(jnp.int16)
        lo_buf[kc] = jnp.where(hi_buf[kc] == thr_hi, lo, jnp.int16(-HALF_OFFSET))
        return carry

    lax.fori_loop(0, n_kc, low_plane_chunk, 0)
    lo_u = search16(lo_buf, need_lo)
    thr = lax.shift_left(hi_u - HALF_OFFSET, HALF_BITS) | lo_u
    n_tie_take = need_lo - count16(lo_buf, to_plane(lo_u), True)

    qlat_t = _dot_nt(wukt_ref[...], q_ref[...]) * (ATTN_SCALE * LOG2E)
    for h in range(N_HEADS):
        qs_buf[:, h * TQ:(h + 1) * TQ] = _bf(qlat_t[h * KV_LATENT:(h + 1) * KV_LATENT, :])
    lower = _strict_lower(LANES)
    m_buf[...] = jnp.full(m_buf.shape, -3e38, jnp.float32)
    acc_buf[...] = jnp.zeros(acc_buf.shape, jnp.float32)

    def attend_chunk(kc, tie_seen):
        k0 = pl.multiple_of(kc * TK, TK)
        key = key_buf[kc]
        eq = key == thr
        eq_f = jnp.where(eq, 1.0, 0.0)
        eq_b = _bf(eq_f)
        ranks = []
        for j in range(TK // LANES):
            rows = slice(j * LANES, (j + 1) * LANES)
            ranks.append(_dot(lower, eq_b[rows]) + tie_seen)
            tie_seen = tie_seen + jnp.sum(eq_f[rows], axis=0, keepdims=True)
        tie_rank = jnp.concatenate(ranks, axis=0)
        sel = ((key > thr) | (eq & (tie_rank < n_tie_take))) & (krow + k0 <= qcol)
        bias = jnp.where(sel, 0.0, NEG_BIG)
        kv = ckv_ref[pl.ds(k0, TK), :]
        kv_t = ckvt_ref[kc]
        logits = _dot(kv, qs_buf[...])
        for g in range(N_HEADS // GROUP_HEADS):
            gcols = slice(g * GROUP_HEADS * TQ, (g + 1) * GROUP_HEADS * TQ)
            ps, alphas = [], []
            for hh in range(GROUP_HEADS):
                cols = slice((g * GROUP_HEADS + hh) * TQ, (g * GROUP_HEADS + hh + 1) * TQ)
                lg = logits[:, cols] + bias
                m_old = m_buf[:, cols]
                m_new = jnp.maximum(m_old, jnp.max(lg, axis=0, keepdims=True))
                m_buf[:, cols] = m_new
                alphas.append(jnp.exp2(m_old - m_new))
                ps.append(_bf(jnp.exp2(lg - m_new)))
            acc_buf[:, gcols] = (acc_buf[:, gcols] * jnp.concatenate(alphas, axis=1)
                                 + _dot(kv_t, jnp.concatenate(ps, axis=1)))
        return tie_seen

    lax.fori_loop(0, n_kc, attend_chunk, jnp.zeros((1, TQ), jnp.float32))

    o_lat_t = acc_buf[:KV_LATENT, :] / acc_buf[KV_LATENT:KV_LATENT + 1, :]
    stacked = jnp.concatenate(
        [o_lat_t[:, h * TQ:(h + 1) * TQ] for h in range(N_HEADS)], axis=0)
    o_ref[...] = _bf(_dot(_bf(stacked.T), wuv_ref[...]))


def _attn_call(q, qi, wi_t, ckv, ckv_t, ki4, wukt_bd, wuv_bd, bsz, seq):
    n = q.shape[0]
    nq = seq // TQ
    nkc = seq // TK
    k_sel = min(TOPK_MAX, seq // 4)
    qrow = lambda b, i: (b * nq + i, 0)
    per_b = lambda b, i: (b, 0)
    fixed = lambda b, i: (0, 0)
    return pl.pallas_call(
        functools.partial(_attn_kernel, k_sel=k_sel),
        grid=(bsz, nq),
        in_specs=[
            pl.BlockSpec((TQ, ATTN_WIDTH), qrow),
            pl.BlockSpec((TQ, LANES), qrow),
            pl.BlockSpec((SUBLANES, TQ), lambda b, i: (0, b * nq + i)),
            pl.BlockSpec((seq, KV_LATENT), per_b),
            pl.BlockSpec((nkc, KVT_ROWS, TK), lambda b, i: (b, 0, 0)),
            pl.BlockSpec((seq, LANES), per_b),
            pl.BlockSpec((N_HEADS * KV_LATENT, ATTN_WIDTH), fixed),
            pl.BlockSpec((N_HEADS * KV_LATENT, ATTN_WIDTH), fixed),
        ],
        out_specs=pl.BlockSpec((TQ, ATTN_WIDTH), qrow),
        out_shape=jax.ShapeDtypeStruct((n, ATTN_WIDTH), jnp.bfloat16),
        scratch_shapes=[
            pltpu.VMEM((nkc, TK, TQ), jnp.int32),
            pltpu.VMEM((nkc, TK, TQ), jnp.int16),
            pltpu.VMEM((nkc, TK, TQ), jnp.int16),
            pltpu.VMEM((KV_LATENT, N_HEADS * TQ), jnp.bfloat16),
            pltpu.VMEM((KVT_ROWS, N_HEADS * TQ), jnp.float32),
            pltpu.VMEM((1, N_HEADS * TQ), jnp.float32),
        ],
        compiler_params=_cparams(("arbitrary", "arbitrary")),
        name="dsa_attention",
    )(q, qi, wi_t, ckv, ckv_t, ki4, wukt_bd, wuv_bd)


def _mix_kernel(x_ref, oa_ref, ap_ref, halo_ref, sh1_ref, sc1_ref, g1_ref, n1g_ref,
                wgate_ref, wpool_ref, pscale_ref, waup_ref, wbup_ref, wout_ref,
                sh2_ref, sc2_ref, n2g_ref, wr_ref, br_ref,
                x1_ref, h2_ref, lg_ref, *, tiles_per_batch):
    i = pl.program_id(0)
    t_in_b = i % tiles_per_batch
    x = x_ref[...]
    d = x.shape[1]
    h = _rms(x, n1g_ref[...]) * (1.0 + sc1_ref[0]) + sh1_ref[0]
    gates = _dot(_bf(h), wgate_ref[...])

    pos1 = (t_in_b * TM_MIX + 1 + lax.broadcasted_iota(jnp.int32, (TM_MIX, 1), 0)).astype(jnp.float32)
    halo = jnp.where(t_in_b == 0, 0.0, halo_ref[...])
    zs = []
    for g, w in enumerate(POOL_WINDOWS):
        sl = slice(g * POOL_GROUP_DIM, (g + 1) * POOL_GROUP_DIM)
        a = ap_ref[:, sl]
        ext = jnp.concatenate([halo[:, sl], a], axis=0)
        span = 1
        while span < w:
            ext = ext + pltpu.roll(ext, span, 0)
            span *= 2
        mean = ext[POOL_HALO:] / jnp.minimum(pos1, float(w))
        zs.append(_dot(_bf(mean - a), wpool_ref[g]))
    o_b = jnp.concatenate(zs, axis=1) * pscale_ref[...]

    merged = (jax.nn.sigmoid(gates[:, :d]) * _dot(oa_ref[...], waup_ref[...])
              + jax.nn.sigmoid(gates[:, d:]) * _dot(_bf(o_b), wbup_ref[...]))
    x1 = x + g1_ref[0] * _dot(_bf(merged), wout_ref[...])
    x1_ref[...] = x1
    h2 = _rms(x1, n2g_ref[...]) * (1.0 + sc2_ref[0]) + sh2_ref[0]
    _store_rows_as_tiles(h2_ref, h2)
    lg_ref[...] = _dot(_bf(h2), wr_ref[...]) + br_ref[...]


def _mix_call(xf, o_a, apool, sh1, sc1, g1, n1g, wgate, wpool, pscale, waup, wbup, wout,
              sh2, sc2, n2g, wr, br, seq):
    n, d = xf.shape
    tpb = seq // TM_MIX
    row = lambda i: (i, 0)
    per_b = lambda i: (i // tpb, 0, 0)
    fixed = lambda i: (0, 0)
    fixed3 = lambda i: (0, 0, 0)
    halo_idx = lambda i: (jnp.maximum(i * (TM_MIX // POOL_HALO) - 1, 0), 0)
    mod = pl.BlockSpec((1, 1, d), per_b)
    return pl.pallas_call(
        functools.partial(_mix_kernel, tiles_per_batch=tpb),
        grid=(n // TM_MIX,),
        in_specs=[
            pl.BlockSpec((TM_MIX, d), row),
            pl.BlockSpec((TM_MIX, ATTN_WIDTH), row),
            pl.BlockSpec((TM_MIX, POOL_WIDTH), row),
            pl.BlockSpec((POOL_HALO, POOL_WIDTH), halo_idx),
            mod, mod, mod,
            pl.BlockSpec((1, d), fixed),
            pl.BlockSpec((d, 2 * d), fixed),
            pl.BlockSpec((len(POOL_WINDOWS), POOL_GROUP_DIM, POOL_GROUP_DIM), fixed3),
            pl.BlockSpec((1, POOL_WIDTH), fixed),
            pl.BlockSpec((ATTN_WIDTH, d), fixed),
            pl.BlockSpec((POOL_WIDTH, d), fixed),
            pl.BlockSpec((d, d), fixed),
            mod, mod,
            pl.BlockSpec((1, d), fixed),
            pl.BlockSpec((d, LANES), fixed),
            pl.BlockSpec((1, LANES), fixed),
        ],
        out_specs=[pl.BlockSpec((TM_MIX, d), row), pl.BlockSpec((TM_MIX * SUBLANES, LANES), row),
                   pl.BlockSpec((TM_MIX, LANES), row)],
        out_shape=[jax.ShapeDtypeStruct((n, d), jnp.float32),
                   jax.ShapeDtypeStruct((n * SUBLANES, LANES), jnp.float32),
                   jax.ShapeDtypeStruct((n, LANES), jnp.float32)],
        compiler_params=_cparams(("arbitrary",)),
        name="mix_merge",
    )(xf, o_a, apool, apool, sh1, sc1, g1, n1g, wgate, wpool, pscale, waup, wbup, wout,
      sh2, sc2, n2g, wr, br)


def _route_kernel(lg_ref, wt_ref, lpos_ref, cnt_ref, off_ref, base_ref, tot_ref, carry):
    @pl.when(pl.program_id(0) == 0)
    def _():
        carry[...] = jnp.zeros(carry.shape, jnp.float32)

    work = lg_ref[...]
    lane = lax.broadcasted_iota(jnp.int32, work.shape, 1).astype(jnp.float32)
    vals, idxs = [], []
    for _ in range(TOP_K):
        m = jnp.max(work, axis=1, keepdims=True)
        idx = jnp.min(jnp.where(work == m, lane, float(LANES)), axis=1, keepdims=True)
        vals.append(m)
        idxs.append(idx)
        work = jnp.where(lane == idx, -jnp.inf, work)
    exps = [jnp.exp(v - vals[0]) for v in vals]
    denom = exps[0] + exps[1] + exps[2] + exps[3]

    onehot = jnp.zeros(work.shape, jnp.float32)
    for idx in idxs:
        onehot = onehot + jnp.where(lane == idx, 1.0, 0.0)
    tm = work.shape[0]
    earlier = _dot(_strict_lower(tm), _bf(onehot))
    count = jnp.sum(onehot, axis=0, keepdims=True)
    units = jnp.floor((count + (SEG_ROWS - 1)) * (1.0 / SEG_ROWS))
    upper = _bf(jnp.where(lax.broadcasted_iota(jnp.int32, (LANES, LANES), 0)
                          < lax.broadcasted_iota(jnp.int32, (LANES, LANES), 1), 1.0, 0.0))
    seg_off = _dot(_bf(jnp.broadcast_to(units, (SUBLANES, LANES))), upper)[0:1] * SEG_ROWS
    local = earlier + seg_off

    wt_out = jnp.zeros(work.shape, jnp.float32)
    lpos_out = jnp.zeros(work.shape, jnp.int32)
    for k in range(TOP_K):
        lpos_k = jnp.sum(jnp.where(lane == idxs[k], local, 0.0), axis=1, keepdims=True)
        wt_out = jnp.where(lane == k, exps[k] / denom, wt_out)
        lpos_out = jnp.where(lane == k, lpos_k.astype(jnp.int32), lpos_out)
    wt_ref[...] = wt_out
    lpos_ref[...] = lpos_out
    cnt_ref[0] = count.astype(jnp.int32)
    off_ref[0] = seg_off.astype(jnp.int32)
    base_ref[0] = carry[...].astype(jnp.int32)
    carry[...] = carry[...] + count
    tot_ref[...] = carry[...]


def _route_call(logits):
    n = logits.shape[0]
    n_tiles = n // TM_ROUTE
    row = lambda i: (i, 0)
    tile = lambda i: (i, 0, 0)
    table = jax.ShapeDtypeStruct((n_tiles, 1, LANES), jnp.int32)
    return pl.pallas_call(
        _route_kernel,
        grid=(n_tiles,),
        in_specs=[pl.BlockSpec((TM_ROUTE, LANES), row)],
        out_specs=[pl.BlockSpec((TM_ROUTE, LANES), row)] * 2 + [pl.BlockSpec((1, 1, LANES), tile)] * 3
        + [pl.BlockSpec((1, LANES), lambda i: (0, 0))],
        out_shape=[jax.ShapeDtypeStruct((n, LANES), jnp.float32), jax.ShapeDtypeStruct((n, LANES), jnp.int32),
                   table, table, table, jax.ShapeDtypeStruct((1, LANES), jnp.float32)],
        scratch_shapes=[pltpu.VMEM((1, LANES), jnp.float32)],
        compiler_params=_cparams(("arbitrary",)),
        name="route",
    )(logits)


def _segment_copies(cnt_ref, off_ref, gdst_ref, make_copy, act):
    def per_expert(e, c):
        def per_piece(p, c2):
            act(make_copy(off_ref[e] + p * SEG_ROWS, gdst_ref[e] + p * SEG_ROWS))
            return c2

        n_pieces = lax.shift_right_logical(cnt_ref[e] + (SEG_ROWS - 1), SEG_SHIFT)
        return lax.fori_loop(0, n_pieces, per_piece, c)

    lax.fori_loop(0, N_EXPERTS, per_expert, 0)


def _scatter_kernel(endpad_ref, lpos_ref, cnt_ref, off_ref, gdst_ref, cnt_prev, off_prev, gdst_prev,
                    h2_ref, xs_ref, cbuf, zero_buf, sem_zero, sem_rows):
    i = pl.program_id(0)
    slot = lax.rem(i, 2)

    @pl.when(i == 0)
    def _():
        zero_buf[...] = jnp.zeros(zero_buf.shape, zero_buf.dtype)
        cbuf[...] = jnp.zeros(cbuf.shape, cbuf.dtype)

        def zero_block(start):
            return pltpu.make_async_copy(zero_buf, _tile_rows(xs_ref, start, MOE_BLOCK), sem_zero)

        def last_block_of(e):
            return jnp.maximum(endpad_ref[e] - MOE_BLOCK, 0)

        def start_zero(e, c):
            zero_block(last_block_of(e)).start()
            return c

        def wait_zero(e, c):
            zero_block(last_block_of(e)).wait()
            return c

        def start_tail(b, c):
            zero_block(b * MOE_BLOCK).start()
            return c

        def wait_tail(b, c):
            zero_block(b * MOE_BLOCK).wait()
            return c

        n_used = endpad_ref[N_EXPERTS - 1] // MOE_BLOCK
        n_blocks = xs_ref.shape[0] // (MOE_BLOCK * SUBLANES)
        lax.fori_loop(0, N_EXPERTS, start_zero, 0)
        lax.fori_loop(n_used, n_blocks, start_tail, 0)
        lax.fori_loop(0, N_EXPERTS, wait_zero, 0)
        lax.fori_loop(n_used, n_blocks, wait_tail, 0)

    def place_token(t, c):
        row = _tile_rows(h2_ref, t, 1)[...]
        for k in range(TOP_K):
            _tile_rows(cbuf.at[slot], lpos_ref[t * TOP_K + k], 1)[...] = row
        return c

    lax.fori_loop(0, TM_ROUTE, place_token, 0)

    def send_from(s):
        def make_copy(local_row, global_row):
            return pltpu.make_async_copy(_tile_rows(cbuf.at[s], local_row, SEG_ROWS),
                                         _tile_rows(xs_ref, global_row, SEG_ROWS), sem_rows.at[s])
        return make_copy

    @pl.when(i > 0)
    def _():
        _segment_copies(cnt_prev, off_prev, gdst_prev, send_from(1 - slot), lambda cp: cp.wait())

    _segment_copies(cnt_ref, off_ref, gdst_ref, send_from(slot), lambda cp: cp.start())

    @pl.when(i + 1 == pl.num_programs(0))
    def _():
        _segment_copies(cnt_ref, off_ref, gdst_ref, send_from(slot), lambda cp: cp.wait())


def _tile_table_spec(index_map):
    return pl.BlockSpec((LANES,), index_map, memory_space=pltpu.SMEM)


def _scatter_call(h2, lpos_flat, cnt_flat, off_flat, gdst_flat, end_pad, n_rows):
    n = h2.shape[0] // SUBLANES
    by_tile = lambda i, ep: (i,)
    prev_tile = lambda i, ep: (jnp.maximum(i - 1, 0),)
    grid_spec = pltpu.PrefetchScalarGridSpec(
        num_scalar_prefetch=1,
        grid=(n // TM_ROUTE,),
        in_specs=[
            pl.BlockSpec((TM_ROUTE * TOP_K,), by_tile, memory_space=pltpu.SMEM),
            _tile_table_spec(by_tile), _tile_table_spec(by_tile), _tile_table_spec(by_tile),
            _tile_table_spec(prev_tile), _tile_table_spec(prev_tile), _tile_table_spec(prev_tile),
            pl.BlockSpec((TM_ROUTE * SUBLANES, LANES), lambda i, ep: (i, 0)),
        ],
        out_specs=pl.BlockSpec(memory_space=pl.ANY),
        scratch_shapes=[pltpu.VMEM((2, COMPACT_ROWS * SUBLANES, LANES), h2.dtype),
                        pltpu.VMEM((MOE_BLOCK * SUBLANES, LANES), h2.dtype),
                        pltpu.SemaphoreType.DMA(()), pltpu.SemaphoreType.DMA((2,))],
    )
    return pl.pallas_call(
        _scatter_kernel,
        grid_spec=grid_spec,
        out_shape=jax.ShapeDtypeStruct((n_rows * SUBLANES, LANES), h2.dtype),
        compiler_params=_cparams(("arbitrary",)),
        name="moe_scatter",
    )(end_pad, lpos_flat, cnt_flat, off_flat, gdst_flat, cnt_flat, off_flat, gdst_flat, h2)


def _expert_kernel(blk_e_ref, blk_src_ref, blk_on_ref, xs_ref, wgu_ref, bgu_ref, wdn_ref, bdn_ref, ys_ref,
                   wgu_b, wdn_b):
    i = pl.program_id(0)
    on = blk_on_ref[i] == 1

    @pl.when(jnp.logical_or(i == 0, blk_e_ref[i] != blk_e_ref[jnp.maximum(i - 1, 0)]))
    def _():
        wgu_b[...] = _bf(wgu_ref[0])
        wdn_b[...] = _bf(wdn_ref[0])

    @pl.when(jnp.logical_not(on))
    def _():
        ys_ref[...] = jnp.zeros(ys_ref.shape, ys_ref.dtype)

    @pl.when(on)
    def _():
        f = wdn_ref.shape[1]
        gu = _dot(_bf(_load_rows_from_tiles(xs_ref, MOE_BLOCK)), wgu_b[...]) + bgu_ref[0]
        gt = jnp.minimum(gu[:, :f], SWIGLU_LIMIT)
        up = jnp.clip(gu[:, f:], -SWIGLU_LIMIT, SWIGLU_LIMIT)
        act = gt * jax.nn.sigmoid(SWIGLU_ALPHA * gt) * (up + 1.0)
        _store_rows_as_tiles(ys_ref, _dot(_bf(act), wdn_b[...]) + bdn_ref[0])


def _expert_call(xs, blk_e, blk_src, blk_on, wgu, bgu, wdn, bdn):
    n_blocks = xs.shape[0] // (MOE_BLOCK * SUBLANES)
    d, f2 = wgu.shape[1], wgu.shape[2]
    f = wdn.shape[1]
    src = lambda i, be, bs, bo: (bs[i], 0)
    by_e = lambda i, be, bs, bo: (be[i], 0, 0)
    grid_spec = pltpu.PrefetchScalarGridSpec(
        num_scalar_prefetch=3,
        grid=(n_blocks,),
        in_specs=[
            pl.BlockSpec((MOE_BLOCK * SUBLANES, LANES), src),
            pl.BlockSpec((1, d, f2), by_e),
            pl.BlockSpec((1, 1, f2), by_e),
            pl.BlockSpec((1, f, d), by_e),
            pl.BlockSpec((1, 1, d), by_e),
        ],
        out_specs=pl.BlockSpec((MOE_BLOCK * SUBLANES, LANES), lambda i, be, bs, bo: (i, 0)),
        scratch_shapes=[pltpu.VMEM((d, f2), jnp.bfloat16), pltpu.VMEM((f, d), jnp.bfloat16)],
    )
    return pl.pallas_call(
        _expert_kernel,
        grid_spec=grid_spec,
        out_shape=jax.ShapeDtypeStruct(xs.shape, jnp.float32),
        compiler_params=_cparams(("arbitrary",)),
        name="moe_experts",
    )(blk_e, blk_src, blk_on, xs, wgu, bgu, wdn, bdn)


def _combine_kernel(lpos_ref, cnt_ref, off_ref, gdst_ref, cnt_next, off_next, gdst_next,
                    x1_ref, wt_ref, g2_ref, fg_ref, ys_ref, o_ref, ybuf, buf, sem, *, final_norm):
    i = pl.program_id(0)
    slot = lax.rem(i, 2)

    def fetch_into(s):
        def make_copy(local_row, global_row):
            return pltpu.make_async_copy(_tile_rows(ys_ref, global_row, SEG_ROWS),
                                         _tile_rows(ybuf.at[s], local_row, SEG_ROWS), sem.at[s])
        return make_copy

    @pl.when(i == 0)
    def _():
        _segment_copies(cnt_ref, off_ref, gdst_ref, fetch_into(slot), lambda cp: cp.start())

    @pl.when(i + 1 < pl.num_programs(0))
    def _():
        _segment_copies(cnt_next, off_next, gdst_next, fetch_into(1 - slot), lambda cp: cp.start())

    _segment_copies(cnt_ref, off_ref, gdst_ref, fetch_into(slot), lambda cp: cp.wait())

    def pick_token(t, c):
        for k in range(TOP_K):
            _tile_rows(buf.at[k], t, 1)[...] = _tile_rows(ybuf.at[slot], lpos_ref[t * TOP_K + k], 1)[...]
        return c

    lax.fori_loop(0, TM_ROUTE, pick_token, 0)

    wt = wt_ref[...]
    y = wt[:, 0:1] * _load_rows_from_tiles(buf.at[0], TM_ROUTE)
    for k in range(1, TOP_K):
        y = y + wt[:, k:k + 1] * _load_rows_from_tiles(buf.at[k], TM_ROUTE)
    out = x1_ref[...] + g2_ref[0] * y
    if final_norm:
        out = _rms(out, fg_ref[...])
    o_ref[...] = out


def _combine_call(x1, wts, g2, final_g, ys, lpos_flat, cnt_flat, off_flat, gdst_flat, seq, final_norm):
    n, d = x1.shape
    tpb = seq // TM_ROUTE
    n_tiles = n // TM_ROUTE
    row = lambda i: (i, 0)
    by_tile = lambda i: (i,)
    next_tile = lambda i: (jnp.minimum(i + 1, n_tiles - 1),)
    return pl.pallas_call(
        functools.partial(_combine_kernel, final_norm=final_norm),
        grid=(n_tiles,),
        in_specs=[
            pl.BlockSpec((TM_ROUTE * TOP_K,), by_tile, memory_space=pltpu.SMEM),
            _tile_table_spec(by_tile), _tile_table_spec(by_tile), _tile_table_spec(by_tile),
            _tile_table_spec(next_tile), _tile_table_spec(next_tile), _tile_table_spec(next_tile),
            pl.BlockSpec((TM_ROUTE, d), row),
            pl.BlockSpec((TM_ROUTE, LANES), row),
            pl.BlockSpec((1, 1, d), lambda i: (i // tpb, 0, 0)),
            pl.BlockSpec((1, d), lambda i: (0, 0)),
            pl.BlockSpec(memory_space=pl.ANY),
        ],
        out_specs=pl.BlockSpec((TM_ROUTE, d), row),
        out_shape=jax.ShapeDtypeStruct((n, d), jnp.float32),
        scratch_shapes=[pltpu.VMEM((2, COMPACT_ROWS * SUBLANES, LANES), jnp.float32),
                        pltpu.VMEM((TOP_K, TM_ROUTE * SUBLANES, LANES), jnp.float32),
                        pltpu.SemaphoreType.DMA((2,))],
        compiler_params=_cparams(("arbitrary",)),
        name="moe_combine",
    )(lpos_flat, cnt_flat, off_flat, gdst_flat, cnt_flat, off_flat, gdst_flat, x1, wts, g2, final_g, ys)


def _regroup_w_in(w_in):
    o_q, o_kv = 0, ATTN_WIDTH
    o_qi = o_kv + KV_LATENT
    o_ki = o_qi + IDX_HEADS * IDX_DIM
    o_wi = o_ki + IDX_DIM
    o_pool = o_wi + IDX_HEADS
    o_ga = o_pool + POOL_WIDTH
    ki = w_in[..., o_ki:o_wi]
    w_r = jnp.concatenate([w_in[..., o_q:o_ki], ki, ki, ki, ki, w_in[..., o_pool:o_ga]], axis=-1)
    w_wi_t = jnp.pad(jnp.swapaxes(w_in[..., o_wi:o_pool], 1, 2), ((0, 0), (0, SUBLANES - IDX_HEADS), (0, 0)))
    return _bf(w_r), _bf(w_wi_t), _bf(w_in[..., o_ga:])


def _block_diag_uk_t(w_uk):
    depth = w_uk.shape[0]
    eye = jnp.eye(N_HEADS, dtype=w_uk.dtype)
    t = jnp.einsum('zlhd,hg->zglhd', w_uk, eye)
    return _bf(t.reshape(depth, N_HEADS * KV_LATENT, N_HEADS * HEAD_DIM))


def _block_diag_uv(w_uv):
    depth = w_uv.shape[0]
    eye = jnp.eye(N_HEADS, dtype=w_uv.dtype)
    t = jnp.einsum('zlhd,hg->zhlgd', w_uv, eye)
    return _bf(t.reshape(depth, N_HEADS * KV_LATENT, N_HEADS * HEAD_DIM))


def kernel(x, c, norm1_g, norm2_g, w_ada, b_ada, w_in, kv_norm_g, w_uk, w_uv, w_pool, pool_scale, w_a_up, w_b_up, w_out, w_router, b_router, w_gu, b_gu, w_down, b_down, final_g):
    bsz, seq, d = x.shape
    depth = w_in.shape[0]
    n = bsz * seq
    assert TM_PROJ == TK and seq % TK == 0 and n % MOE_BLOCK == 0
    assert seq // CNT_ROWS <= 256
    assert d == SUBLANES * LANES

    ada = _ada_call(c, w_ada, b_ada).reshape(depth, bsz, 6, 1, d)
    w_r, w_wi_t, w_gate = _regroup_w_in(w_in)
    wukt_bd = _block_diag_uk_t(w_uk)
    wuv_bd = _block_diag_uv(w_uv)
    w_pool_b, w_a_up_b, w_b_up_b, w_out_b = _bf(w_pool), _bf(w_a_up), _bf(w_b_up), _bf(w_out)
    w_router_p = _bf(jnp.pad(w_router, ((0, 0), (0, 0), (0, LANES - N_EXPERTS))))
    b_router_p = jnp.pad(b_router, ((0, 0), (0, LANES - N_EXPERTS)), constant_values=NEG_BIG)

    n_asg = n * TOP_K
    n_blocks = -(-(n_asg + N_EXPERTS * (SEG_ROWS - 1)) // MOE_BLOCK) + N_EXPERTS
    n_rows = n_blocks * MOE_BLOCK

    xf = x.reshape(n, d)
    for l in range(depth):
        sh1, sc1, g1, sh2, sc2, g2 = [ada[l, :, j] for j in range(6)]
        q, ckv, ckv_t, qi, ki4, wi_t, apool = _proj_call(
            xf, sh1, sc1, norm1_g[l][None], w_r[l], w_wi_t[l], kv_norm_g[l][None], seq)
        o_a = _attn_call(q, qi, wi_t, ckv, ckv_t, ki4, wukt_bd[l], wuv_bd[l], bsz, seq)
        x1, h2, logits = _mix_call(xf, o_a, apool, sh1, sc1, g1, norm1_g[l][None], w_gate[l], w_pool_b[l],
                                   pool_scale[l][None], w_a_up_b[l], w_b_up_b[l], w_out_b[l],
                                   sh2, sc2, norm2_g[l][None], w_router_p[l], b_router_p[l][None], seq)
        wts, lpos, cnt_t, off_t, base_t, counts = _route_call(logits)

        sizes = counts[0, :N_EXPERTS].astype(jnp.int32)
        padded = ((sizes + (SEG_ROWS - 1) + MOE_BLOCK - 1) // MOE_BLOCK) * MOE_BLOCK
        end_pad = jnp.cumsum(padded)
        start_pad = end_pad - padded
        gdst_t = base_t + jnp.pad(start_pad, (0, LANES - N_EXPERTS))[None, None, :]
        lpos_flat = lpos[:, :TOP_K].reshape(-1)
        cnt_flat, off_flat, gdst_flat = cnt_t.reshape(-1), off_t.reshape(-1), gdst_t.reshape(-1)
        blk_start = jnp.arange(n_blocks, dtype=jnp.int32) * MOE_BLOCK
        blk_e = jnp.minimum(jnp.sum((end_pad[None, :] <= blk_start[:, None]).astype(jnp.int32), axis=1),
                            N_EXPERTS - 1)
        n_used = end_pad[-1] // MOE_BLOCK
        blk_idx = jnp.arange(n_blocks, dtype=jnp.int32)
        blk_on = (blk_idx < n_used).astype(jnp.int32)
        blk_src = jnp.minimum(blk_idx, n_used - 1).astype(jnp.int32)
        blk_e = jnp.where(blk_on == 1, blk_e, blk_e[n_used - 1])

        xs = _scatter_call(h2, lpos_flat, cnt_flat, off_flat, gdst_flat, end_pad.astype(jnp.int32), n_rows)
        ys = _expert_call(xs, blk_e, blk_src, blk_on, w_gu[l], b_gu[l][:, None, :], w_down[l], b_down[l][:, None, :])
        xf = _combine_call(x1, wts, g2, final_g[None], ys, lpos_flat, cnt_flat, off_flat, gdst_flat, seq,
                           final_norm=(l == depth - 1))
    return xf.reshape(bsz, seq, d)
```

```python
import functools

import jax
import jax.numpy as jnp
from jax import lax
from jax.experimental import pallas as pl
from jax.experimental.pallas import tpu as pltpu

N_HEADS = 8
HEAD_DIM = 64
ATTN_WIDTH = N_HEADS * HEAD_DIM
KV_LATENT = 128
IDX_HEADS = 4
IDX_DIM = 32
TOPK_MAX = 256
ATTN_SCALE = HEAD_DIM ** -0.5
IDX_W_SCALE = (IDX_HEADS ** -0.5) * (IDX_DIM ** -0.5)
POOL_WINDOWS = (2, 4, 8, 16)
POOL_GROUP_DIM = 128
POOL_WIDTH = len(POOL_WINDOWS) * POOL_GROUP_DIM
N_EXPERTS = 32
TOP_K = 4
MOE_BLOCK = 512
SWIGLU_LIMIT = 7.0
SWIGLU_ALPHA = 1.702
RMS_EPS = 1e-6

LANES = 128
SUBLANES = 8
POOL_HALO = 16
DMA_PRIORITIES = 2
NEG_BIG = -1e30
INT_MIN = -2 ** 31
HALF_BITS = 16
HALF_OFFSET = 2 ** (HALF_BITS - 1)

TM_PROJ = 512
TQ = 512
TK = 512
GROUP_HEADS = 2
CNT_ROWS = 64
ONES_ROWS = 16
KVT_ROWS = KV_LATENT + ONES_ROWS
LOG2E = 1.4426950408889634
TM_MIX = 256
TM_ROUTE = 512
SEG_SHIFT = 5
SEG_ROWS = 1 << SEG_SHIFT
COMPACT_ROWS = TM_ROUTE * TOP_K + N_EXPERTS * SEG_ROWS
VMEM_LIMIT = 56 * 1024 * 1024


def _cparams(sem):
    return pltpu.CompilerParams(dimension_semantics=sem, vmem_limit_bytes=VMEM_LIMIT)


def _rms(xf, g):
    return xf * lax.rsqrt(jnp.mean(xf * xf, axis=-1, keepdims=True) + RMS_EPS) * g


def _bf(v):
    return v.astype(jnp.bfloat16)


def _dot(a, b):
    return jnp.dot(a, b, preferred_element_type=jnp.float32)


def _dot_nt(a, b):
    return lax.dot_general(a, b, (((1,), (1,)), ((), ())), preferred_element_type=jnp.float32)


def _store_rows_as_tiles(ref, v):
    rows = v.shape[0]
    for s in range(SUBLANES):
        ref[pl.ds(s, rows, stride=SUBLANES), :] = v[:, s * LANES:(s + 1) * LANES]


def _tile_rows(ref, row, count):
    return ref.at[pl.ds(pl.multiple_of(row * SUBLANES, SUBLANES), count * SUBLANES), :]


def _load_rows_from_tiles(ref, rows):
    return jnp.concatenate([ref[pl.ds(s, rows, stride=SUBLANES), :] for s in range(SUBLANES)], axis=1)


def _strict_lower(n):
    return _bf(jnp.where(lax.broadcasted_iota(jnp.int32, (n, n), 1)
                         < lax.broadcasted_iota(jnp.int32, (n, n), 0), 1.0, 0.0))


def _ada_kernel(c_ref, w_ref, b_ref, o_ref):
    cf = c_ref[...]
    cond = cf * jax.nn.sigmoid(cf)
    o_ref[0] = _dot(_bf(cond), _bf(w_ref[0])) + b_ref[0]


def _ada_call(c, w_ada, b_ada):
    depth, d, n6 = w_ada.shape
    bsz = c.shape[0]
    tn = 1024
    return pl.pallas_call(
        _ada_kernel,
        grid=(depth, n6 // tn),
        in_specs=[
            pl.BlockSpec((bsz, d), lambda l, j: (0, 0)),
            pl.BlockSpec((1, d, tn), lambda l, j: (l, 0, j)),
            pl.BlockSpec((1, 1, tn), lambda l, j: (l, 0, j)),
        ],
        out_specs=pl.BlockSpec((1, bsz, tn), lambda l, j: (l, 0, j)),
        out_shape=jax.ShapeDtypeStruct((depth, bsz, n6), jnp.float32),
        compiler_params=_cparams(("arbitrary", "arbitrary")),
        name="ada",
    )(c, w_ada, b_ada.reshape(depth, 1, n6))


PROJ_COLS = ATTN_WIDTH + 3 * LANES + POOL_WIDTH


def _proj_kernel(x_ref, sh_ref, sc_ref, g_ref, w_ref, wwi_ref, kvg_ref,
                 q_ref, ckv_ref, ckvt_ref, qi_ref, ki_ref, wit_ref, ap_ref):
    h = _bf(_rms(x_ref[...], g_ref[...]) * (1.0 + sc_ref[0]) + sh_ref[0])
    p = _dot(h, w_ref[...])
    o = 0
    q_ref[...] = _bf(p[:, o:o + ATTN_WIDTH]); o += ATTN_WIDTH
    ckv = _bf(_rms(p[:, o:o + KV_LATENT], kvg_ref[...])); o += KV_LATENT
    ckv_ref[...] = ckv
    eye = _bf(jnp.where(lax.broadcasted_iota(jnp.int32, (KV_LATENT, KV_LATENT), 0)
                        == lax.broadcasted_iota(jnp.int32, (KV_LATENT, KV_LATENT), 1), 1.0, 0.0))
    ckvt_ref[0, :KV_LATENT, :] = _bf(_dot_nt(eye, ckv))
    ckvt_ref[0, KV_LATENT:, :] = jnp.ones((ONES_ROWS, TM_PROJ), jnp.bfloat16)
    qi_ref[...] = _bf(p[:, o:o + LANES]); o += LANES
    ki_ref[...] = _bf(p[:, o:o + LANES]); o += LANES
    ap_ref[...] = p[:, o:o + POOL_WIDTH]
    wit_ref[...] = _dot_nt(wwi_ref[...], h)


def _proj_call(xf, sh1, sc1, g1n, w_r, w_wi_t, kvg, seq):
    n, d = xf.shape
    tpb = seq // TM_PROJ
    row = lambda i: (i, 0)
    per_b = lambda i: (i // tpb, 0, 0)
    fixed = lambda i: (0, 0)
    return pl.pallas_call(
        _proj_kernel,
        grid=(n // TM_PROJ,),
        in_specs=[
            pl.BlockSpec((TM_PROJ, d), row),
            pl.BlockSpec((1, 1, d), per_b),
            pl.BlockSpec((1, 1, d), per_b),
            pl.BlockSpec((1, d), fixed),
            pl.BlockSpec((d, PROJ_COLS), fixed),
            pl.BlockSpec((SUBLANES, d), fixed),
            pl.BlockSpec((1, KV_LATENT), fixed),
        ],
        out_specs=[
            pl.BlockSpec((TM_PROJ, ATTN_WIDTH), row),
            pl.BlockSpec((TM_PROJ, KV_LATENT), row),
            pl.BlockSpec((1, KVT_ROWS, TM_PROJ), lambda i: (i, 0, 0)),
            pl.BlockSpec((TM_PROJ, LANES), row),
            pl.BlockSpec((TM_PROJ, LANES), row),
            pl.BlockSpec((SUBLANES, TM_PROJ), lambda i: (0, i)),
            pl.BlockSpec((TM_PROJ, POOL_WIDTH), row),
        ],
        out_shape=[
            jax.ShapeDtypeStruct((n, ATTN_WIDTH), jnp.bfloat16),
            jax.ShapeDtypeStruct((n, KV_LATENT), jnp.bfloat16),
            jax.ShapeDtypeStruct((n // TM_PROJ, KVT_ROWS, TM_PROJ), jnp.bfloat16),
            jax.ShapeDtypeStruct((n, LANES), jnp.bfloat16),
            jax.ShapeDtypeStruct((n, LANES), jnp.bfloat16),
            jax.ShapeDtypeStruct((SUBLANES, n), jnp.float32),
            jax.ShapeDtypeStruct((n, POOL_WIDTH), jnp.float32),
        ],
        compiler_params=_cparams(("arbitrary",)),
        name="in_proj",
    )(xf, sh1, sc1, g1n, w_r, w_wi_t, kvg)


def _attn_kernel(q_ref, qi_ref, wit_ref, ckv_ref, ckvt_ref, ki_ref, wukt_ref, wuv_ref, o_ref,
                 key_buf, hi_buf, lo_buf, qs_buf, acc_buf, m_buf, *, k_sel):
    qb = pl.program_id(1)
    q_start = qb * TQ
    n_kc = (q_start + TQ + TK - 1) // TK

    krow = lax.broadcasted_iota(jnp.int32, (TK, TQ), 0)
    qcol = q_start + lax.broadcasted_iota(jnp.int32, (TK, TQ), 1)

    qi = qi_ref[...]
    lane = lax.broadcasted_iota(jnp.int32, (TQ, LANES), 1)
    q4 = jnp.concatenate(
        [jnp.where((lane >= h * IDX_DIM) & (lane < (h + 1) * IDX_DIM), qi, jnp.zeros_like(qi))
         for h in range(IDX_HEADS)], axis=0)
    wit = wit_ref[...] * IDX_W_SCALE
    w_rows = [wit[h:h + 1, :] for h in range(IDX_HEADS)]

    def score_chunk(kc, carry):
        k0 = pl.multiple_of(kc * TK, TK)
        raw = _dot_nt(ki_ref[pl.ds(k0, TK), :], q4)
        score = w_rows[0] * jnp.maximum(raw[:, 0:TQ], 0.0)
        for h in range(1, IDX_HEADS):
            score = score + w_rows[h] * jnp.maximum(raw[:, h * TQ:(h + 1) * TQ], 0.0)
        score = score + 0.0
        bits = pltpu.bitcast(score, jnp.int32)
        key = jnp.where(bits < 0, bits ^ jnp.int32(0x7FFFFFFF), bits)
        key = jnp.where(krow + k0 <= qcol, key, jnp.int32(INT_MIN))
        key_buf[kc] = key
        hi_buf[kc] = (key >> HALF_BITS).astype(jnp.int16)
        return carry

    lax.fori_loop(0, n_kc, score_chunk, 0)

    def count16(plane, cand, strict):
        def body(kc, acc):
            for j in range(TK // CNT_ROWS):
                pj = plane[kc, j * CNT_ROWS:(j + 1) * CNT_ROWS, :]
                hit = (pj > cand) if strict else (pj >= cand)
                acc = acc + jnp.where(hit, jnp.bfloat16(1.0), jnp.bfloat16(0.0))
            return acc

        acc = lax.fori_loop(0, n_kc, body, jnp.zeros((CNT_ROWS, TQ), jnp.bfloat16))
        return jnp.sum(acc.astype(jnp.float32), axis=0, keepdims=True)

    def to_plane(t_u):
        return (t_u - HALF_OFFSET).astype(jnp.int16)

    def search16(plane, need):
        def bit_step(i, t_u):
            cand_u = t_u | lax.shift_left(jnp.int32(1), HALF_BITS - 1 - i)
            cnt = count16(plane, to_plane(cand_u), False)
            return jnp.where(cnt >= need, cand_u, t_u)

        return lax.fori_loop(0, HALF_BITS, bit_step, jnp.zeros((1, TQ), jnp.int32))

    hi_u = search16(hi_buf, float(k_sel))
    thr_hi = to_plane(hi_u)
    need_lo = float(k_sel) - count16(hi_buf, thr_hi, True)

    def low_plane_chunk(kc, carry):
        lo = ((key_buf[kc] & jnp.int32(HALF_OFFSET * 2 - 1)) - HALF_OFFSET).astype(jnp.int16)
        lo_buf[kc] = jnp.where(hi_buf[kc] == thr_hi, lo, jnp.int16(-HALF_OFFSET))
        return carry

    lax.fori_loop(0, n_kc, low_plane_chunk, 0)
    lo_u = search16(lo_buf, need_lo)
    thr = lax.shift_left(hi_u - HALF_OFFSET, HALF_BITS) | lo_u
    n_tie_take = need_lo - count16(lo_buf, to_plane(lo_u), True)

    qlat_t = _dot_nt(wukt_ref[...], q_ref[...]) * (ATTN_SCALE * LOG2E)
    for h in range(N_HEADS):
        qs_buf[:, h * TQ:(h + 1) * TQ] = _bf(qlat_t[h * KV_LATENT:(h + 1) * KV_LATENT, :])
    lower = _strict_lower(LANES)
    m_buf[...] = jnp.full(m_buf.shape, -3e38, jnp.float32)
    acc_buf[...] = jnp.zeros(acc_buf.shape, jnp.float32)

    def attend_chunk(kc, tie_seen):
        k0 = pl.multiple_of(kc * TK, TK)
        key = key_buf[kc]
        eq = key == thr
        eq_f = jnp.where(eq, 1.0, 0.0)
        eq_b = _bf(eq_f)
        ranks = []
        for j in range(TK // LANES):
            rows = slice(j * LANES, (j + 1) * LANES)
            ranks.append(_dot(lower, eq_b[rows]) + tie_seen)
            tie_seen = tie_seen + jnp.sum(eq_f[rows], axis=0, keepdims=True)
        tie_rank = jnp.concatenate(ranks, axis=0)
        sel = ((key > thr) | (eq & (tie_rank < n_tie_take))) & (krow + k0 <= qcol)
        bias = jnp.where(sel, 0.0, NEG_BIG)
        kv = ckv_ref[pl.ds(k0, TK), :]
        kv_t = ckvt_ref[kc]
        logits = _dot(kv, qs_buf[...])
        for g in range(N_HEADS // GROUP_HEADS):
            gcols = slice(g * GROUP_HEADS * TQ, (g + 1) * GROUP_HEADS * TQ)
            ps, alphas = [], []
            for hh in range(GROUP_HEADS):
                cols = slice((g * GROUP_HEADS + hh) * TQ, (g * GROUP_HEADS + hh + 1) * TQ)
                lg = logits[:, cols] + bias
                m_old = m_buf[:, cols]
                m_new = jnp.maximum(m_old, jnp.max(lg, axis=0, keepdims=True))
                m_buf[:, cols] = m_new
                alphas.append(jnp.exp2(m_old - m_new))
                ps.append(_bf(jnp.exp2(lg - m_new)))
            acc_buf[:, gcols] = (acc_buf[:, gcols] * jnp.concatenate(alphas, axis=1)
                                 + _dot(kv_t, jnp.concatenate(ps, axis=1)))
        return tie_seen

    lax.fori_loop(0, n_kc, attend_chunk, jnp.zeros((1, TQ), jnp.float32))

    o_lat_t = acc_buf[:KV_LATENT, :] / acc_buf[KV_LATENT:KV_LATENT + 1, :]
    stacked = jnp.concatenate(
        [o_lat_t[:, h * TQ:(h + 1) * TQ] for h in range(N_HEADS)], axis=0)
    o_ref[...] = _bf(_dot(_bf(stacked.T), wuv_ref[...]))


def _attn_call(q, qi, wi_t, ckv, ckv_t, ki4, wukt_bd, wuv_bd, bsz, seq):
    n = q.shape[0]
    nq = seq // TQ
    nkc = seq // TK
    k_sel = min(TOPK_MAX, seq // 4)
    qrow = lambda b, i: (b * nq + i, 0)
    per_b = lambda b, i: (b, 0)
    fixed = lambda b, i: (0, 0)
    return pl.pallas_call(
        functools.partial(_attn_kernel, k_sel=k_sel),
        grid=(bsz, nq),
        in_specs=[
            pl.BlockSpec((TQ, ATTN_WIDTH), qrow),
            pl.BlockSpec((TQ, LANES), qrow),
            pl.BlockSpec((SUBLANES, TQ), lambda b, i: (0, b * nq + i)),
            pl.BlockSpec((seq, KV_LATENT), per_b),
            pl.BlockSpec((nkc, KVT_ROWS, TK), lambda b, i: (b, 0, 0)),
            pl.BlockSpec((seq, LANES), per_b),
            pl.BlockSpec((N_HEADS * KV_LATENT, ATTN_WIDTH), fixed),
            pl.BlockSpec((N_HEADS * KV_LATENT, ATTN_WIDTH), fixed),
        ],
        out_specs=pl.BlockSpec((TQ, ATTN_WIDTH), qrow),
        out_shape=jax.ShapeDtypeStruct((n, ATTN_WIDTH), jnp.bfloat16),
        scratch_shapes=[
            pltpu.VMEM((nkc, TK, TQ), jnp.int32),
            pltpu.VMEM((nkc, TK, TQ), jnp.int16),
            pltpu.VMEM((nkc, TK, TQ), jnp.int16),
            pltpu.VMEM((KV_LATENT, N_HEADS * TQ), jnp.bfloat16),
            pltpu.VMEM((KVT_ROWS, N_HEADS * TQ), jnp.float32),
            pltpu.VMEM((1, N_HEADS * TQ), jnp.float32),
        ],
        compiler_params=_cparams(("arbitrary", "arbitrary")),
        name="dsa_attention",
    )(q, qi, wi_t, ckv, ckv_t, ki4, wukt_bd, wuv_bd)


def _mix_kernel(x_ref, oa_ref, ap_ref, halo_ref, sh1_ref, sc1_ref, g1_ref, n1g_ref,
                wgate_ref, wpool_ref, pscale_ref, waup_ref, wbup_ref, wout_ref,
                sh2_ref, sc2_ref, n2g_ref, wr_ref, br_ref,
                x1_ref, h2_ref, lg_ref, *, tiles_per_batch):
    i = pl.program_id(0)
    t_in_b = i % tiles_per_batch
    x = x_ref[...]
    d = x.shape[1]
    h = _rms(x, n1g_ref[...]) * (1.0 + sc1_ref[0]) + sh1_ref[0]
    gates = _dot(_bf(h), wgate_ref[...])

    pos1 = (t_in_b * TM_MIX + 1 + lax.broadcasted_iota(jnp.int32, (TM_MIX, 1), 0)).astype(jnp.float32)
    halo = jnp.where(t_in_b == 0, 0.0, halo_ref[...])
    zs = []
    for g, w in enumerate(POOL_WINDOWS):
        sl = slice(g * POOL_GROUP_DIM, (g + 1) * POOL_GROUP_DIM)
        a = ap_ref[:, sl]
        ext = jnp.concatenate([halo[:, sl], a], axis=0)
        span = 1
        while span < w:
            ext = ext + pltpu.roll(ext, span, 0)
            span *= 2
        mean = ext[POOL_HALO:] / jnp.minimum(pos1, float(w))
        zs.append(_dot(_bf(mean - a), wpool_ref[g]))
    o_b = jnp.concatenate(zs, axis=1) * pscale_ref[...]

    merged = (jax.nn.sigmoid(gates[:, :d]) * _dot(oa_ref[...], waup_ref[...])
              + jax.nn.sigmoid(gates[:, d:]) * _dot(_bf(o_b), wbup_ref[...]))
    x1 = x + g1_ref[0] * _dot(_bf(merged), wout_ref[...])
    x1_ref[...] = x1
    h2 = _rms(x1, n2g_ref[...]) * (1.0 + sc2_ref[0]) + sh2_ref[0]
    _store_rows_as_tiles(h2_ref, h2)
    lg_ref[...] = _dot(_bf(h2), wr_ref[...]) + br_ref[...]


def _mix_call(xf, o_a, apool, sh1, sc1, g1, n1g, wgate, wpool, pscale, waup, wbup, wout,
              sh2, sc2, n2g, wr, br, seq):
    n, d = xf.shape
    tpb = seq // TM_MIX
    row = lambda i: (i, 0)
    per_b = lambda i: (i // tpb, 0, 0)
    fixed = lambda i: (0, 0)
    fixed3 = lambda i: (0, 0, 0)
    halo_idx = lambda i: (jnp.maximum(i * (TM_MIX // POOL_HALO) - 1, 0), 0)
    mod = pl.BlockSpec((1, 1, d), per_b)
    return pl.pallas_call(
        functools.partial(_mix_kernel, tiles_per_batch=tpb),
        grid=(n // TM_MIX,),
        in_specs=[
            pl.BlockSpec((TM_MIX, d), row),
            pl.BlockSpec((TM_MIX, ATTN_WIDTH), row),
            pl.BlockSpec((TM_MIX, POOL_WIDTH), row),
            pl.BlockSpec((POOL_HALO, POOL_WIDTH), halo_idx),
            mod, mod, mod,
            pl.BlockSpec((1, d), fixed),
            pl.BlockSpec((d, 2 * d), fixed),
            pl.BlockSpec((len(POOL_WINDOWS), POOL_GROUP_DIM, POOL_GROUP_DIM), fixed3),
            pl.BlockSpec((1, POOL_WIDTH), fixed),
            pl.BlockSpec((ATTN_WIDTH, d), fixed),
            pl.BlockSpec((POOL_WIDTH, d), fixed),
            pl.BlockSpec((d, d), fixed),
            mod, mod,
            pl.BlockSpec((1, d), fixed),
            pl.BlockSpec((d, LANES), fixed),
            pl.BlockSpec((1, LANES), fixed),
        ],
        out_specs=[pl.BlockSpec((TM_MIX, d), row), pl.BlockSpec((TM_MIX * SUBLANES, LANES), row),
                   pl.BlockSpec((TM_MIX, LANES), row)],
        out_shape=[jax.ShapeDtypeStruct((n, d), jnp.float32),
                   jax.ShapeDtypeStruct((n * SUBLANES, LANES), jnp.float32),
                   jax.ShapeDtypeStruct((n, LANES), jnp.float32)],
        compiler_params=_cparams(("arbitrary",)),
        name="mix_merge",
    )(xf, o_a, apool, apool, sh1, sc1, g1, n1g, wgate, wpool, pscale, waup, wbup, wout,
      sh2, sc2, n2g, wr, br)


def _route_kernel(lg_ref, wt_ref, lpos_ref, cnt_ref, off_ref, base_ref, tot_ref, carry):
    @pl.when(pl.program_id(0) == 0)
    def _():
        carry[...] = jnp.zeros(carry.shape, jnp.float32)

    work = lg_ref[...]
    lane = lax.broadcasted_iota(jnp.int32, work.shape, 1).astype(jnp.float32)
    vals, idxs = [], []
    for _ in range(TOP_K):
        m = jnp.max(work, axis=1, keepdims=True)
        idx = jnp.min(jnp.where(work == m, lane, float(LANES)), axis=1, keepdims=True)
        vals.append(m)
        idxs.append(idx)
        work = jnp.where(lane == idx, -jnp.inf, work)
    exps = [jnp.exp(v - vals[0]) for v in vals]
    denom = exps[0] + exps[1] + exps[2] + exps[3]

    onehot = jnp.zeros(work.shape, jnp.float32)
    for idx in idxs:
        onehot = onehot + jnp.where(lane == idx, 1.0, 0.0)
    tm = work.shape[0]
    earlier = _dot(_strict_lower(tm), _bf(onehot))
    count = jnp.sum(onehot, axis=0, keepdims=True)
    units = jnp.floor((count + (SEG_ROWS - 1)) * (1.0 / SEG_ROWS))
    upper = _bf(jnp.where(lax.broadcasted_iota(jnp.int32, (LANES, LANES), 0)
                          < lax.broadcasted_iota(jnp.int32, (LANES, LANES), 1), 1.0, 0.0))
    seg_off = _dot(_bf(jnp.broadcast_to(units, (SUBLANES, LANES))), upper)[0:1] * SEG_ROWS
    local = earlier + seg_off

    wt_out = jnp.zeros(work.shape, jnp.float32)
    lpos_out = jnp.zeros(work.shape, jnp.int32)
    for k in range(TOP_K):
        lpos_k = jnp.sum(jnp.where(lane == idxs[k], local, 0.0), axis=1, keepdims=True)
        wt_out = jnp.where(lane == k, exps[k] / denom, wt_out)
        lpos_out = jnp.where(lane == k, lpos_k.astype(jnp.int32), lpos_out)
    wt_ref[...] = wt_out
    lpos_ref[...] = lpos_out
    cnt_ref[0] = count.astype(jnp.int32)
    off_ref[0] = seg_off.astype(jnp.int32)
    base_ref[0] = carry[...].astype(jnp.int32)
    carry[...] = carry[...] + count
    tot_ref[...] = carry[...]


def _route_call(logits):
    n = logits.shape[0]
    n_tiles = n // TM_ROUTE
    row = lambda i: (i, 0)
    tile = lambda i: (i, 0, 0)
    table = jax.ShapeDtypeStruct((n_tiles, 1, LANES), jnp.int32)
    return pl.pallas_call(
        _route_kernel,
        grid=(n_tiles,),
        in_specs=[pl.BlockSpec((TM_ROUTE, LANES), row)],
        out_specs=[pl.BlockSpec((TM_ROUTE, LANES), row)] * 2 + [pl.BlockSpec((1, 1, LANES), tile)] * 3
        + [pl.BlockSpec((1, LANES), lambda i: (0, 0))],
        out_shape=[jax.ShapeDtypeStruct((n, LANES), jnp.float32), jax.ShapeDtypeStruct((n, LANES), jnp.int32),
                   table, table, table, jax.ShapeDtypeStruct((1, LANES), jnp.float32)],
        scratch_shapes=[pltpu.VMEM((1, LANES), jnp.float32)],
        compiler_params=_cparams(("arbitrary",)),
        name="route",
    )(logits)


def _segment_copies(cnt_ref, off_ref, gdst_ref, make_copy, act):
    for e in range(N_EXPERTS):
        def per_piece(p, c, e=e):
            act(make_copy(off_ref[e] + p * SEG_ROWS, gdst_ref[e] + p * SEG_ROWS), e)
            return c

        n_pieces = lax.shift_right_logical(cnt_ref[e] + (SEG_ROWS - 1), SEG_SHIFT)
        lax.fori_loop(0, n_pieces, per_piece, 0)


def _start_copy(copy, expert):
    copy.start(priority=expert % DMA_PRIORITIES)


def _wait_copy(copy, expert):
    copy.wait()


def _scatter_kernel(endpad_ref, lpos_ref, cnt_ref, off_ref, gdst_ref, cnt_prev, off_prev, gdst_prev,
                    h2_ref, xs_ref, cbuf, zero_buf, sem_zero, sem_rows):
    i = pl.program_id(0)
    slot = lax.rem(i, 2)

    @pl.when(i == 0)
    def _():
        zero_buf[...] = jnp.zeros(zero_buf.shape, zero_buf.dtype)
        cbuf[...] = jnp.zeros(cbuf.shape, cbuf.dtype)

        def zero_block(start):
            return pltpu.make_async_copy(zero_buf, _tile_rows(xs_ref, start, MOE_BLOCK), sem_zero)

        def last_block_of(e):
            return jnp.maximum(endpad_ref[e] - MOE_BLOCK, 0)

        def start_zero(e, c):
            zero_block(last_block_of(e)).start()
            return c

        def wait_zero(e, c):
            zero_block(last_block_of(e)).wait()
            return c

        def start_tail(b, c):
            zero_block(b * MOE_BLOCK).start()
            return c

        def wait_tail(b, c):
            zero_block(b * MOE_BLOCK).wait()
            return c

        n_used = endpad_ref[N_EXPERTS - 1] // MOE_BLOCK
        n_blocks = xs_ref.shape[0] // (MOE_BLOCK * SUBLANES)
        lax.fori_loop(0, N_EXPERTS, start_zero, 0)
        lax.fori_loop(n_used, n_blocks, start_tail, 0)
        lax.fori_loop(0, N_EXPERTS, wait_zero, 0)
        lax.fori_loop(n_used, n_blocks, wait_tail, 0)

    def place_token(t, c):
        row = _tile_rows(h2_ref, t, 1)[...]
        for k in range(TOP_K):
            _tile_rows(cbuf.at[slot], lpos_ref[t * TOP_K + k], 1)[...] = row
        return c

    lax.fori_loop(0, TM_ROUTE, place_token, 0)

    def send_from(s):
        def make_copy(local_row, global_row):
            return pltpu.make_async_copy(_tile_rows(cbuf.at[s], local_row, SEG_ROWS),
                                         _tile_rows(xs_ref, global_row, SEG_ROWS), sem_rows.at[s])
        return make_copy

    @pl.when(i > 0)
    def _():
        _segment_copies(cnt_prev, off_prev, gdst_prev, send_from(1 - slot), _wait_copy)

    _segment_copies(cnt_ref, off_ref, gdst_ref, send_from(slot), _start_copy)

    @pl.when(i + 1 == pl.num_programs(0))
    def _():
        _segment_copies(cnt_ref, off_ref, gdst_ref, send_from(slot), _wait_copy)


def _tile_table_spec(index_map):
    return pl.BlockSpec((LANES,), index_map, memory_space=pltpu.SMEM)


def _scatter_call(h2, lpos_flat, cnt_flat, off_flat, gdst_flat, end_pad, n_rows):
    n = h2.shape[0] // SUBLANES
    by_tile = lambda i, ep: (i,)
    prev_tile = lambda i, ep: (jnp.maximum(i - 1, 0),)
    grid_spec = pltpu.PrefetchScalarGridSpec(
        num_scalar_prefetch=1,
        grid=(n // TM_ROUTE,),
        in_specs=[
            pl.BlockSpec((TM_ROUTE * TOP_K,), by_tile, memory_space=pltpu.SMEM),
            _tile_table_spec(by_tile), _tile_table_spec(by_tile), _tile_table_spec(by_tile),
            _tile_table_spec(prev_tile), _tile_table_spec(prev_tile), _tile_table_spec(prev_tile),
            pl.BlockSpec((TM_ROUTE * SUBLANES, LANES), lambda i, ep: (i, 0)),
        ],
        out_specs=pl.BlockSpec(memory_space=pl.ANY),
        scratch_shapes=[pltpu.VMEM((2, COMPACT_ROWS * SUBLANES, LANES), h2.dtype),
                        pltpu.VMEM((MOE_BLOCK * SUBLANES, LANES), h2.dtype),
                        pltpu.SemaphoreType.DMA(()), pltpu.SemaphoreType.DMA((2,))],
    )
    return pl.pallas_call(
        _scatter_kernel,
        grid_spec=grid_spec,
        out_shape=jax.ShapeDtypeStruct((n_rows * SUBLANES, LANES), h2.dtype),
        compiler_params=_cparams(("arbitrary",)),
        name="moe_scatter",
    )(end_pad, lpos_flat, cnt_flat, off_flat, gdst_flat, cnt_flat, off_flat, gdst_flat, h2)


def _expert_kernel(blk_e_ref, blk_src_ref, blk_on_ref, xs_ref, wgu_ref, bgu_ref, wdn_ref, bdn_ref, ys_ref,
                   wgu_b, wdn_b):
    i = pl.program_id(0)
    on = blk_on_ref[i] == 1

    @pl.when(jnp.logical_or(i == 0, blk_e_ref[i] != blk_e_ref[jnp.maximum(i - 1, 0)]))
    def _():
        wgu_b[...] = _bf(wgu_ref[0, 0])
        wdn_b[...] = _bf(wdn_ref[0, 0])

    @pl.when(jnp.logical_not(on))
    def _():
        ys_ref[...] = jnp.zeros(ys_ref.shape, ys_ref.dtype)

    @pl.when(on)
    def _():
        f = wdn_b.shape[0]
        gu = _dot(_bf(_load_rows_from_tiles(xs_ref, MOE_BLOCK)), wgu_b[...]) + bgu_ref[0, 0]
        gt = jnp.minimum(gu[:, :f], SWIGLU_LIMIT)
        up = jnp.clip(gu[:, f:], -SWIGLU_LIMIT, SWIGLU_LIMIT)
        act = gt * jax.nn.sigmoid(SWIGLU_ALPHA * gt) * (up + 1.0)
        _store_rows_as_tiles(ys_ref, _dot(_bf(act), wdn_b[...]) + bdn_ref[0, 0])


def _expert_call(xs, blk_e, blk_src, blk_on, wgu, bgu, wdn, bdn, layer):
    n_blocks = xs.shape[0] // (MOE_BLOCK * SUBLANES)
    d, f2 = wgu.shape[2], wgu.shape[3]
    f = wdn.shape[2]
    src = lambda i, be, bs, bo: (bs[i], 0)
    by_e = lambda i, be, bs, bo: (layer, be[i], 0, 0)
    grid_spec = pltpu.PrefetchScalarGridSpec(
        num_scalar_prefetch=3,
        grid=(n_blocks,),
        in_specs=[
            pl.BlockSpec((MOE_BLOCK * SUBLANES, LANES), src),
            pl.BlockSpec((1, 1, d, f2), by_e),
            pl.BlockSpec((1, 1, 1, f2), by_e),
            pl.BlockSpec((1, 1, f, d), by_e),
            pl.BlockSpec((1, 1, 1, d), by_e),
        ],
        out_specs=pl.BlockSpec((MOE_BLOCK * SUBLANES, LANES), lambda i, be, bs, bo: (i, 0)),
        scratch_shapes=[pltpu.VMEM((d, f2), jnp.bfloat16), pltpu.VMEM((f, d), jnp.bfloat16)],
    )
    return pl.pallas_call(
        _expert_kernel,
        grid_spec=grid_spec,
        out_shape=jax.ShapeDtypeStruct(xs.shape, jnp.float32),
        compiler_params=_cparams(("arbitrary",)),
        name="moe_experts",
    )(blk_e, blk_src, blk_on, xs, wgu, bgu, wdn, bdn)


def _combine_kernel(lpos_ref, cnt_ref, off_ref, gdst_ref, cnt_next, off_next, gdst_next,
                    x1_ref, wt_ref, g2_ref, fg_ref, ys_ref, o_ref, ybuf, buf, sem, *, final_norm):
    i = pl.program_id(0)
    slot = lax.rem(i, 2)

    def fetch_into(s):
        def make_copy(local_row, global_row):
            return pltpu.make_async_copy(_tile_rows(ys_ref, global_row, SEG_ROWS),
                                         _tile_rows(ybuf.at[s], local_row, SEG_ROWS), sem.at[s])
        return make_copy

    @pl.when(i == 0)
    def _():
        _segment_copies(cnt_ref, off_ref, gdst_ref, fetch_into(slot), _start_copy)

    @pl.when(i + 1 < pl.num_programs(0))
    def _():
        _segment_copies(cnt_next, off_next, gdst_next, fetch_into(1 - slot), _start_copy)

    _segment_copies(cnt_ref, off_ref, gdst_ref, fetch_into(slot), _wait_copy)

    def pick_token(t, c):
        for k in range(TOP_K):
            _tile_rows(buf.at[k], t, 1)[...] = _tile_rows(ybuf.at[slot], lpos_ref[t * TOP_K + k], 1)[...]
        return c

    lax.fori_loop(0, TM_ROUTE, pick_token, 0)

    wt = wt_ref[...]
    y = wt[:, 0:1] * _load_rows_from_tiles(buf.at[0], TM_ROUTE)
    for k in range(1, TOP_K):
        y = y + wt[:, k:k + 1] * _load_rows_from_tiles(buf.at[k], TM_ROUTE)
    out = x1_ref[...] + g2_ref[0] * y
    if final_norm:
        out = _rms(out, fg_ref[...])
    o_ref[...] = out


def _combine_call(x1, wts, g2, final_g, ys, lpos_flat, cnt_flat, off_flat, gdst_flat, seq, final_norm):
    n, d = x1.shape
    tpb = seq // TM_ROUTE
    n_tiles = n // TM_ROUTE
    row = lambda i: (i, 0)
    by_tile = lambda i: (i,)
    next_tile = lambda i: (jnp.minimum(i + 1, n_tiles - 1),)
    return pl.pallas_call(
        functools.partial(_combine_kernel, final_norm=final_norm),
        grid=(n_tiles,),
        in_specs=[
            pl.BlockSpec((TM_ROUTE * TOP_K,), by_tile, memory_space=pltpu.SMEM),
            _tile_table_spec(by_tile), _tile_table_spec(by_tile), _tile_table_spec(by_tile),
            _tile_table_spec(next_tile), _tile_table_spec(next_tile), _tile_table_spec(next_tile),
            pl.BlockSpec((TM_ROUTE, d), row),
            pl.BlockSpec((TM_ROUTE, LANES), row),
            pl.BlockSpec((1, 1, d), lambda i: (i // tpb, 0, 0)),
            pl.BlockSpec((1, d), lambda i: (0, 0)),
            pl.BlockSpec(memory_space=pl.ANY),
        ],
        out_specs=pl.BlockSpec((TM_ROUTE, d), row),
        out_shape=jax.ShapeDtypeStruct((n, d), jnp.float32),
        scratch_shapes=[pltpu.VMEM((2, COMPACT_ROWS * SUBLANES, LANES), jnp.float32),
                        pltpu.VMEM((TOP_K, TM_ROUTE * SUBLANES, LANES), jnp.float32),
                        pltpu.SemaphoreType.DMA((2,))],
        compiler_params=_cparams(("arbitrary",)),
        name="moe_combine",
    )(lpos_flat, cnt_flat, off_flat, gdst_flat, cnt_flat, off_flat, gdst_flat, x1, wts, g2, final_g, ys)


_O_KI = ATTN_WIDTH + KV_LATENT + IDX_HEADS * IDX_DIM
_O_WI = _O_KI + IDX_DIM
_O_POOL = _O_WI + IDX_HEADS
_O_GATE = _O_POOL + POOL_WIDTH
TR_REGROUP = 256


def _regroup_kernel(w_ref, wr_ref, wg_ref):
    w = w_ref[0]
    head = w[:, _O_KI:_O_KI + LANES]
    lane = lax.broadcasted_iota(jnp.int32, head.shape, 1)
    ki4 = head
    for rep in range(1, LANES // IDX_DIM):
        ki4 = jnp.where(lane >= rep * IDX_DIM, pltpu.roll(head, rep * IDX_DIM, 1), ki4)
    wr_ref[0, :, :_O_KI] = _bf(w[:, :_O_KI])
    wr_ref[0, :, _O_KI:_O_KI + LANES] = _bf(ki4)
    wr_ref[0, :, _O_KI + LANES:] = _bf(w[:, _O_POOL:_O_GATE])
    wg_ref[0] = _bf(w[:, _O_GATE:])


def _regroup_w_in(w_in):
    depth, d, width = w_in.shape
    w_r, w_gate = pl.pallas_call(
        _regroup_kernel,
        grid=(depth, d // TR_REGROUP),
        in_specs=[pl.BlockSpec((1, TR_REGROUP, width), lambda l, i: (l, i, 0))],
        out_specs=[pl.BlockSpec((1, TR_REGROUP, PROJ_COLS), lambda l, i: (l, i, 0)),
                   pl.BlockSpec((1, TR_REGROUP, width - _O_GATE), lambda l, i: (l, i, 0))],
        out_shape=[jax.ShapeDtypeStruct((depth, d, PROJ_COLS), jnp.bfloat16),
                   jax.ShapeDtypeStruct((depth, d, width - _O_GATE), jnp.bfloat16)],
        compiler_params=_cparams(("arbitrary", "arbitrary")),
        name="regroup_w_in",
    )(w_in)
    w_wi_t = jnp.pad(jnp.swapaxes(w_in[..., _O_WI:_O_POOL], 1, 2), ((0, 0), (0, SUBLANES - IDX_HEADS), (0, 0)))
    return w_r, _bf(w_wi_t), w_gate


def _block_diag_uk_t(w_uk):
    depth = w_uk.shape[0]
    eye = jnp.eye(N_HEADS, dtype=w_uk.dtype)
    t = jnp.einsum('zlhd,hg->zglhd', w_uk, eye)
    return _bf(t.reshape(depth, N_HEADS * KV_LATENT, N_HEADS * HEAD_DIM))


def _block_diag_uv(w_uv):
    depth = w_uv.shape[0]
    eye = jnp.eye(N_HEADS, dtype=w_uv.dtype)
    t = jnp.einsum('zlhd,hg->zhlgd', w_uv, eye)
    return _bf(t.reshape(depth, N_HEADS * KV_LATENT, N_HEADS * HEAD_DIM))


def kernel(x, c, norm1_g, norm2_g, w_ada, b_ada, w_in, kv_norm_g, w_uk, w_uv, w_pool, pool_scale, w_a_up, w_b_up, w_out, w_router, b_router, w_gu, b_gu, w_down, b_down, final_g):
    bsz, seq, d = x.shape
    depth = w_in.shape[0]
    n = bsz * seq
    assert TM_PROJ == TK and seq % TK == 0 and n % MOE_BLOCK == 0
    assert seq // CNT_ROWS <= 256
    assert d == SUBLANES * LANES

    ada = _ada_call(c, w_ada, b_ada).reshape(depth, bsz, 6, 1, d)
    w_r, w_wi_t, w_gate = _regroup_w_in(w_in)
    wukt_bd = _block_diag_uk_t(w_uk)
    wuv_bd = _block_diag_uv(w_uv)
    w_pool_b, w_a_up_b, w_b_up_b, w_out_b = _bf(w_pool), _bf(w_a_up), _bf(w_b_up), _bf(w_out)
    w_router_p = _bf(jnp.pad(w_router, ((0, 0), (0, 0), (0, LANES - N_EXPERTS))))
    b_router_p = jnp.pad(b_router, ((0, 0), (0, LANES - N_EXPERTS)), constant_values=NEG_BIG)

    n_asg = n * TOP_K
    n_blocks = -(-(n_asg + N_EXPERTS * (SEG_ROWS - 1)) // MOE_BLOCK) + N_EXPERTS
    n_rows = n_blocks * MOE_BLOCK

    xf = x.reshape(n, d)
    for l in range(depth):
        sh1, sc1, g1, sh2, sc2, g2 = [ada[l, :, j] for j in range(6)]
        q, ckv, ckv_t, qi, ki4, wi_t, apool = _proj_call(
            xf, sh1, sc1, norm1_g[l][None], w_r[l], w_wi_t[l], kv_norm_g[l][None], seq)
        o_a = _attn_call(q, qi, wi_t, ckv, ckv_t, ki4, wukt_bd[l], wuv_bd[l], bsz, seq)
        x1, h2, logits = _mix_call(xf, o_a, apool, sh1, sc1, g1, norm1_g[l][None], w_gate[l], w_pool_b[l],
                                   pool_scale[l][None], w_a_up_b[l], w_b_up_b[l], w_out_b[l],
                                   sh2, sc2, norm2_g[l][None], w_router_p[l], b_router_p[l][None], seq)
        wts, lpos, cnt_t, off_t, base_t, counts = _route_call(logits)

        sizes = counts[0, :N_EXPERTS].astype(jnp.int32)
        padded = ((sizes + (SEG_ROWS - 1) + MOE_BLOCK - 1) // MOE_BLOCK) * MOE_BLOCK
        end_pad = jnp.cumsum(padded)
        start_pad = end_pad - padded
        gdst_t = base_t + jnp.pad(start_pad, (0, LANES - N_EXPERTS))[None, None, :]
        lpos_flat = lpos[:, :TOP_K].reshape(-1)
        cnt_flat, off_flat, gdst_flat = cnt_t.reshape(-1), off_t.reshape(-1), gdst_t.reshape(-1)
        blk_start = jnp.arange(n_blocks, dtype=jnp.int32) * MOE_BLOCK
        blk_e = jnp.minimum(jnp.sum((end_pad[None, :] <= blk_start[:, None]).astype(jnp.int32), axis=1),
                            N_EXPERTS - 1)
        n_used = end_pad[-1] // MOE_BLOCK
        blk_idx = jnp.arange(n_blocks, dtype=jnp.int32)
        blk_on = (blk_idx < n_used).astype(jnp.int32)
        blk_src = jnp.minimum(blk_idx, n_used - 1).astype(jnp.int32)
        blk_e = jnp.where(blk_on == 1, blk_e, blk_e[n_used - 1])

        xs = _scatter_call(h2, lpos_flat, cnt_flat, off_flat, gdst_flat, end_pad.astype(jnp.int32), n_rows)
        ys = _expert_call(xs, blk_e, blk_src, blk_on, w_gu, b_gu[:, :, None, :], w_down, b_down[:, :, None, :], l)
        xf = _combine_call(x1, wts, g2, final_g[None], ys, lpos_flat, cnt_flat, off_flat, gdst_flat, seq,
                           final_norm=(l == depth - 1))
    return xf.reshape(bsz, seq, d)
```

```python
import functools

import jax
import jax.numpy as jnp
from jax import lax
from jax.experimental import pallas as pl
from jax.experimental.pallas import tpu as pltpu

N_HEADS = 8
HEAD_DIM = 64
ATTN_WIDTH = N_HEADS * HEAD_DIM
KV_LATENT = 128
IDX_HEADS = 4
IDX_DIM = 32
TOPK_MAX = 256
ATTN_SCALE = HEAD_DIM ** -0.5
IDX_W_SCALE = (IDX_HEADS ** -0.5) * (IDX_DIM ** -0.5)
POOL_WINDOWS = (2, 4, 8, 16)
POOL_GROUP_DIM = 128
POOL_WIDTH = len(POOL_WINDOWS) * POOL_GROUP_DIM
N_EXPERTS = 32
TOP_K = 4
MOE_BLOCK = 512
SWIGLU_LIMIT = 7.0
SWIGLU_ALPHA = 1.702
RMS_EPS = 1e-6

LANES = 128
SUBLANES = 8
POOL_HALO = 16
DMA_PRIORITIES = 2
NEG_BIG = -1e30
INT_MIN = -2 ** 31
HALF_BITS = 16
HALF_OFFSET = 2 ** (HALF_BITS - 1)

TM_PROJ = 512
TQ = 512
TK = 512
GROUP_HEADS = 2
CNT_ROWS = 64
ONES_ROWS = 16
KVT_ROWS = KV_LATENT + ONES_ROWS
LOG2E = 1.4426950408889634
TM_MIX = 256
TM_ROUTE = 512
SEG_SHIFT = 6
SEG_ROWS = 1 << SEG_SHIFT
COMPACT_ROWS = TM_ROUTE * TOP_K + N_EXPERTS * SEG_ROWS
VMEM_LIMIT = 56 * 1024 * 1024


def _cparams(sem):
    return pltpu.CompilerParams(dimension_semantics=sem, vmem_limit_bytes=VMEM_LIMIT)


def _rms(xf, g):
    return xf * lax.rsqrt(jnp.mean(xf * xf, axis=-1, keepdims=True) + RMS_EPS) * g


def _bf(v):
    return v.astype(jnp.bfloat16)


def _dot(a, b):
    return jnp.dot(a, b, preferred_element_type=jnp.float32)


def _dot_nt(a, b):
    return lax.dot_general(a, b, (((1,), (1,)), ((), ())), preferred_element_type=jnp.float32)


def _store_rows_as_tiles(ref, v):
    rows = v.shape[0]
    for s in range(SUBLANES):
        ref[pl.ds(s, rows, stride=SUBLANES), :] = v[:, s * LANES:(s + 1) * LANES]


def _tile_rows(ref, row, count):
    return ref.at[pl.ds(pl.multiple_of(row * SUBLANES, SUBLANES), count * SUBLANES), :]


def _load_rows_from_tiles(ref, rows):
    return jnp.concatenate([ref[pl.ds(s, rows, stride=SUBLANES), :] for s in range(SUBLANES)], axis=1)


def _strict_lower(n):
    return _bf(jnp.where(lax.broadcasted_iota(jnp.int32, (n, n), 1)
                         < lax.broadcasted_iota(jnp.int32, (n, n), 0), 1.0, 0.0))


def _ada_kernel(c_ref, w_ref, b_ref, o_ref):
    cf = c_ref[...]
    cond = cf * jax.nn.sigmoid(cf)
    o_ref[0] = _dot(_bf(cond), _bf(w_ref[0])) + b_ref[0]


def _ada_call(c, w_ada, b_ada):
    depth, d, n6 = w_ada.shape
    bsz = c.shape[0]
    tn = 1024
    return pl.pallas_call(
        _ada_kernel,
        grid=(depth, n6 // tn),
        in_specs=[
            pl.BlockSpec((bsz, d), lambda l, j: (0, 0)),
            pl.BlockSpec((1, d, tn), lambda l, j: (l, 0, j)),
            pl.BlockSpec((1, 1, tn), lambda l, j: (l, 0, j)),
        ],
        out_specs=pl.BlockSpec((1, bsz, tn), lambda l, j: (l, 0, j)),
        out_shape=jax.ShapeDtypeStruct((depth, bsz, n6), jnp.float32),
        compiler_params=_cparams(("arbitrary", "arbitrary")),
        name="ada",
    )(c, w_ada, b_ada.reshape(depth, 1, n6))


PROJ_COLS = ATTN_WIDTH + 3 * LANES + POOL_WIDTH


def _proj_kernel(x_ref, sh_ref, sc_ref, g_ref, w_ref, wwi_ref, kvg_ref,
                 q_ref, ckv_ref, ckvt_ref, qi_ref, ki_ref, wit_ref, ap_ref):
    h = _bf(_rms(x_ref[...], g_ref[...]) * (1.0 + sc_ref[0]) + sh_ref[0])
    p = _dot(h, w_ref[...])
    o = 0
    q_ref[...] = _bf(p[:, o:o + ATTN_WIDTH]); o += ATTN_WIDTH
    ckv = _bf(_rms(p[:, o:o + KV_LATENT], kvg_ref[...])); o += KV_LATENT
    ckv_ref[...] = ckv
    eye = _bf(jnp.where(lax.broadcasted_iota(jnp.int32, (KV_LATENT, KV_LATENT), 0)
                        == lax.broadcasted_iota(jnp.int32, (KV_LATENT, KV_LATENT), 1), 1.0, 0.0))
    ckvt_ref[0, :KV_LATENT, :] = _bf(_dot_nt(eye, ckv))
    ckvt_ref[0, KV_LATENT:, :] = jnp.ones((ONES_ROWS, TM_PROJ), jnp.bfloat16)
    qi_ref[...] = _bf(p[:, o:o + LANES]); o += LANES
    ki_ref[...] = _bf(p[:, o:o + LANES]); o += LANES
    ap_ref[...] = p[:, o:o + POOL_WIDTH]
    wit_ref[...] = _dot_nt(wwi_ref[...], h)


def _proj_call(xf, sh1, sc1, g1n, w_r, w_wi_t, kvg, seq):
    n, d = xf.shape
    tpb = seq // TM_PROJ
    row = lambda i: (i, 0)
    per_b = lambda i: (i // tpb, 0, 0)
    fixed = lambda i: (0, 0)
    return pl.pallas_call(
        _proj_kernel,
        grid=(n // TM_PROJ,),
        in_specs=[
            pl.BlockSpec((TM_PROJ, d), row),
            pl.BlockSpec((1, 1, d), per_b),
            pl.BlockSpec((1, 1, d), per_b),
            pl.BlockSpec((1, d), fixed),
            pl.BlockSpec((d, PROJ_COLS), fixed),
            pl.BlockSpec((SUBLANES, d), fixed),
            pl.BlockSpec((1, KV_LATENT), fixed),
        ],
        out_specs=[
            pl.BlockSpec((TM_PROJ, ATTN_WIDTH), row),
            pl.BlockSpec((TM_PROJ, KV_LATENT), row),
            pl.BlockSpec((1, KVT_ROWS, TM_PROJ), lambda i: (i, 0, 0)),
            pl.BlockSpec((TM_PROJ, LANES), row),
            pl.BlockSpec((TM_PROJ, LANES), row),
            pl.BlockSpec((SUBLANES, TM_PROJ), lambda i: (0, i)),
            pl.BlockSpec((TM_PROJ, POOL_WIDTH), row),
        ],
        out_shape=[
            jax.ShapeDtypeStruct((n, ATTN_WIDTH), jnp.bfloat16),
            jax.ShapeDtypeStruct((n, KV_LATENT), jnp.bfloat16),
            jax.ShapeDtypeStruct((n // TM_PROJ, KVT_ROWS, TM_PROJ), jnp.bfloat16),
            jax.ShapeDtypeStruct((n, LANES), jnp.bfloat16),
            jax.ShapeDtypeStruct((n, LANES), jnp.bfloat16),
            jax.ShapeDtypeStruct((SUBLANES, n), jnp.float32),
            jax.ShapeDtypeStruct((n, POOL_WIDTH), jnp.float32),
        ],
        compiler_params=_cparams(("arbitrary",)),
        name="in_proj",
    )(xf, sh1, sc1, g1n, w_r, w_wi_t, kvg)


def _attn_kernel(q_ref, qi_ref, wit_ref, ckv_ref, ckvt_ref, ki_ref, wukt_ref, wuv_ref, o_ref,
                 key_buf, hi_buf, lo_buf, qs_buf, acc_buf, m_buf, *, k_sel):
    qb = pl.program_id(1)
    q_start = qb * TQ
    n_kc = (q_start + TQ + TK - 1) // TK

    krow = lax.broadcasted_iota(jnp.int32, (TK, TQ), 0)
    qcol = q_start + lax.broadcasted_iota(jnp.int32, (TK, TQ), 1)

    qi = qi_ref[...]
    lane = lax.broadcasted_iota(jnp.int32, (TQ, LANES), 1)
    q4 = jnp.concatenate(
        [jnp.where((lane >= h * IDX_DIM) & (lane < (h + 1) * IDX_DIM), qi, jnp.zeros_like(qi))
         for h in range(IDX_HEADS)], axis=0)
    wit = wit_ref[...] * IDX_W_SCALE
    w_rows = [wit[h:h + 1, :] for h in range(IDX_HEADS)]

    def score_chunk(kc, carry):
        k0 = pl.multiple_of(kc * TK, TK)
        raw = _dot_nt(ki_ref[pl.ds(k0, TK), :], q4)
        score = w_rows[0] * jnp.maximum(raw[:, 0:TQ], 0.0)
        for h in range(1, IDX_HEADS):
            score = score + w_rows[h] * jnp.maximum(raw[:, h * TQ:(h + 1) * TQ], 0.0)
        score = score + 0.0
        bits = pltpu.bitcast(score, jnp.int32)
        key = jnp.where(bits < 0, bits ^ jnp.int32(0x7FFFFFFF), bits)
        key = jnp.where(krow + k0 <= qcol, key, jnp.int32(INT_MIN))
        key_buf[kc] = key
        hi_buf[kc] = (key >> HALF_BITS).astype(jnp.int16)
        return carry

    lax.fori_loop(0, n_kc, score_chunk, 0)

    def count16(plane, cand, strict):
        def body(kc, acc):
            for j in range(TK // CNT_ROWS):
                pj = plane[kc, j * CNT_ROWS:(j + 1) * CNT_ROWS, :]
                hit = (pj > cand) if strict else (pj >= cand)
                acc = acc + jnp.where(hit, jnp.bfloat16(1.0), jnp.bfloat16(0.0))
            return acc

        acc = lax.fori_loop(0, n_kc, body, jnp.zeros((CNT_ROWS, TQ), jnp.bfloat16))
        return jnp.sum(acc.astype(jnp.float32), axis=0, keepdims=True)

    def to_plane(t_u):
        return (t_u - HALF_OFFSET).astype(jnp.int16)

    def search16(plane, need):
        def bit_step(i, t_u):
            cand_u = t_u | lax.shift_left(jnp.int32(1), HALF_BITS - 1 - i)
            cnt = count16(plane, to_plane(cand_u), False)
            return jnp.where(cnt >= need, cand_u, t_u)

        return lax.fori_loop(0, HALF_BITS, bit_step, jnp.zeros((1, TQ), jnp.int32))

    hi_u = search16(hi_buf, float(k_sel))
    thr_hi = to_plane(hi_u)
    need_lo = float(k_sel) - count16(hi_buf, thr_hi, True)

    def low_plane_chunk(kc, carry):
        lo = ((key_buf[kc] & jnp.int32(HALF_OFFSET * 2 - 1)) - HALF_OFFSET).astype(jnp.int16)
        lo_buf[kc] = jnp.where(hi_buf[kc] == thr_hi, lo, jnp.int16(-HALF_OFFSET))
        return carry

    lax.fori_loop(0, n_kc, low_plane_chunk, 0)
    lo_u = search16(lo_buf, need_lo)
    thr = lax.shift_left(hi_u - HALF_OFFSET, HALF_BITS) | lo_u
    n_tie_take = need_lo - count16(lo_buf, to_plane(lo_u), True)

    qlat_t = _dot_nt(wukt_ref[...], q_ref[...]) * (ATTN_SCALE * LOG2E)
    for h in range(N_HEADS):
        qs_buf[:, h * TQ:(h + 1) * TQ] = _bf(qlat_t[h * KV_LATENT:(h + 1) * KV_LATENT, :])
    lower = _strict_lower(LANES)
    m_buf[...] = jnp.full(m_buf.shape, -3e38, jnp.float32)
    acc_buf[...] = jnp.zeros(acc_buf.shape, jnp.float32)

    def attend_chunk(kc, tie_seen):
        k0 = pl.multiple_of(kc * TK, TK)
        key = key_buf[kc]
        eq = key == thr
        eq_f = jnp.where(eq, 1.0, 0.0)
        eq_b = _bf(eq_f)
        ranks = []
        for j in range(TK // LANES):
            rows = slice(j * LANES, (j + 1) * LANES)
            ranks.append(_dot(lower, eq_b[rows]) + tie_seen)
            tie_seen = tie_seen + jnp.sum(eq_f[rows], axis=0, keepdims=True)
        tie_rank = jnp.concatenate(ranks, axis=0)
        sel = ((key > thr) | (eq & (tie_rank < n_tie_take))) & (krow + k0 <= qcol)
        bias = jnp.where(sel, 0.0, NEG_BIG)
        kv = ckv_ref[pl.ds(k0, TK), :]
        kv_t = ckvt_ref[kc]
        logits = _dot(kv, qs_buf[...])
        for g in range(N_HEADS // GROUP_HEADS):
            gcols = slice(g * GROUP_HEADS * TQ, (g + 1) * GROUP_HEADS * TQ)
            ps, alphas = [], []
            for hh in range(GROUP_HEADS):
                cols = slice((g * GROUP_HEADS + hh) * TQ, (g * GROUP_HEADS + hh + 1) * TQ)
                lg = logits[:, cols] + bias
                m_old = m_buf[:, cols]
                m_new = jnp.maximum(m_old, jnp.max(lg, axis=0, keepdims=True))
                m_buf[:, cols] = m_new
                alphas.append(jnp.exp2(m_old - m_new))
                ps.append(_bf(jnp.exp2(lg - m_new)))
            acc_buf[:, gcols] = (acc_buf[:, gcols] * jnp.concatenate(alphas, axis=1)
                                 + _dot(kv_t, jnp.concatenate(ps, axis=1)))
        return tie_seen

    lax.fori_loop(0, n_kc, attend_chunk, jnp.zeros((1, TQ), jnp.float32))

    o_lat_t = acc_buf[:KV_LATENT, :] / acc_buf[KV_LATENT:KV_LATENT + 1, :]
    stacked = jnp.concatenate(
        [o_lat_t[:, h * TQ:(h + 1) * TQ] for h in range(N_HEADS)], axis=0)
    o_ref[...] = _bf(_dot(_bf(stacked.T), wuv_ref[...]))


def _attn_call(q, qi, wi_t, ckv, ckv_t, ki4, wukt_bd, wuv_bd, bsz, seq):
    n = q.shape[0]
    nq = seq // TQ
    nkc = seq // TK
    k_sel = min(TOPK_MAX, seq // 4)
    qrow = lambda b, i: (b * nq + i, 0)
    per_b = lambda b, i: (b, 0)
    fixed = lambda b, i: (0, 0)
    return pl.pallas_call(
        functools.partial(_attn_kernel, k_sel=k_sel),
        grid=(bsz, nq),
        in_specs=[
            pl.BlockSpec((TQ, ATTN_WIDTH), qrow),
            pl.BlockSpec((TQ, LANES), qrow),
            pl.BlockSpec((SUBLANES, TQ), lambda b, i: (0, b * nq + i)),
            pl.BlockSpec((seq, KV_LATENT), per_b),
            pl.BlockSpec((nkc, KVT_ROWS, TK), lambda b, i: (b, 0, 0)),
            pl.BlockSpec((seq, LANES), per_b),
            pl.BlockSpec((N_HEADS * KV_LATENT, ATTN_WIDTH), fixed),
            pl.BlockSpec((N_HEADS * KV_LATENT, ATTN_WIDTH), fixed),
        ],
        out_specs=pl.BlockSpec((TQ, ATTN_WIDTH), qrow),
        out_shape=jax.ShapeDtypeStruct((n, ATTN_WIDTH), jnp.bfloat16),
        scratch_shapes=[
            pltpu.VMEM((nkc, TK, TQ), jnp.int32),
            pltpu.VMEM((nkc, TK, TQ), jnp.int16),
            pltpu.VMEM((nkc, TK, TQ), jnp.int16),
            pltpu.VMEM((KV_LATENT, N_HEADS * TQ), jnp.bfloat16),
            pltpu.VMEM((KVT_ROWS, N_HEADS * TQ), jnp.float32),
            pltpu.VMEM((1, N_HEADS * TQ), jnp.float32),
        ],
        compiler_params=_cparams(("arbitrary", "arbitrary")),
        name="dsa_attention",
    )(q, qi, wi_t, ckv, ckv_t, ki4, wukt_bd, wuv_bd)


def _mix_kernel(x_ref, oa_ref, ap_ref, halo_ref, sh1_ref, sc1_ref, g1_ref, n1g_ref,
                wgate_ref, wpool_ref, pscale_ref, waup_ref, wbup_ref, wout_ref,
                sh2_ref, sc2_ref, n2g_ref, wr_ref, br_ref,
                x1_ref, h2_ref, lg_ref, *, tiles_per_batch):
    i = pl.program_id(0)
    t_in_b = i % tiles_per_batch
    x = x_ref[...]
    d = x.shape[1]
    h = _rms(x, n1g_ref[...]) * (1.0 + sc1_ref[0]) + sh1_ref[0]
    gates = _dot(_bf(h), wgate_ref[...])

    pos1 = (t_in_b * TM_MIX + 1 + lax.broadcasted_iota(jnp.int32, (TM_MIX, 1), 0)).astype(jnp.float32)
    halo = jnp.where(t_in_b == 0, 0.0, halo_ref[...])
    zs = []
    for g, w in enumerate(POOL_WINDOWS):
        sl = slice(g * POOL_GROUP_DIM, (g + 1) * POOL_GROUP_DIM)
        a = ap_ref[:, sl]
        ext = jnp.concatenate([halo[:, sl], a], axis=0)
        span = 1
        while span < w:
            ext = ext + pltpu.roll(ext, span, 0)
            span *= 2
        mean = ext[POOL_HALO:] / jnp.minimum(pos1, float(w))
        zs.append(_dot(_bf(mean - a), wpool_ref[g]))
    o_b = jnp.concatenate(zs, axis=1) * pscale_ref[...]

    merged = (jax.nn.sigmoid(gates[:, :d]) * _dot(oa_ref[...], waup_ref[...])
              + jax.nn.sigmoid(gates[:, d:]) * _dot(_bf(o_b), wbup_ref[...]))
    x1 = x + g1_ref[0] * _dot(_bf(merged), wout_ref[...])
    x1_ref[...] = x1
    h2 = _rms(x1, n2g_ref[...]) * (1.0 + sc2_ref[0]) + sh2_ref[0]
    _store_rows_as_tiles(h2_ref, h2)
    lg_ref[...] = _dot(_bf(h2), wr_ref[...]) + br_ref[...]


def _mix_call(xf, o_a, apool, sh1, sc1, g1, n1g, wgate, wpool, pscale, waup, wbup, wout,
              sh2, sc2, n2g, wr, br, seq):
    n, d = xf.shape
    tpb = seq // TM_MIX
    row = lambda i: (i, 0)
    per_b = lambda i: (i // tpb, 0, 0)
    fixed = lambda i: (0, 0)
    fixed3 = lambda i: (0, 0, 0)
    halo_idx = lambda i: (jnp.maximum(i * (TM_MIX // POOL_HALO) - 1, 0), 0)
    mod = pl.BlockSpec((1, 1, d), per_b)
    return pl.pallas_call(
        functools.partial(_mix_kernel, tiles_per_batch=tpb),
        grid=(n // TM_MIX,),
        in_specs=[
            pl.BlockSpec((TM_MIX, d), row),
            pl.BlockSpec((TM_MIX, ATTN_WIDTH), row),
            pl.BlockSpec((TM_MIX, POOL_WIDTH), row),
            pl.BlockSpec((POOL_HALO, POOL_WIDTH), halo_idx),
            mod, mod, mod,
            pl.BlockSpec((1, d), fixed),
            pl.BlockSpec((d, 2 * d), fixed),
            pl.BlockSpec((len(POOL_WINDOWS), POOL_GROUP_DIM, POOL_GROUP_DIM), fixed3),
            pl.BlockSpec((1, POOL_WIDTH), fixed),
            pl.BlockSpec((ATTN_WIDTH, d), fixed),
            pl.BlockSpec((POOL_WIDTH, d), fixed),
            pl.BlockSpec((d, d), fixed),
            mod, mod,
            pl.BlockSpec((1, d), fixed),
            pl.BlockSpec((d, LANES), fixed),
            pl.BlockSpec((1, LANES), fixed),
        ],
        out_specs=[pl.BlockSpec((TM_MIX, d), row), pl.BlockSpec((TM_MIX * SUBLANES, LANES), row),
                   pl.BlockSpec((TM_MIX, LANES), row)],
        out_shape=[jax.ShapeDtypeStruct((n, d), jnp.float32),
                   jax.ShapeDtypeStruct((n * SUBLANES, LANES), jnp.float32),
                   jax.ShapeDtypeStruct((n, LANES), jnp.float32)],
        compiler_params=_cparams(("arbitrary",)),
        name="mix_merge",
    )(xf, o_a, apool, apool, sh1, sc1, g1, n1g, wgate, wpool, pscale, waup, wbup, wout,
      sh2, sc2, n2g, wr, br)


def _route_kernel(lg_ref, wt_ref, lpos_ref, cnt_ref, off_ref, base_ref, tot_ref, carry):
    @pl.when(pl.program_id(0) == 0)
    def _():
        carry[...] = jnp.zeros(carry.shape, jnp.float32)

    work = lg_ref[...]
    lane = lax.broadcasted_iota(jnp.int32, work.shape, 1).astype(jnp.float32)
    vals, idxs = [], []
    for _ in range(TOP_K):
        m = jnp.max(work, axis=1, keepdims=True)
        idx = jnp.min(jnp.where(work == m, lane, float(LANES)), axis=1, keepdims=True)
        vals.append(m)
        idxs.append(idx)
        work = jnp.where(lane == idx, -jnp.inf, work)
    exps = [jnp.exp(v - vals[0]) for v in vals]
    denom = exps[0] + exps[1] + exps[2] + exps[3]

    onehot = jnp.zeros(work.shape, jnp.float32)
    for idx in idxs:
        onehot = onehot + jnp.where(lane == idx, 1.0, 0.0)
    tm = work.shape[0]
    earlier = _dot(_strict_lower(tm), _bf(onehot))
    count = jnp.sum(onehot, axis=0, keepdims=True)
    units = jnp.floor((count + (SEG_ROWS - 1)) * (1.0 / SEG_ROWS))
    upper = _bf(jnp.where(lax.broadcasted_iota(jnp.int32, (LANES, LANES), 0)
                          < lax.broadcasted_iota(jnp.int32, (LANES, LANES), 1), 1.0, 0.0))
    seg_off = _dot(_bf(jnp.broadcast_to(units, (SUBLANES, LANES))), upper)[0:1] * SEG_ROWS
    local = earlier + seg_off

    wt_out = jnp.zeros(work.shape, jnp.float32)
    lpos_out = jnp.zeros(work.shape, jnp.int32)
    for k in range(TOP_K):
        lpos_k = jnp.sum(jnp.where(lane == idxs[k], local, 0.0), axis=1, keepdims=True)
        wt_out = jnp.where(lane == k, exps[k] / denom, wt_out)
        lpos_out = jnp.where(lane == k, lpos_k.astype(jnp.int32), lpos_out)
    wt_ref[...] = wt_out
    lpos_ref[...] = lpos_out
    cnt_ref[0] = count.astype(jnp.int32)
    off_ref[0] = seg_off.astype(jnp.int32)
    base_ref[0] = carry[...].astype(jnp.int32)
    carry[...] = carry[...] + count
    tot_ref[...] = carry[...]


def _route_call(logits):
    n = logits.shape[0]
    n_tiles = n // TM_ROUTE
    row = lambda i: (i, 0)
    tile = lambda i: (i, 0, 0)
    table = jax.ShapeDtypeStruct((n_tiles, 1, LANES), jnp.int32)
    return pl.pallas_call(
        _route_kernel,
        grid=(n_tiles,),
        in_specs=[pl.BlockSpec((TM_ROUTE, LANES), row)],
        out_specs=[pl.BlockSpec((TM_ROUTE, LANES), row)] * 2 + [pl.BlockSpec((1, 1, LANES), tile)] * 3
        + [pl.BlockSpec((1, LANES), lambda i: (0, 0))],
        out_shape=[jax.ShapeDtypeStruct((n, LANES), jnp.float32), jax.ShapeDtypeStruct((n, LANES), jnp.int32),
                   table, table, table, jax.ShapeDtypeStruct((1, LANES), jnp.float32)],
        scratch_shapes=[pltpu.VMEM((1, LANES), jnp.float32)],
        compiler_params=_cparams(("arbitrary",)),
        name="route",
    )(logits)


def _segment_copies(cnt_ref, off_ref, gdst_ref, make_copy, act):
    for e in range(N_EXPERTS):
        def per_piece(p, c, e=e):
            act(make_copy(off_ref[e] + p * SEG_ROWS, gdst_ref[e] + p * SEG_ROWS), e)
            return c

        n_pieces = lax.shift_right_logical(cnt_ref[e] + (SEG_ROWS - 1), SEG_SHIFT)
        lax.fori_loop(0, n_pieces, per_piece, 0)


def _start_copy(copy, expert):
    copy.start(priority=expert % DMA_PRIORITIES)


def _wait_copy(copy, expert):
    copy.wait()


def _scatter_kernel(endpad_ref, lpos_ref, cnt_ref, off_ref, gdst_ref, cnt_prev, off_prev, gdst_prev,
                    h2_ref, xs_ref, cbuf, zero_buf, sem_zero, sem_rows):
    i = pl.program_id(0)
    slot = lax.rem(i, 2)

    @pl.when(i == 0)
    def _():
        zero_buf[...] = jnp.zeros(zero_buf.shape, zero_buf.dtype)
        cbuf[...] = jnp.zeros(cbuf.shape, cbuf.dtype)

        def zero_block(start):
            return pltpu.make_async_copy(zero_buf, _tile_rows(xs_ref, start, MOE_BLOCK), sem_zero)

        def last_block_of(e):
            return jnp.maximum(endpad_ref[e] - MOE_BLOCK, 0)

        def start_zero(e, c):
            zero_block(last_block_of(e)).start()
            return c

        def wait_zero(e, c):
            zero_block(last_block_of(e)).wait()
            return c

        def start_tail(b, c):
            zero_block(b * MOE_BLOCK).start()
            return c

        def wait_tail(b, c):
            zero_block(b * MOE_BLOCK).wait()
            return c

        n_used = endpad_ref[N_EXPERTS - 1] // MOE_BLOCK
        n_blocks = xs_ref.shape[0] // (MOE_BLOCK * SUBLANES)
        lax.fori_loop(0, N_EXPERTS, start_zero, 0)
        lax.fori_loop(n_used, n_blocks, start_tail, 0)
        lax.fori_loop(0, N_EXPERTS, wait_zero, 0)
        lax.fori_loop(n_used, n_blocks, wait_tail, 0)

    def place_token(t, c):
        row = _tile_rows(h2_ref, t, 1)[...]
        for k in range(TOP_K):
            _tile_rows(cbuf.at[slot], lpos_ref[t * TOP_K + k], 1)[...] = row
        return c

    lax.fori_loop(0, TM_ROUTE, place_token, 0)

    def send_from(s):
        def make_copy(local_row, global_row):
            return pltpu.make_async_copy(_tile_rows(cbuf.at[s], local_row, SEG_ROWS),
                                         _tile_rows(xs_ref, global_row, SEG_ROWS), sem_rows.at[s])
        return make_copy

    @pl.when(i > 0)
    def _():
        _segment_copies(cnt_prev, off_prev, gdst_prev, send_from(1 - slot), _wait_copy)

    _segment_copies(cnt_ref, off_ref, gdst_ref, send_from(slot), _start_copy)

    @pl.when(i + 1 == pl.num_programs(0))
    def _():
        _segment_copies(cnt_ref, off_ref, gdst_ref, send_from(slot), _wait_copy)


def _tile_table_spec(index_map):
    return pl.BlockSpec((LANES,), index_map, memory_space=pltpu.SMEM)


def _scatter_call(h2, lpos_flat, cnt_flat, off_flat, gdst_flat, end_pad, n_rows):
    n = h2.shape[0] // SUBLANES
    by_tile = lambda i, ep: (i,)
    prev_tile = lambda i, ep: (jnp.maximum(i - 1, 0),)
    grid_spec = pltpu.PrefetchScalarGridSpec(
        num_scalar_prefetch=1,
        grid=(n // TM_ROUTE,),
        in_specs=[
            pl.BlockSpec((TM_ROUTE * TOP_K,), by_tile, memory_space=pltpu.SMEM),
            _tile_table_spec(by_tile), _tile_table_spec(by_tile), _tile_table_spec(by_tile),
            _tile_table_spec(prev_tile), _tile_table_spec(prev_tile), _tile_table_spec(prev_tile),
            pl.BlockSpec((TM_ROUTE * SUBLANES, LANES), lambda i, ep: (i, 0)),
        ],
        out_specs=pl.BlockSpec(memory_space=pl.ANY),
        scratch_shapes=[pltpu.VMEM((2, COMPACT_ROWS * SUBLANES, LANES), h2.dtype),
                        pltpu.VMEM((MOE_BLOCK * SUBLANES, LANES), h2.dtype),
                        pltpu.SemaphoreType.DMA(()), pltpu.SemaphoreType.DMA((2,))],
    )
    return pl.pallas_call(
        _scatter_kernel,
        grid_spec=grid_spec,
        out_shape=jax.ShapeDtypeStruct((n_rows * SUBLANES, LANES), h2.dtype),
        compiler_params=_cparams(("arbitrary",)),
        name="moe_scatter",
    )(end_pad, lpos_flat, cnt_flat, off_flat, gdst_flat, cnt_flat, off_flat, gdst_flat, h2)


def _expert_kernel(blk_e_ref, blk_src_ref, blk_on_ref, xs_ref, wgu_ref, bgu_ref, wdn_ref, bdn_ref, ys_ref,
                   wgu_b, wdn_b):
    i = pl.program_id(0)
    on = blk_on_ref[i] == 1

    @pl.when(jnp.logical_or(i == 0, blk_e_ref[i] != blk_e_ref[jnp.maximum(i - 1, 0)]))
    def _():
        wgu_b[...] = _bf(wgu_ref[0, 0])
        wdn_b[...] = _bf(wdn_ref[0, 0])

    @pl.when(jnp.logical_not(on))
    def _():
        ys_ref[...] = jnp.zeros(ys_ref.shape, ys_ref.dtype)

    @pl.when(on)
    def _():
        f = wdn_b.shape[0]
        gu = _dot(_bf(_load_rows_from_tiles(xs_ref, MOE_BLOCK)), wgu_b[...]) + bgu_ref[0, 0]
        gt = jnp.minimum(gu[:, :f], SWIGLU_LIMIT)
        up = jnp.clip(gu[:, f:], -SWIGLU_LIMIT, SWIGLU_LIMIT)
        act = gt * jax.nn.sigmoid(SWIGLU_ALPHA * gt) * (up + 1.0)
        _store_rows_as_tiles(ys_ref, _dot(_bf(act), wdn_b[...]) + bdn_ref[0, 0])


def _expert_call(xs, blk_e, blk_src, blk_on, wgu, bgu, wdn, bdn, layer):
    n_blocks = xs.shape[0] // (MOE_BLOCK * SUBLANES)
    d, f2 = wgu.shape[2], wgu.shape[3]
    f = wdn.shape[2]
    src = lambda i, be, bs, bo: (bs[i], 0)
    by_e = lambda i, be, bs, bo: (layer, be[i], 0, 0)
    grid_spec = pltpu.PrefetchScalarGridSpec(
        num_scalar_prefetch=3,
        grid=(n_blocks,),
        in_specs=[
            pl.BlockSpec((MOE_BLOCK * SUBLANES, LANES), src),
            pl.BlockSpec((1, 1, d, f2), by_e),
            pl.BlockSpec((1, 1, 1, f2), by_e),
            pl.BlockSpec((1, 1, f, d), by_e),
            pl.BlockSpec((1, 1, 1, d), by_e),
        ],
        out_specs=pl.BlockSpec((MOE_BLOCK * SUBLANES, LANES), lambda i, be, bs, bo: (i, 0)),
        scratch_shapes=[pltpu.VMEM((d, f2), jnp.bfloat16), pltpu.VMEM((f, d), jnp.bfloat16)],
    )
    return pl.pallas_call(
        _expert_kernel,
        grid_spec=grid_spec,
        out_shape=jax.ShapeDtypeStruct(xs.shape, jnp.float32),
        compiler_params=_cparams(("arbitrary",)),
        name="moe_experts",
    )(blk_e, blk_src, blk_on, xs, wgu, bgu, wdn, bdn)


def _combine_kernel(lpos_ref, cnt_ref, off_ref, gdst_ref, cnt_next, off_next, gdst_next,
                    x1_ref, wt_ref, g2_ref, fg_ref, ys_ref, o_ref, ybuf, buf, sem, *, final_norm):
    i = pl.program_id(0)
    slot = lax.rem(i, 2)

    def fetch_into(s):
        def make_copy(local_row, global_row):
            return pltpu.make_async_copy(_tile_rows(ys_ref, global_row, SEG_ROWS),
                                         _tile_rows(ybuf.at[s], local_row, SEG_ROWS), sem.at[s])
        return make_copy

    @pl.when(i == 0)
    def _():
        _segment_copies(cnt_ref, off_ref, gdst_ref, fetch_into(slot), _start_copy)

    @pl.when(i + 1 < pl.num_programs(0))
    def _():
        _segment_copies(cnt_next, off_next, gdst_next, fetch_into(1 - slot), _start_copy)

    _segment_copies(cnt_ref, off_ref, gdst_ref, fetch_into(slot), _wait_copy)

    def pick_token(t, c):
        for k in range(TOP_K):
            _tile_rows(buf.at[k], t, 1)[...] = _tile_rows(ybuf.at[slot], lpos_ref[t * TOP_K + k], 1)[...]
        return c

    lax.fori_loop(0, TM_ROUTE, pick_token, 0)

    wt = wt_ref[...]
    y = wt[:, 0:1] * _load_rows_from_tiles(buf.at[0], TM_ROUTE)
    for k in range(1, TOP_K):
        y = y + wt[:, k:k + 1] * _load_rows_from_tiles(buf.at[k], TM_ROUTE)
    out = x1_ref[...] + g2_ref[0] * y
    if final_norm:
        out = _rms(out, fg_ref[...])
    o_ref[...] = out


def _combine_call(x1, wts, g2, final_g, ys, lpos_flat, cnt_flat, off_flat, gdst_flat, seq, final_norm):
    n, d = x1.shape
    tpb = seq // TM_ROUTE
    n_tiles = n // TM_ROUTE
    row = lambda i: (i, 0)
    by_tile = lambda i: (i,)
    next_tile = lambda i: (jnp.minimum(i + 1, n_tiles - 1),)
    return pl.pallas_call(
        functools.partial(_combine_kernel, final_norm=final_norm),
        grid=(n_tiles,),
        in_specs=[
            pl.BlockSpec((TM_ROUTE * TOP_K,), by_tile, memory_space=pltpu.SMEM),
            _tile_table_spec(by_tile), _tile_table_spec(by_tile), _tile_table_spec(by_tile),
            _tile_table_spec(next_tile), _tile_table_spec(next_tile), _tile_table_spec(next_tile),
            pl.BlockSpec((TM_ROUTE, d), row),
            pl.BlockSpec((TM_ROUTE, LANES), row),
            pl.BlockSpec((1, 1, d), lambda i: (i // tpb, 0, 0)),
            pl.BlockSpec((1, d), lambda i: (0, 0)),
            pl.BlockSpec(memory_space=pl.ANY),
        ],
        out_specs=pl.BlockSpec((TM_ROUTE, d), row),
        out_shape=jax.ShapeDtypeStruct((n, d), jnp.float32),
        scratch_shapes=[pltpu.VMEM((2, COMPACT_ROWS * SUBLANES, LANES), jnp.float32),
                        pltpu.VMEM((TOP_K, TM_ROUTE * SUBLANES, LANES), jnp.float32),
                        pltpu.SemaphoreType.DMA((2,))],
        compiler_params=_cparams(("arbitrary",)),
        name="moe_combine",
    )(lpos_flat, cnt_flat, off_flat, gdst_flat, cnt_flat, off_flat, gdst_flat, x1, wts, g2, final_g, ys)


_O_KI = ATTN_WIDTH + KV_LATENT + IDX_HEADS * IDX_DIM
_O_WI = _O_KI + IDX_DIM
_O_POOL = _O_WI + IDX_HEADS
_O_GATE = _O_POOL + POOL_WIDTH
TR_REGROUP = 256


def _regroup_kernel(w_ref, wr_ref, wg_ref):
    w = w_ref[0]
    head = w[:, _O_KI:_O_KI + LANES]
    lane = lax.broadcasted_iota(jnp.int32, head.shape, 1)
    ki4 = head
    for rep in range(1, LANES // IDX_DIM):
        ki4 = jnp.where(lane >= rep * IDX_DIM, pltpu.roll(head, rep * IDX_DIM, 1), ki4)
    wr_ref[0, :, :_O_KI] = _bf(w[:, :_O_KI])
    wr_ref[0, :, _O_KI:_O_KI + LANES] = _bf(ki4)
    wr_ref[0, :, _O_KI + LANES:] = _bf(w[:, _O_POOL:_O_GATE])
    wg_ref[0] = _bf(w[:, _O_GATE:])


def _regroup_w_in(w_in):
    depth, d, width = w_in.shape
    w_r, w_gate = pl.pallas_call(
        _regroup_kernel,
        grid=(depth, d // TR_REGROUP),
        in_specs=[pl.BlockSpec((1, TR_REGROUP, width), lambda l, i: (l, i, 0))],
        out_specs=[pl.BlockSpec((1, TR_REGROUP, PROJ_COLS), lambda l, i: (l, i, 0)),
                   pl.BlockSpec((1, TR_REGROUP, width - _O_GATE), lambda l, i: (l, i, 0))],
        out_shape=[jax.ShapeDtypeStruct((depth, d, PROJ_COLS), jnp.bfloat16),
                   jax.ShapeDtypeStruct((depth, d, width - _O_GATE), jnp.bfloat16)],
        compiler_params=_cparams(("arbitrary", "arbitrary")),
        name="regroup_w_in",
    )(w_in)
    w_wi_t = jnp.pad(jnp.swapaxes(w_in[..., _O_WI:_O_POOL], 1, 2), ((0, 0), (0, SUBLANES - IDX_HEADS), (0, 0)))
    return w_r, _bf(w_wi_t), w_gate


def _block_diag_uk_t(w_uk):
    depth = w_uk.shape[0]
    eye = jnp.eye(N_HEADS, dtype=w_uk.dtype)
    t = jnp.einsum('zlhd,hg->zglhd', w_uk, eye)
    return _bf(t.reshape(depth, N_HEADS * KV_LATENT, N_HEADS * HEAD_DIM))


def _block_diag_uv(w_uv):
    depth = w_uv.shape[0]
    eye = jnp.eye(N_HEADS, dtype=w_uv.dtype)
    t = jnp.einsum('zlhd,hg->zhlgd', w_uv, eye)
    return _bf(t.reshape(depth, N_HEADS * KV_LATENT, N_HEADS * HEAD_DIM))


def kernel(x, c, norm1_g, norm2_g, w_ada, b_ada, w_in, kv_norm_g, w_uk, w_uv, w_pool, pool_scale, w_a_up, w_b_up, w_out, w_router, b_router, w_gu, b_gu, w_down, b_down, final_g):
    bsz, seq, d = x.shape
    depth = w_in.shape[0]
    n = bsz * seq
    assert TM_PROJ == TK and seq % TK == 0 and n % MOE_BLOCK == 0
    assert seq // CNT_ROWS <= 256
    assert d == SUBLANES * LANES

    ada = _ada_call(c, w_ada, b_ada).reshape(depth, bsz, 6, 1, d)
    w_r, w_wi_t, w_gate = _regroup_w_in(w_in)
    wukt_bd = _block_diag_uk_t(w_uk)
    wuv_bd = _block_diag_uv(w_uv)
    w_pool_b, w_a_up_b, w_b_up_b, w_out_b = _bf(w_pool), _bf(w_a_up), _bf(w_b_up), _bf(w_out)
    w_router_p = _bf(jnp.pad(w_router, ((0, 0), (0, 0), (0, LANES - N_EXPERTS))))
    b_router_p = jnp.pad(b_router, ((0, 0), (0, LANES - N_EXPERTS)), constant_values=NEG_BIG)

    n_asg = n * TOP_K
    n_blocks = -(-(n_asg + N_EXPERTS * (SEG_ROWS - 1)) // MOE_BLOCK) + N_EXPERTS
    n_rows = n_blocks * MOE_BLOCK

    xf = x.reshape(n, d)
    for l in range(depth):
        sh1, sc1, g1, sh2, sc2, g2 = [ada[l, :, j] for j in range(6)]
        q, ckv, ckv_t, qi, ki4, wi_t, apool = _proj_call(
            xf, sh1, sc1, norm1_g[l][None], w_r[l], w_wi_t[l], kv_norm_g[l][None], seq)
        o_a = _attn_call(q, qi, wi_t, ckv, ckv_t, ki4, wukt_bd[l], wuv_bd[l], bsz, seq)
        x1, h2, logits = _mix_call(xf, o_a, apool, sh1, sc1, g1, norm1_g[l][None], w_gate[l], w_pool_b[l],
                                   pool_scale[l][None], w_a_up_b[l], w_b_up_b[l], w_out_b[l],
                                   sh2, sc2, norm2_g[l][None], w_router_p[l], b_router_p[l][None], seq)
        wts, lpos, cnt_t, off_t, base_t, counts = _route_call(logits)

        sizes = counts[0, :N_EXPERTS].astype(jnp.int32)
        padded = ((sizes + (SEG_ROWS - 1) + MOE_BLOCK - 1) // MOE_BLOCK) * MOE_BLOCK
        end_pad = jnp.cumsum(padded)
        start_pad = end_pad - padded
        gdst_t = base_t + jnp.pad(start_pad, (0, LANES - N_EXPERTS))[None, None, :]
        lpos_flat = lpos[:, :TOP_K].reshape(-1)
        cnt_flat, off_flat, gdst_flat = cnt_t.reshape(-1), off_t.reshape(-1), gdst_t.reshape(-1)
        blk_start = jnp.arange(n_blocks, dtype=jnp.int32) * MOE_BLOCK
        blk_e = jnp.minimum(jnp.sum((end_pad[None, :] <= blk_start[:, None]).astype(jnp.int32), axis=1),
                            N_EXPERTS - 1)
        n_used = end_pad[-1] // MOE_BLOCK
        blk_idx = jnp.arange(n_blocks, dtype=jnp.int32)
        blk_on = (blk_idx < n_used).astype(jnp.int32)
        blk_src = jnp.minimum(blk_idx, n_used - 1).astype(jnp.int32)
        blk_e = jnp.where(blk_on == 1, blk_e, blk_e[n_used - 1])

        xs = _scatter_call(h2, lpos_flat, cnt_flat, off_flat, gdst_flat, end_pad.astype(jnp.int32), n_rows)
        ys = _expert_call(xs, blk_e, blk_src, blk_on, w_gu, b_gu[:, :, None, :], w_down, b_down[:, :, None, :], l)
        xf = _combine_call(x1, wts, g2, final_g[None], ys, lpos_flat, cnt_flat, off_flat, gdst_flat, seq,
                           final_norm=(l == depth - 1))
    return xf.reshape(bsz, seq, d)
```

```python
import functools

import jax
import jax.numpy as jnp
from jax import lax
from jax.experimental import pallas as pl
from jax.experimental.pallas import tpu as pltpu

N_HEADS = 8
HEAD_DIM = 64
ATTN_WIDTH = N_HEADS * HEAD_DIM
KV_LATENT = 128
IDX_HEADS = 4
IDX_DIM = 32
TOPK_MAX = 256
ATTN_SCALE = HEAD_DIM ** -0.5
IDX_W_SCALE = (IDX_HEADS ** -0.5) * (IDX_DIM ** -0.5)
POOL_WINDOWS = (2, 4, 8, 16)
POOL_GROUP_DIM = 128
POOL_WIDTH = len(POOL_WINDOWS) * POOL_GROUP_DIM
N_EXPERTS = 32
TOP_K = 4
MOE_BLOCK = 512
SWIGLU_LIMIT = 7.0
SWIGLU_ALPHA = 1.702
RMS_EPS = 1e-6

LANES = 128
SUBLANES = 8
POOL_HALO = 16
DMA_PRIORITIES = 2
NEG_BIG = -1e30
INT_MIN = -2 ** 31
HALF_BITS = 16
HALF_OFFSET = 2 ** (HALF_BITS - 1)

TM_PROJ = 512
TQ = 512
TK = 512
GROUP_HEADS = 2
CNT_ROWS = 64
ONES_ROWS = 16
KVT_ROWS = KV_LATENT + ONES_ROWS
LOG2E = 1.4426950408889634
TM_MIX = 256
TM_ROUTE = 512
SEG_SHIFT = 6
SEG_ROWS = 1 << SEG_SHIFT
COMPACT_ROWS = TM_ROUTE * TOP_K + N_EXPERTS * SEG_ROWS
VMEM_LIMIT = 56 * 1024 * 1024


def _cparams(sem):
    return pltpu.CompilerParams(dimension_semantics=sem, vmem_limit_bytes=VMEM_LIMIT)


def _rms(xf, g):
    return xf * lax.rsqrt(jnp.mean(xf * xf, axis=-1, keepdims=True) + RMS_EPS) * g


def _bf(v):
    return v.astype(jnp.bfloat16)


def _dot(a, b):
    return jnp.dot(a, b, preferred_element_type=jnp.float32)


def _dot_nt(a, b):
    return lax.dot_general(a, b, (((1,), (1,)), ((), ())), preferred_element_type=jnp.float32)


def _store_rows_as_tiles(ref, v):
    rows = v.shape[0]
    for s in range(SUBLANES):
        ref[pl.ds(s, rows, stride=SUBLANES), :] = v[:, s * LANES:(s + 1) * LANES]


def _tile_rows(ref, row, count):
    return ref.at[pl.ds(pl.multiple_of(row * SUBLANES, SUBLANES), count * SUBLANES), :]


def _load_rows_from_tiles(ref, rows):
    return jnp.concatenate([ref[pl.ds(s, rows, stride=SUBLANES), :] for s in range(SUBLANES)], axis=1)


def _strict_lower(n):
    return _bf(jnp.where(lax.broadcasted_iota(jnp.int32, (n, n), 1)
                         < lax.broadcasted_iota(jnp.int32, (n, n), 0), 1.0, 0.0))


def _ada_kernel(c_ref, w_ref, b_ref, o_ref):
    cf = c_ref[...]
    cond = cf * jax.nn.sigmoid(cf)
    o_ref[0] = _dot(_bf(cond), _bf(w_ref[0])) + b_ref[0]


def _ada_call(c, w_ada, b_ada):
    depth, d, n6 = w_ada.shape
    bsz = c.shape[0]
    tn = 1024
    return pl.pallas_call(
        _ada_kernel,
        grid=(depth, n6 // tn),
        in_specs=[
            pl.BlockSpec((bsz, d), lambda l, j: (0, 0)),
            pl.BlockSpec((1, d, tn), lambda l, j: (l, 0, j)),
            pl.BlockSpec((1, 1, tn), lambda l, j: (l, 0, j)),
        ],
        out_specs=pl.BlockSpec((1, bsz, tn), lambda l, j: (l, 0, j)),
        out_shape=jax.ShapeDtypeStruct((depth, bsz, n6), jnp.float32),
        compiler_params=_cparams(("arbitrary", "arbitrary")),
        name="ada",
    )(c, w_ada, b_ada.reshape(depth, 1, n6))


PROJ_COLS = ATTN_WIDTH + 3 * LANES + POOL_WIDTH


def _proj_kernel(x_ref, sh_ref, sc_ref, g_ref, w_ref, wwi_ref, kvg_ref,
                 q_ref, ckv_ref, ckvt_ref, qi_ref, ki_ref, wit_ref, ap_ref):
    h = _bf(_rms(x_ref[...], g_ref[...]) * (1.0 + sc_ref[0]) + sh_ref[0])
    p = _dot(h, w_ref[...])
    o = 0
    q_ref[...] = _bf(p[:, o:o + ATTN_WIDTH]); o += ATTN_WIDTH
    ckv = _bf(_rms(p[:, o:o + KV_LATENT], kvg_ref[...])); o += KV_LATENT
    ckv_ref[...] = ckv
    eye = _bf(jnp.where(lax.broadcasted_iota(jnp.int32, (KV_LATENT, KV_LATENT), 0)
                        == lax.broadcasted_iota(jnp.int32, (KV_LATENT, KV_LATENT), 1), 1.0, 0.0))
    ckvt_ref[0, :KV_LATENT, :] = _bf(_dot_nt(eye, ckv))
    ckvt_ref[0, KV_LATENT:, :] = jnp.ones((ONES_ROWS, TM_PROJ), jnp.bfloat16)
    qi_ref[...] = _bf(p[:, o:o + LANES]); o += LANES
    ki_ref[...] = _bf(p[:, o:o + LANES]); o += LANES
    ap_ref[...] = p[:, o:o + POOL_WIDTH]
    wit_ref[...] = _dot_nt(wwi_ref[...], h)


def _proj_call(xf, sh1, sc1, g1n, w_r, w_wi_t, kvg, seq):
    n, d = xf.shape
    tpb = seq // TM_PROJ
    row = lambda i: (i, 0)
    per_b = lambda i: (i // tpb, 0, 0)
    fixed = lambda i: (0, 0)
    return pl.pallas_call(
        _proj_kernel,
        grid=(n // TM_PROJ,),
        in_specs=[
            pl.BlockSpec((TM_PROJ, d), row),
            pl.BlockSpec((1, 1, d), per_b),
            pl.BlockSpec((1, 1, d), per_b),
            pl.BlockSpec((1, d), fixed),
            pl.BlockSpec((d, PROJ_COLS), fixed),
            pl.BlockSpec((SUBLANES, d), fixed),
            pl.BlockSpec((1, KV_LATENT), fixed),
        ],
        out_specs=[
            pl.BlockSpec((TM_PROJ, ATTN_WIDTH), row),
            pl.BlockSpec((TM_PROJ, KV_LATENT), row),
            pl.BlockSpec((1, KVT_ROWS, TM_PROJ), lambda i: (i, 0, 0)),
            pl.BlockSpec((TM_PROJ, LANES), row),
            pl.BlockSpec((TM_PROJ, LANES), row),
            pl.BlockSpec((SUBLANES, TM_PROJ), lambda i: (0, i)),
            pl.BlockSpec((TM_PROJ, POOL_WIDTH), row),
        ],
        out_shape=[
            jax.ShapeDtypeStruct((n, ATTN_WIDTH), jnp.bfloat16),
            jax.ShapeDtypeStruct((n, KV_LATENT), jnp.bfloat16),
            jax.ShapeDtypeStruct((n // TM_PROJ, KVT_ROWS, TM_PROJ), jnp.bfloat16),
            jax.ShapeDtypeStruct((n, LANES), jnp.bfloat16),
            jax.ShapeDtypeStruct((n, LANES), jnp.bfloat16),
            jax.ShapeDtypeStruct((SUBLANES, n), jnp.float32),
            jax.ShapeDtypeStruct((n, POOL_WIDTH), jnp.float32),
        ],
        compiler_params=_cparams(("arbitrary",)),
        name="in_proj",
    )(xf, sh1, sc1, g1n, w_r, w_wi_t, kvg)


def _attn_kernel(q_ref, qi_ref, wit_ref, ckv_ref, ckvt_ref, ki_ref, wukt_ref, wuv_ref, o_ref,
                 key_buf, hi_buf, lo_buf, qs_buf, acc_buf, m_buf, *, k_sel):
    qb = pl.program_id(1)
    q_start = qb * TQ
    n_kc = (q_start + TQ + TK - 1) // TK

    krow = lax.broadcasted_iota(jnp.int32, (TK, TQ), 0)
    qcol = q_start + lax.broadcasted_iota(jnp.int32, (TK, TQ), 1)

    qi = qi_ref[...]
    lane = lax.broadcasted_iota(jnp.int32, (TQ, LANES), 1)
    q4 = jnp.concatenate(
        [jnp.where((lane >= h * IDX_DIM) & (lane < (h + 1) * IDX_DIM), qi, jnp.zeros_like(qi))
         for h in range(IDX_HEADS)], axis=0)
    wit = wit_ref[...] * IDX_W_SCALE
    w_rows = [wit[h:h + 1, :] for h in range(IDX_HEADS)]

    def score_chunk(kc, carry):
        k0 = pl.multiple_of(kc * TK, TK)
        raw = _dot_nt(ki_ref[pl.ds(k0, TK), :], q4)
        score = w_rows[0] * jnp.maximum(raw[:, 0:TQ], 0.0)
        for h in range(1, IDX_HEADS):
            score = score + w_rows[h] * jnp.maximum(raw[:, h * TQ:(h + 1) * TQ], 0.0)
        score = score + 0.0
        bits = pltpu.bitcast(score, jnp.int32)
        key = jnp.where(bits < 0, bits ^ jnp.int32(0x7FFFFFFF), bits)
        key = jnp.where(krow + k0 <= qcol, key, jnp.int32(INT_MIN))
        key_buf[kc] = key
        hi_buf[kc] = (key >> HALF_BITS).astype(jnp.int16)
        return carry

    lax.fori_loop(0, n_kc, score_chunk, 0)

    def count16(plane, cand):
        def body(kc, acc):
            for j in range(TK // CNT_ROWS):
                pj = plane[kc, j * CNT_ROWS:(j + 1) * CNT_ROWS, :]
                acc = acc + jnp.where(pj >= cand, jnp.bfloat16(1.0), jnp.bfloat16(0.0))
            return acc

        acc = lax.fori_loop(0, n_kc, body, jnp.zeros((CNT_ROWS, TQ), jnp.bfloat16))
        return jnp.sum(acc.astype(jnp.float32), axis=0, keepdims=True)

    def to_plane(t_u):
        return (t_u - HALF_OFFSET).astype(jnp.int16)

    def search16(plane, need):
        def bit_step(i, state):
            t_u, above = state
            cand_u = t_u | lax.shift_left(jnp.int32(1), HALF_BITS - 1 - i)
            cnt = count16(plane, to_plane(cand_u))
            ok = cnt >= need
            return jnp.where(ok, cand_u, t_u), jnp.where(ok, above, cnt)

        return lax.fori_loop(0, HALF_BITS, bit_step,
                             (jnp.zeros((1, TQ), jnp.int32), jnp.zeros((1, TQ), jnp.float32)))

    hi_u, above_hi = search16(hi_buf, float(k_sel))
    thr_hi = to_plane(hi_u)
    need_lo = float(k_sel) - above_hi

    def low_plane_chunk(kc, carry):
        lo = ((key_buf[kc] & jnp.int32(HALF_OFFSET * 2 - 1)) - HALF_OFFSET).astype(jnp.int16)
        lo_buf[kc] = jnp.where(hi_buf[kc] == thr_hi, lo, jnp.int16(-HALF_OFFSET))
        return carry

    lax.fori_loop(0, n_kc, low_plane_chunk, 0)
    lo_u, above_lo = search16(lo_buf, need_lo)
    thr = lax.shift_left(hi_u - HALF_OFFSET, HALF_BITS) | lo_u
    n_tie_take = need_lo - above_lo

    qlat_t = _dot_nt(wukt_ref[...], q_ref[...]) * (ATTN_SCALE * LOG2E)
    for h in range(N_HEADS):
        qs_buf[:, h * TQ:(h + 1) * TQ] = _bf(qlat_t[h * KV_LATENT:(h + 1) * KV_LATENT, :])
    lower = _strict_lower(LANES)
    m_buf[...] = jnp.full(m_buf.shape, -3e38, jnp.float32)
    acc_buf[...] = jnp.zeros(acc_buf.shape, jnp.float32)

    def attend_chunk(kc, tie_seen):
        k0 = pl.multiple_of(kc * TK, TK)
        key = key_buf[kc]
        eq = key == thr
        eq_f = jnp.where(eq, 1.0, 0.0)
        eq_b = _bf(eq_f)
        ranks = []
        for j in range(TK // LANES):
            rows = slice(j * LANES, (j + 1) * LANES)
            ranks.append(_dot(lower, eq_b[rows]) + tie_seen)
            tie_seen = tie_seen + jnp.sum(eq_f[rows], axis=0, keepdims=True)
        tie_rank = jnp.concatenate(ranks, axis=0)
        sel = ((key > thr) | (eq & (tie_rank < n_tie_take))) & (krow + k0 <= qcol)
        bias = jnp.where(sel, 0.0, NEG_BIG)
        kv = ckv_ref[pl.ds(k0, TK), :]
        kv_t = ckvt_ref[kc]
        logits = _dot(kv, qs_buf[...])
        for g in range(N_HEADS // GROUP_HEADS):
            gcols = slice(g * GROUP_HEADS * TQ, (g + 1) * GROUP_HEADS * TQ)
            ps, alphas = [], []
            for hh in range(GROUP_HEADS):
                cols = slice((g * GROUP_HEADS + hh) * TQ, (g * GROUP_HEADS + hh + 1) * TQ)
                lg = logits[:, cols] + bias
                m_old = m_buf[:, cols]
                m_new = jnp.maximum(m_old, jnp.max(lg, axis=0, keepdims=True))
                m_buf[:, cols] = m_new
                alphas.append(jnp.exp2(m_old - m_new))
                ps.append(_bf(jnp.exp2(lg - m_new)))
            acc_buf[:, gcols] = (acc_buf[:, gcols] * jnp.concatenate(alphas, axis=1)
                                 + _dot(kv_t, jnp.concatenate(ps, axis=1)))
        return tie_seen

    lax.fori_loop(0, n_kc, attend_chunk, jnp.zeros((1, TQ), jnp.float32))

    o_lat_t = acc_buf[:KV_LATENT, :] / acc_buf[KV_LATENT:KV_LATENT + 1, :]
    stacked = jnp.concatenate(
        [o_lat_t[:, h * TQ:(h + 1) * TQ] for h in range(N_HEADS)], axis=0)
    o_ref[...] = _bf(_dot(_bf(stacked.T), wuv_ref[...]))


def _attn_call(q, qi, wi_t, ckv, ckv_t, ki4, wukt_bd, wuv_bd, bsz, seq):
    n = q.shape[0]
    nq = seq // TQ
    nkc = seq // TK
    k_sel = min(TOPK_MAX, seq // 4)
    qrow = lambda b, i: (b * nq + i, 0)
    per_b = lambda b, i: (b, 0)
    fixed = lambda b, i: (0, 0)
    return pl.pallas_call(
        functools.partial(_attn_kernel, k_sel=k_sel),
        grid=(bsz, nq),
        in_specs=[
            pl.BlockSpec((TQ, ATTN_WIDTH), qrow),
            pl.BlockSpec((TQ, LANES), qrow),
            pl.BlockSpec((SUBLANES, TQ), lambda b, i: (0, b * nq + i)),
            pl.BlockSpec((seq, KV_LATENT), per_b),
            pl.BlockSpec((nkc, KVT_ROWS, TK), lambda b, i: (b, 0, 0)),
            pl.BlockSpec((seq, LANES), per_b),
            pl.BlockSpec((N_HEADS * KV_LATENT, ATTN_WIDTH), fixed),
            pl.BlockSpec((N_HEADS * KV_LATENT, ATTN_WIDTH), fixed),
        ],
        out_specs=pl.BlockSpec((TQ, ATTN_WIDTH), qrow),
        out_shape=jax.ShapeDtypeStruct((n, ATTN_WIDTH), jnp.bfloat16),
        scratch_shapes=[
            pltpu.VMEM((nkc, TK, TQ), jnp.int32),
            pltpu.VMEM((nkc, TK, TQ), jnp.int16),
            pltpu.VMEM((nkc, TK, TQ), jnp.int16),
            pltpu.VMEM((KV_LATENT, N_HEADS * TQ), jnp.bfloat16),
            pltpu.VMEM((KVT_ROWS, N_HEADS * TQ), jnp.float32),
            pltpu.VMEM((1, N_HEADS * TQ), jnp.float32),
        ],
        compiler_params=_cparams(("arbitrary", "arbitrary")),
        name="dsa_attention",
    )(q, qi, wi_t, ckv, ckv_t, ki4, wukt_bd, wuv_bd)


def _mix_kernel(x_ref, oa_ref, ap_ref, halo_ref, sh1_ref, sc1_ref, g1_ref, n1g_ref,
                wgate_ref, wpool_ref, pscale_ref, waup_ref, wbup_ref, wout_ref,
                sh2_ref, sc2_ref, n2g_ref, wr_ref, br_ref,
                x1_ref, h2_ref, lg_ref, *, tiles_per_batch):
    i = pl.program_id(0)
    t_in_b = i % tiles_per_batch
    x = x_ref[...]
    d = x.shape[1]
    h = _rms(x, n1g_ref[...]) * (1.0 + sc1_ref[0]) + sh1_ref[0]
    gates = _dot(_bf(h), wgate_ref[...])

    pos1 = (t_in_b * TM_MIX + 1 + lax.broadcasted_iota(jnp.int32, (TM_MIX, 1), 0)).astype(jnp.float32)
    halo = jnp.where(t_in_b == 0, 0.0, halo_ref[...])
    zs = []
    for g, w in enumerate(POOL_WINDOWS):
        sl = slice(g * POOL_GROUP_DIM, (g + 1) * POOL_GROUP_DIM)
        a = ap_ref[:, sl]
        ext = jnp.concatenate([halo[:, sl], a], axis=0)
        span = 1
        while span < w:
            ext = ext + pltpu.roll(ext, span, 0)
            span *= 2
        mean = ext[POOL_HALO:] / jnp.minimum(pos1, float(w))
        zs.append(_dot(_bf(mean - a), wpool_ref[g]))
    o_b = jnp.concatenate(zs, axis=1) * pscale_ref[...]

    merged = (jax.nn.sigmoid(gates[:, :d]) * _dot(oa_ref[...], waup_ref[...])
              + jax.nn.sigmoid(gates[:, d:]) * _dot(_bf(o_b), wbup_ref[...]))
    x1 = x + g1_ref[0] * _dot(_bf(merged), wout_ref[...])
    x1_ref[...] = x1
    h2 = _rms(x1, n2g_ref[...]) * (1.0 + sc2_ref[0]) + sh2_ref[0]
    _store_rows_as_tiles(h2_ref, h2)
    lg_ref[...] = _dot(_bf(h2), wr_ref[...]) + br_ref[...]


def _mix_call(xf, o_a, apool, sh1, sc1, g1, n1g, wgate, wpool, pscale, waup, wbup, wout,
              sh2, sc2, n2g, wr, br, seq):
    n, d = xf.shape
    tpb = seq // TM_MIX
    row = lambda i: (i, 0)
    per_b = lambda i: (i // tpb, 0, 0)
    fixed = lambda i: (0, 0)
    fixed3 = lambda i: (0, 0, 0)
    halo_idx = lambda i: (jnp.maximum(i * (TM_MIX // POOL_HALO) - 1, 0), 0)
    mod = pl.BlockSpec((1, 1, d), per_b)
    return pl.pallas_call(
        functools.partial(_mix_kernel, tiles_per_batch=tpb),
        grid=(n // TM_MIX,),
        in_specs=[
            pl.BlockSpec((TM_MIX, d), row),
            pl.BlockSpec((TM_MIX, ATTN_WIDTH), row),
            pl.BlockSpec((TM_MIX, POOL_WIDTH), row),
            pl.BlockSpec((POOL_HALO, POOL_WIDTH), halo_idx),
            mod, mod, mod,
            pl.BlockSpec((1, d), fixed),
            pl.BlockSpec((d, 2 * d), fixed),
            pl.BlockSpec((len(POOL_WINDOWS), POOL_GROUP_DIM, POOL_GROUP_DIM), fixed3),
            pl.BlockSpec((1, POOL_WIDTH), fixed),
            pl.BlockSpec((ATTN_WIDTH, d), fixed),
            pl.BlockSpec((POOL_WIDTH, d), fixed),
            pl.BlockSpec((d, d), fixed),
            mod, mod,
            pl.BlockSpec((1, d), fixed),
            pl.BlockSpec((d, LANES), fixed),
            pl.BlockSpec((1, LANES), fixed),
        ],
        out_specs=[pl.BlockSpec((TM_MIX, d), row), pl.BlockSpec((TM_MIX * SUBLANES, LANES), row),
                   pl.BlockSpec((TM_MIX, LANES), row)],
        out_shape=[jax.ShapeDtypeStruct((n, d), jnp.float32),
                   jax.ShapeDtypeStruct((n * SUBLANES, LANES), jnp.float32),
                   jax.ShapeDtypeStruct((n, LANES), jnp.float32)],
        compiler_params=_cparams(("arbitrary",)),
        name="mix_merge",
    )(xf, o_a, apool, apool, sh1, sc1, g1, n1g, wgate, wpool, pscale, waup, wbup, wout,
      sh2, sc2, n2g, wr, br)


def _route_kernel(lg_ref, wt_ref, lpos_ref, cnt_ref, off_ref, base_ref, tot_ref, carry):
    @pl.when(pl.program_id(0) == 0)
    def _():
        carry[...] = jnp.zeros(carry.shape, jnp.float32)

    work = lg_ref[...]
    lane = lax.broadcasted_iota(jnp.int32, work.shape, 1).astype(jnp.float32)
    vals, idxs = [], []
    for _ in range(TOP_K):
        m = jnp.max(work, axis=1, keepdims=True)
        idx = jnp.min(jnp.where(work == m, lane, float(LANES)), axis=1, keepdims=True)
        vals.append(m)
        idxs.append(idx)
        work = jnp.where(lane == idx, -jnp.inf, work)
    exps = [jnp.exp(v - vals[0]) for v in vals]
    denom = exps[0] + exps[1] + exps[2] + exps[3]

    onehot = jnp.zeros(work.shape, jnp.float32)
    for idx in idxs:
        onehot = onehot + jnp.where(lane == idx, 1.0, 0.0)
    tm = work.shape[0]
    earlier = _dot(_strict_lower(tm), _bf(onehot))
    count = jnp.sum(onehot, axis=0, keepdims=True)
    units = jnp.floor((count + (SEG_ROWS - 1)) * (1.0 / SEG_ROWS))
    upper = _bf(jnp.where(lax.broadcasted_iota(jnp.int32, (LANES, LANES), 0)
                          < lax.broadcasted_iota(jnp.int32, (LANES, LANES), 1), 1.0, 0.0))
    seg_off = _dot(_bf(jnp.broadcast_to(units, (SUBLANES, LANES))), upper)[0:1] * SEG_ROWS
    local = earlier + seg_off

    wt_out = jnp.zeros(work.shape, jnp.float32)
    lpos_out = jnp.zeros(work.shape, jnp.int32)
    for k in range(TOP_K):
        lpos_k = jnp.sum(jnp.where(lane == idxs[k], local, 0.0), axis=1, keepdims=True)
        wt_out = jnp.where(lane == k, exps[k] / denom, wt_out)
        lpos_out = jnp.where(lane == k, lpos_k.astype(jnp.int32), lpos_out)
    wt_ref[...] = wt_out
    lpos_ref[...] = lpos_out
    cnt_ref[0] = count.astype(jnp.int32)
    off_ref[0] = seg_off.astype(jnp.int32)
    base_ref[0] = carry[...].astype(jnp.int32)
    carry[...] = carry[...] + count
    tot_ref[...] = carry[...]


def _route_call(logits):
    n = logits.shape[0]
    n_tiles = n // TM_ROUTE
    row = lambda i: (i, 0)
    tile = lambda i: (i, 0, 0)
    table = jax.ShapeDtypeStruct((n_tiles, 1, LANES), jnp.int32)
    return pl.pallas_call(
        _route_kernel,
        grid=(n_tiles,),
        in_specs=[pl.BlockSpec((TM_ROUTE, LANES), row)],
        out_specs=[pl.BlockSpec((TM_ROUTE, LANES), row)] * 2 + [pl.BlockSpec((1, 1, LANES), tile)] * 3
        + [pl.BlockSpec((1, LANES), lambda i: (0, 0))],
        out_shape=[jax.ShapeDtypeStruct((n, LANES), jnp.float32), jax.ShapeDtypeStruct((n, LANES), jnp.int32),
                   table, table, table, jax.ShapeDtypeStruct((1, LANES), jnp.float32)],
        scratch_shapes=[pltpu.VMEM((1, LANES), jnp.float32)],
        compiler_params=_cparams(("arbitrary",)),
        name="route",
    )(logits)


def _segment_copies(cnt_ref, off_ref, gdst_ref, make_copy, act):
    for e in range(N_EXPERTS):
        def per_piece(p, c, e=e):
            act(make_copy(off_ref[e] + p * SEG_ROWS, gdst_ref[e] + p * SEG_ROWS), e)
            return c

        n_pieces = lax.shift_right_logical(cnt_ref[e] + (SEG_ROWS - 1), SEG_SHIFT)
        lax.fori_loop(0, n_pieces, per_piece, 0)


def _start_copy(copy, expert):
    copy.start(priority=expert % DMA_PRIORITIES)


def _wait_copy(copy, expert):
    copy.wait()


def _scatter_kernel(endpad_ref, lpos_ref, cnt_ref, off_ref, gdst_ref, cnt_prev, off_prev, gdst_prev,
                    h2_ref, xs_ref, cbuf, zero_buf, sem_zero, sem_rows):
    i = pl.program_id(0)
    slot = lax.rem(i, 2)

    @pl.when(i == 0)
    def _():
        zero_buf[...] = jnp.zeros(zero_buf.shape, zero_buf.dtype)
        cbuf[...] = jnp.zeros(cbuf.shape, cbuf.dtype)

        def zero_block(start):
            return pltpu.make_async_copy(zero_buf, _tile_rows(xs_ref, start, MOE_BLOCK), sem_zero)

        def tail_block_of(e, back):
            return jnp.maximum(endpad_ref[e] - back * MOE_BLOCK, 0)

        def start_zero(e, c):
            zero_block(tail_block_of(e, 1)).start()
            zero_block(tail_block_of(e, 2)).start()
            return c

        def wait_zero(e, c):
            zero_block(tail_block_of(e, 1)).wait()
            zero_block(tail_block_of(e, 2)).wait()
            return c

        def start_tail(b, c):
            zero_block(b * MOE_BLOCK).start()
            return c

        def wait_tail(b, c):
            zero_block(b * MOE_BLOCK).wait()
            return c

        n_used = endpad_ref[N_EXPERTS - 1] // MOE_BLOCK
        n_blocks = xs_ref.shape[0] // (MOE_BLOCK * SUBLANES)
        lax.fori_loop(0, N_EXPERTS, start_zero, 0)
        lax.fori_loop(n_used, n_blocks, start_tail, 0)
        lax.fori_loop(0, N_EXPERTS, wait_zero, 0)
        lax.fori_loop(n_used, n_blocks, wait_tail, 0)

    def place_token(t, c):
        row = _tile_rows(h2_ref, t, 1)[...]
        for k in range(TOP_K):
            _tile_rows(cbuf.at[slot], lpos_ref[t * TOP_K + k], 1)[...] = row
        return c

    lax.fori_loop(0, TM_ROUTE, place_token, 0)

    def send_from(s):
        def make_copy(local_row, global_row):
            return pltpu.make_async_copy(_tile_rows(cbuf.at[s], local_row, SEG_ROWS),
                                         _tile_rows(xs_ref, global_row, SEG_ROWS), sem_rows.at[s])
        return make_copy

    @pl.when(i > 0)
    def _():
        _segment_copies(cnt_prev, off_prev, gdst_prev, send_from(1 - slot), _wait_copy)

    _segment_copies(cnt_ref, off_ref, gdst_ref, send_from(slot), _start_copy)

    @pl.when(i + 1 == pl.num_programs(0))
    def _():
        _segment_copies(cnt_ref, off_ref, gdst_ref, send_from(slot), _wait_copy)


def _tile_table_spec(index_map):
    return pl.BlockSpec((LANES,), index_map, memory_space=pltpu.SMEM)


def _scatter_call(h2, lpos_flat, cnt_flat, off_flat, gdst_flat, end_pad, n_rows):
    n = h2.shape[0] // SUBLANES
    by_tile = lambda i, ep: (i,)
    prev_tile = lambda i, ep: (jnp.maximum(i - 1, 0),)
    grid_spec = pltpu.PrefetchScalarGridSpec(
        num_scalar_prefetch=1,
        grid=(n // TM_ROUTE,),
        in_specs=[
            pl.BlockSpec((TM_ROUTE * TOP_K,), by_tile, memory_space=pltpu.SMEM),
            _tile_table_spec(by_tile), _tile_table_spec(by_tile), _tile_table_spec(by_tile),
            _tile_table_spec(prev_tile), _tile_table_spec(prev_tile), _tile_table_spec(prev_tile),
            pl.BlockSpec((TM_ROUTE * SUBLANES, LANES), lambda i, ep: (i, 0)),
        ],
        out_specs=pl.BlockSpec(memory_space=pl.ANY),
        scratch_shapes=[pltpu.VMEM((2, COMPACT_ROWS * SUBLANES, LANES), h2.dtype),
                        pltpu.VMEM((MOE_BLOCK * SUBLANES, LANES), h2.dtype),
                        pltpu.SemaphoreType.DMA(()), pltpu.SemaphoreType.DMA((2,))],
    )
    return pl.pallas_call(
        _scatter_kernel,
        grid_spec=grid_spec,
        out_shape=jax.ShapeDtypeStruct((n_rows * SUBLANES, LANES), h2.dtype),
        compiler_params=_cparams(("arbitrary",)),
        name="moe_scatter",
    )(end_pad, lpos_flat, cnt_flat, off_flat, gdst_flat, cnt_flat, off_flat, gdst_flat, h2)


def _expert_kernel(blk_e_ref, blk_src_ref, blk_on_ref, xs_ref, wgu_ref, bgu_ref, wdn_ref, bdn_ref, ys_ref,
                   wgu_b, wdn_b):
    i = pl.program_id(0)
    on = blk_on_ref[i] == 1

    @pl.when(jnp.logical_or(i == 0, blk_e_ref[i] != blk_e_ref[jnp.maximum(i - 1, 0)]))
    def _():
        wgu_b[...] = _bf(wgu_ref[0, 0])
        wdn_b[...] = _bf(wdn_ref[0, 0])

    @pl.when(jnp.logical_not(on))
    def _():
        ys_ref[...] = jnp.zeros(ys_ref.shape, ys_ref.dtype)

    @pl.when(on)
    def _():
        f = wdn_b.shape[0]
        gu = _dot(_bf(_load_rows_from_tiles(xs_ref, MOE_BLOCK)), wgu_b[...]) + bgu_ref[0, 0]
        gt = jnp.minimum(gu[:, :f], SWIGLU_LIMIT)
        up = jnp.clip(gu[:, f:], -SWIGLU_LIMIT, SWIGLU_LIMIT)
        act = gt * jax.nn.sigmoid(SWIGLU_ALPHA * gt) * (up + 1.0)
        _store_rows_as_tiles(ys_ref, _dot(_bf(act), wdn_b[...]) + bdn_ref[0, 0])


def _expert_call(xs, blk_e, blk_src, blk_on, wgu, bgu, wdn, bdn, layer):
    n_blocks = xs.shape[0] // (MOE_BLOCK * SUBLANES)
    d, f2 = wgu.shape[2], wgu.shape[3]
    f = wdn.shape[2]
    src = lambda i, be, bs, bo: (bs[i], 0)
    by_e = lambda i, be, bs, bo: (layer, be[i], 0, 0)
    grid_spec = pltpu.PrefetchScalarGridSpec(
        num_scalar_prefetch=3,
        grid=(n_blocks,),
        in_specs=[
            pl.BlockSpec((MOE_BLOCK * SUBLANES, LANES), src),
            pl.BlockSpec((1, 1, d, f2), by_e),
            pl.BlockSpec((1, 1, 1, f2), by_e),
            pl.BlockSpec((1, 1, f, d), by_e),
            pl.BlockSpec((1, 1, 1, d), by_e),
        ],
        out_specs=pl.BlockSpec((MOE_BLOCK * SUBLANES, LANES), lambda i, be, bs, bo: (i, 0)),
        scratch_shapes=[pltpu.VMEM((d, f2), jnp.bfloat16), pltpu.VMEM((f, d), jnp.bfloat16)],
    )
    return pl.pallas_call(
        _expert_kernel,
        grid_spec=grid_spec,
        out_shape=jax.ShapeDtypeStruct(xs.shape, jnp.float32),
        compiler_params=_cparams(("arbitrary",)),
        name="moe_experts",
    )(blk_e, blk_src, blk_on, xs, wgu, bgu, wdn, bdn)


def _combine_kernel(lpos_ref, cnt_ref, off_ref, gdst_ref, cnt_next, off_next, gdst_next,
                    x1_ref, wt_ref, g2_ref, fg_ref, ys_ref, o_ref, ybuf, buf, sem, *, final_norm):
    i = pl.program_id(0)
    slot = lax.rem(i, 2)

    def fetch_into(s):
        def make_copy(local_row, global_row):
            return pltpu.make_async_copy(_tile_rows(ys_ref, global_row, SEG_ROWS),
                                         _tile_rows(ybuf.at[s], local_row, SEG_ROWS), sem.at[s])
        return make_copy

    @pl.when(i == 0)
    def _():
        _segment_copies(cnt_ref, off_ref, gdst_ref, fetch_into(slot), _start_copy)

    @pl.when(i + 1 < pl.num_programs(0))
    def _():
        _segment_copies(cnt_next, off_next, gdst_next, fetch_into(1 - slot), _start_copy)

    _segment_copies(cnt_ref, off_ref, gdst_ref, fetch_into(slot), _wait_copy)

    def pick_token(t, c):
        for k in range(TOP_K):
            _tile_rows(buf.at[k], t, 1)[...] = _tile_rows(ybuf.at[slot], lpos_ref[t * TOP_K + k], 1)[...]
        return c

    lax.fori_loop(0, TM_ROUTE, pick_token, 0)

    wt = wt_ref[...]
    y = wt[:, 0:1] * _load_rows_from_tiles(buf.at[0], TM_ROUTE)
    for k in range(1, TOP_K):
        y = y + wt[:, k:k + 1] * _load_rows_from_tiles(buf.at[k], TM_ROUTE)
    out = x1_ref[...] + g2_ref[0] * y
    if final_norm:
        out = _rms(out, fg_ref[...])
    o_ref[...] = out


def _combine_call(x1, wts, g2, final_g, ys, lpos_flat, cnt_flat, off_flat, gdst_flat, seq, final_norm):
    n, d = x1.shape
    tpb = seq // TM_ROUTE
    n_tiles = n // TM_ROUTE
    row = lambda i: (i, 0)
    by_tile = lambda i: (i,)
    next_tile = lambda i: (jnp.minimum(i + 1, n_tiles - 1),)
    return pl.pallas_call(
        functools.partial(_combine_kernel, final_norm=final_norm),
        grid=(n_tiles,),
        in_specs=[
            pl.BlockSpec((TM_ROUTE * TOP_K,), by_tile, memory_space=pltpu.SMEM),
            _tile_table_spec(by_tile), _tile_table_spec(by_tile), _tile_table_spec(by_tile),
            _tile_table_spec(next_tile), _tile_table_spec(next_tile), _tile_table_spec(next_tile),
            pl.BlockSpec((TM_ROUTE, d), row),
            pl.BlockSpec((TM_ROUTE, LANES), row),
            pl.BlockSpec((1, 1, d), lambda i: (i // tpb, 0, 0)),
            pl.BlockSpec((1, d), lambda i: (0, 0)),
            pl.BlockSpec(memory_space=pl.ANY),
        ],
        out_specs=pl.BlockSpec((TM_ROUTE, d), row),
        out_shape=jax.ShapeDtypeStruct((n, d), jnp.float32),
        scratch_shapes=[pltpu.VMEM((2, COMPACT_ROWS * SUBLANES, LANES), jnp.float32),
                        pltpu.VMEM((TOP_K, TM_ROUTE * SUBLANES, LANES), jnp.float32),
                        pltpu.SemaphoreType.DMA((2,))],
        compiler_params=_cparams(("arbitrary",)),
        name="moe_combine",
    )(lpos_flat, cnt_flat, off_flat, gdst_flat, cnt_flat, off_flat, gdst_flat, x1, wts, g2, final_g, ys)


_O_KI = ATTN_WIDTH + KV_LATENT + IDX_HEADS * IDX_DIM
_O_WI = _O_KI + IDX_DIM
_O_POOL = _O_WI + IDX_HEADS
_O_GATE = _O_POOL + POOL_WIDTH
TR_REGROUP = 256


def _regroup_kernel(w_ref, wr_ref, wg_ref):
    w = w_ref[0]
    head = w[:, _O_KI:_O_KI + LANES]
    lane = lax.broadcasted_iota(jnp.int32, head.shape, 1)
    ki4 = head
    for rep in range(1, LANES // IDX_DIM):
        ki4 = jnp.where(lane >= rep * IDX_DIM, pltpu.roll(head, rep * IDX_DIM, 1), ki4)
    wr_ref[0, :, :_O_KI] = _bf(w[:, :_O_KI])
    wr_ref[0, :, _O_KI:_O_KI + LANES] = _bf(ki4)
    wr_ref[0, :, _O_KI + LANES:] = _bf(w[:, _O_POOL:_O_GATE])
    wg_ref[0] = _bf(w[:, _O_GATE:])


def _regroup_w_in(w_in):
    depth, d, width = w_in.shape
    w_r, w_gate = pl.pallas_call(
        _regroup_kernel,
        grid=(depth, d // TR_REGROUP),
        in_specs=[pl.BlockSpec((1, TR_REGROUP, width), lambda l, i: (l, i, 0))],
        out_specs=[pl.BlockSpec((1, TR_REGROUP, PROJ_COLS), lambda l, i: (l, i, 0)),
                   pl.BlockSpec((1, TR_REGROUP, width - _O_GATE), lambda l, i: (l, i, 0))],
        out_shape=[jax.ShapeDtypeStruct((depth, d, PROJ_COLS), jnp.bfloat16),
                   jax.ShapeDtypeStruct((depth, d, width - _O_GATE), jnp.bfloat16)],
        compiler_params=_cparams(("arbitrary", "arbitrary")),
        name="regroup_w_in",
    )(w_in)
    w_wi_t = jnp.pad(jnp.swapaxes(w_in[..., _O_WI:_O_POOL], 1, 2), ((0, 0), (0, SUBLANES - IDX_HEADS), (0, 0)))
    return w_r, _bf(w_wi_t), w_gate


def _block_diag_uk_t(w_uk):
    depth = w_uk.shape[0]
    eye = jnp.eye(N_HEADS, dtype=w_uk.dtype)
    t = jnp.einsum('zlhd,hg->zglhd', w_uk, eye)
    return _bf(t.reshape(depth, N_HEADS * KV_LATENT, N_HEADS * HEAD_DIM))


def _block_diag_uv(w_uv):
    depth = w_uv.shape[0]
    eye = jnp.eye(N_HEADS, dtype=w_uv.dtype)
    t = jnp.einsum('zlhd,hg->zhlgd', w_uv, eye)
    return _bf(t.reshape(depth, N_HEADS * KV_LATENT, N_HEADS * HEAD_DIM))


def kernel(x, c, norm1_g, norm2_g, w_ada, b_ada, w_in, kv_norm_g, w_uk, w_uv, w_pool, pool_scale, w_a_up, w_b_up, w_out, w_router, b_router, w_gu, b_gu, w_down, b_down, final_g):
    bsz, seq, d = x.shape
    depth = w_in.shape[0]
    n = bsz * seq
    assert TM_PROJ == TK and seq % TK == 0 and n % MOE_BLOCK == 0
    assert seq // CNT_ROWS <= 256
    assert d == SUBLANES * LANES

    ada = _ada_call(c, w_ada, b_ada).reshape(depth, bsz, 6, 1, d)
    w_r, w_wi_t, w_gate = _regroup_w_in(w_in)
    wukt_bd = _block_diag_uk_t(w_uk)
    wuv_bd = _block_diag_uv(w_uv)
    w_pool_b, w_a_up_b, w_b_up_b, w_out_b = _bf(w_pool), _bf(w_a_up), _bf(w_b_up), _bf(w_out)
    w_router_p = _bf(jnp.pad(w_router, ((0, 0), (0, 0), (0, LANES - N_EXPERTS))))
    b_router_p = jnp.pad(b_router, ((0, 0), (0, LANES - N_EXPERTS)), constant_values=NEG_BIG)

    n_asg = n * TOP_K
    n_blocks = -(-(n_asg + N_EXPERTS * (SEG_ROWS - 1)) // MOE_BLOCK) + N_EXPERTS
    n_rows = n_blocks * MOE_BLOCK

    xf = x.reshape(n, d)
    for l in range(depth):
        sh1, sc1, g1, sh2, sc2, g2 = [ada[l, :, j] for j in range(6)]
        q, ckv, ckv_t, qi, ki4, wi_t, apool = _proj_call(
            xf, sh1, sc1, norm1_g[l][None], w_r[l], w_wi_t[l], kv_norm_g[l][None], seq)
        o_a = _attn_call(q, qi, wi_t, ckv, ckv_t, ki4, wukt_bd[l], wuv_bd[l], bsz, seq)
        x1, h2, logits = _mix_call(xf, o_a, apool, sh1, sc1, g1, norm1_g[l][None], w_gate[l], w_pool_b[l],
                                   pool_scale[l][None], w_a_up_b[l], w_b_up_b[l], w_out_b[l],
                                   sh2, sc2, norm2_g[l][None], w_router_p[l], b_router_p[l][None], seq)
        wts, lpos, cnt_t, off_t, base_t, counts = _route_call(logits)

        sizes = counts[0, :N_EXPERTS].astype(jnp.int32)
        padded = ((sizes + (SEG_ROWS - 1) + MOE_BLOCK - 1) // MOE_BLOCK) * MOE_BLOCK
        end_pad = jnp.cumsum(padded)
        start_pad = end_pad - padded
        gdst_t = base_t + jnp.pad(start_pad, (0, LANES - N_EXPERTS))[None, None, :]
        lpos_flat = lpos[:, :TOP_K].reshape(-1)
        cnt_flat, off_flat, gdst_flat = cnt_t.reshape(-1), off_t.reshape(-1), gdst_t.reshape(-1)
        blk_start = jnp.arange(n_blocks, dtype=jnp.int32) * MOE_BLOCK
        blk_e = jnp.minimum(jnp.sum((end_pad[None, :] <= blk_start[:, None]).astype(jnp.int32), axis=1),
                            N_EXPERTS - 1)
        n_used = end_pad[-1] // MOE_BLOCK
        blk_idx = jnp.arange(n_blocks, dtype=jnp.int32)
        blk_on = (blk_idx < n_used).astype(jnp.int32)
        blk_src = jnp.minimum(blk_idx, n_used - 1).astype(jnp.int32)
        blk_e = jnp.where(blk_on == 1, blk_e, blk_e[n_used - 1])

        xs = _scatter_call(h2, lpos_flat, cnt_flat, off_flat, gdst_flat, end_pad.astype(jnp.int32), n_rows)
        ys = _expert_call(xs, blk_e, blk_src, blk_on, w_gu, b_gu[:, :, None, :], w_down, b_down[:, :, None, :], l)
        xf = _combine_call(x1, wts, g2, final_g[None], ys, lpos_flat, cnt_flat, off_flat, gdst_flat, seq,
                           final_norm=(l == depth - 1))
    return xf.reshape(bsz, seq, d)
```
